```python
import math
import jax, jax.numpy as jnp
from jax import lax
import numpy as np

D_MODEL = 2048
BATCH = 2
SEQ = 8192
DEPTH = 2

N_EVEN = (DEPTH + 1) // 2
N_ODD = DEPTH // 2
RMS_EPS = 1e-6

HG_HEADS = 8
HG_KDIM = 128
HG_VDIM = D_MODEL // 2 // HG_HEADS
HG_CHUNK = 64
HG_K_TOTAL = HG_HEADS * HG_KDIM
HG_V_TOTAL = HG_HEADS * HG_VDIM

SB_HEADS = 8
SB_HEAD_DIM = D_MODEL // 2 // SB_HEADS
SB_BLOCK = 128
SB_TOTAL = SB_HEADS * SB_HEAD_DIM

AB_SIZES = (HG_K_TOTAL, HG_K_TOTAL, HG_V_TOTAL, HG_V_TOTAL, SB_TOTAL, SB_TOTAL, SB_TOTAL)
AB_IN_DIM = sum(AB_SIZES)
AB_SPLITS = tuple(int(v) for v in np.cumsum(AB_SIZES)[:-1])
MIX_WIDTH = HG_V_TOTAL + SB_TOTAL

POOL_WINDOWS = (2, 4, 8, 16)
N_POOL_GROUPS = len(POOL_WINDOWS)
POOL_GROUP = D_MODEL // N_POOL_GROUPS

FFN_HIDDEN = -(-8 * D_MODEL // (3 * 256)) * 256

kernel_name = 'hgrn2_stickbreak_pool_hybrid'


def rms_norm(x, gain):
    x32 = x.astype(jnp.float32)
    y = x32 * lax.rsqrt(jnp.mean(x32 * x32, axis=-1, keepdims=True) + RMS_EPS)
    return (y * gain.astype(jnp.float32)).astype(x.dtype)


def swiglu(h, w_gate, w_up, w_down):
    return (jax.nn.silu(h @ w_gate) * (h @ w_up)) @ w_down


def hgrn2_chunked(q, log_f, k, v):
    b, s, h, dk = q.shape
    dv = v.shape[-1]
    nc = s // HG_CHUNK

    def to_chunks(t):
        return t.reshape(b, nc, HG_CHUNK, h, t.shape[-1]).transpose(1, 0, 3, 2, 4)

    causal = jnp.tril(jnp.ones((HG_CHUNK, HG_CHUNK), dtype=bool))

    def step(state, inp):
        qc, gc, kc, vc = inp
        g_cum = jnp.cumsum(gc, axis=-2)
        o_inter = jnp.einsum('bhtk,bhkv->bhtv', qc * jnp.exp(g_cum), state)
        diff = g_cum[:, :, :, None, :] - g_cum[:, :, None, :, :]
        decay = jnp.exp(jnp.where(causal[:, :, None], diff, -jnp.inf))
        att = jnp.einsum('bhtk,bhtsk,bhsk->bhts', qc, decay, kc)
        o_intra = jnp.einsum('bhts,bhsv->bhtv', att, vc)
        g_last = g_cum[:, :, -1:, :]
        new_state = (jnp.exp(g_last[:, :, 0, :])[..., None] * state
                     + jnp.einsum('bhsk,bhsv->bhkv', kc * jnp.exp(g_last - g_cum), vc))
        return new_state, o_inter + o_intra

    s0 = jnp.zeros((b, h, dk, dv), jnp.float32)
    _, o = lax.scan(step, s0, (to_chunks(q), to_chunks(log_f), to_chunks(k), to_chunks(v)))
    return o.transpose(1, 0, 3, 2, 4).reshape(b, s, h, dv)


def stick_breaking_attention(q, k, v):
    b, h, s, d = q.shape
    nb = s // SB_BLOCK
    scale = 1.0 / math.sqrt(d)
    q_blocks = q.reshape(b, h, nb, SB_BLOCK, d).transpose(2, 0, 1, 3, 4)
    key_pos = jnp.arange(s)

    def block(args):
        qb, start = args
        z = jnp.einsum('bhqd,bhsd->bhqs', qb, k).astype(jnp.float32) * scale
        query_pos = start + jnp.arange(SB_BLOCK)
        mask = key_pos[None, :] < query_pos[:, None]
        log_keep = jnp.where(mask, jax.nn.log_sigmoid(-z), 0.0)
        later = lax.cumsum(log_keep, axis=3, reverse=True) - log_keep
        weights = jnp.where(mask, jnp.exp(jax.nn.log_sigmoid(z) + later), 0.0)
        return jnp.einsum('bhqs,bhsd->bhqd', weights.astype(v.dtype), v)

    starts = jnp.arange(nb, dtype=jnp.int32) * SB_BLOCK
    o = lax.map(block, (q_blocks, starts))
    return o.transpose(1, 2, 0, 3, 4).reshape(b, h, s, d)


def hybrid_ab_mixer(h, w_in, lower_bound, hg_out_norm, w_out):
    b, s, _ = h.shape
    proj = h @ w_in
    qa, fa, ia, ga, qb, kb, vb = jnp.split(proj, AB_SPLITS, axis=-1)

    f = lower_bound + (1.0 - lower_bound) * jax.nn.sigmoid(fa.astype(jnp.float32))
    heads_k = lambda t: t.reshape(b, s, HG_HEADS, HG_KDIM)
    o_a = hgrn2_chunked(heads_k(jax.nn.silu(qa.astype(jnp.float32))),
                        heads_k(jnp.log(f)),
                        heads_k(1.0 - f),
                        ia.astype(jnp.float32).reshape(b, s, HG_HEADS, HG_VDIM))
    o_a = rms_norm(o_a, hg_out_norm) * jax.nn.silu(ga.astype(jnp.float32)).reshape(b, s, HG_HEADS, HG_VDIM)
    o_a = o_a.reshape(b, s, HG_V_TOTAL).astype(h.dtype)

    heads_b = lambda t: t.reshape(b, s, SB_HEADS, SB_HEAD_DIM).transpose(0, 2, 1, 3)
    o_b = stick_breaking_attention(heads_b(qb), heads_b(kb), heads_b(vb))
    o_b = o_b.transpose(0, 2, 1, 3).reshape(b, s, SB_TOTAL).astype(h.dtype)

    return jnp.concatenate([o_a, o_b], axis=-1) @ w_out


def multiscale_pool_mixer(h, w_groups, scale):
    b, s, d = h.shape
    hg = h.astype(jnp.float32).reshape(b, s, N_POOL_GROUPS, POOL_GROUP)
    prefix = jnp.concatenate([jnp.zeros((b, 1, N_POOL_GROUPS, POOL_GROUP), jnp.float32),
                              jnp.cumsum(hg, axis=1)], axis=1)
    pos = jnp.arange(s)
    outs = []
    for gi, w in enumerate(POOL_WINDOWS):
        lo = jnp.maximum(pos + 1 - w, 0)
        window_sum = prefix[:, 1:, gi] - prefix[:, lo, gi]
        count = jnp.minimum(pos + 1, w).astype(jnp.float32)
        outs.append(window_sum / count[None, :, None] - hg[:, :, gi])
    pooled = jnp.stack(outs, axis=2).astype(h.dtype)
    mixed = jnp.einsum('bsgc,gcd->bsgd', pooled, w_groups)
    return mixed.reshape(b, s, d) * scale


def setup_inputs(seed: int = 0) -> dict:
    key = jax.random.key(seed)
    ks = jax.random.split(key, 14)
    f32 = jnp.float32
    nrm = lambda k, shape, fan_in: jax.random.normal(k, shape, f32) * (fan_in ** -0.5)
    return {
        'x': jax.random.normal(ks[0], (BATCH, SEQ, D_MODEL), f32),
        'mix_norm': 1.0 + 0.02 * jax.random.normal(ks[1], (DEPTH, D_MODEL), f32),
        'ffn_norm': 1.0 + 0.02 * jax.random.normal(ks[2], (DEPTH, D_MODEL), f32),
        'final_norm': 1.0 + 0.02 * jax.random.normal(ks[3], (D_MODEL,), f32),
        'ab_w_in': nrm(ks[4], (N_EVEN, D_MODEL, AB_IN_DIM), D_MODEL),
        'lb_logits': 0.1 * jax.random.normal(ks[5], (N_EVEN + 1, HG_K_TOTAL), f32),
        'hg_out_norm': 1.0 + 0.02 * jax.random.normal(ks[6], (N_EVEN, HG_VDIM), f32),
        'ab_w_out': nrm(ks[7], (N_EVEN, MIX_WIDTH, D_MODEL), MIX_WIDTH),
        'pool_w': nrm(ks[8], (N_ODD, N_POOL_GROUPS, POOL_GROUP, POOL_GROUP), POOL_GROUP),
        'pool_scale': 1.0 + 0.02 * jax.random.normal(ks[9], (N_ODD, D_MODEL), f32),
        'ffn_w_gate': nrm(ks[10], (DEPTH, D_MODEL, FFN_HIDDEN), D_MODEL),
        'ffn_w_up': nrm(ks[11], (DEPTH, D_MODEL, FFN_HIDDEN), D_MODEL),
        'ffn_w_down': nrm(ks[12], (DEPTH, FFN_HIDDEN, D_MODEL), FFN_HIDDEN),
    }


def reference(x, mix_norm, ffn_norm, final_norm, ab_w_in, lb_logits, hg_out_norm, ab_w_out,
              pool_w, pool_scale, ffn_w_gate, ffn_w_up, ffn_w_down):
    lower_bounds = jnp.cumsum(jax.nn.softmax(lb_logits.astype(jnp.float32), axis=0), axis=0)
    for layer in range(DEPTH):
        h = rms_norm(x, mix_norm[layer])
        i = layer // 2
        if layer % 2 == 0:
            mix = hybrid_ab_mixer(h, ab_w_in[i], lower_bounds[i], hg_out_norm[i], ab_w_out[i])
        else:
            mix = multiscale_pool_mixer(h, pool_w[i], pool_scale[i])
        x = x + mix.astype(x.dtype)
        h = rms_norm(x, ffn_norm[layer])
        x = x + swiglu(h, ffn_w_gate[layer], ffn_w_up[layer], ffn_w_down[layer]).astype(x.dtype)
    return rms_norm(x, final_norm)
```

```python
import functools
import math

import jax
import jax.numpy as jnp
from jax import lax
from jax.experimental import pallas as pl
from jax.experimental.pallas import tpu as pltpu

F32 = jnp.float32
BF16 = jnp.bfloat16

RMS_EPS = 1e-6
HEAD_DIM = 128
HG_HEADS = 8
SB_HEADS = 8
POOL_WINDOWS = (2, 4, 8, 16)

V7X_VMEM_BYTES = 64 * 1024 * 1024
V7X_VMEM_USABLE_BYTES = 56 * 1024 * 1024
F32_SUBLANES = 8

HG_CHUNK = 128
HG_STEP_TOKENS = 512
SB_BLOCK = 256
PROJ_ROWS = 1024
PROJ_COLS = 1024
OUT_PROJ_ROWS = 512
FFN_ROWS = 512
FFN_HIDDEN_TILE = 512
POOL_ROWS = 512
POOL_HALO = 16


def _vmem_limit(pipelined_bytes, resident_bytes):
    need = 2 * pipelined_bytes + resident_bytes
    assert need <= V7X_VMEM_USABLE_BYTES, (need, V7X_VMEM_USABLE_BYTES)
    return int(min(V7X_VMEM_USABLE_BYTES, max(need, 16 * 1024 * 1024)))


def _nbytes(shape, dtype):
    return math.prod(shape) * jnp.dtype(dtype).itemsize


def _rms_norm(x, gain):
    ms = jnp.mean(x * x, axis=-1, keepdims=True)
    return x * lax.rsqrt(ms + RMS_EPS) * gain


def _dot(a, b):
    return jnp.dot(a, b, preferred_element_type=F32)


def _dot_nt(a, b):
    return lax.dot_general(a, b, (((1,), (1,)), ((), ())), preferred_element_type=F32)


def _silu(x):
    return x * jax.nn.sigmoid(x)


def _norm_proj_kernel(x_ref, g_ref, w_ref, cs_ref, o_ref, h_ref):
    @pl.when(pl.program_id(1) == 0)
    def _():
        h_ref[...] = _rms_norm(x_ref[...], g_ref[...]).astype(BF16)

    o_ref[...] = (_dot(h_ref[...], w_ref[...]) * cs_ref[...]).astype(o_ref.dtype)


def _norm_proj(x2, gain, w, col_scale, out_dtype, name):
    n_tok, d = x2.shape
    n_out = w.shape[1]
    tm, tn = PROJ_ROWS, PROJ_COLS
    assert n_tok % tm == 0 and n_out % tn == 0
    pipelined = (_nbytes((tm, d), F32) + _nbytes((d, tn), BF16) + _nbytes((tm, tn), out_dtype)
                 + _nbytes((1, d), F32) + _nbytes((1, tn), F32))
    resident = _nbytes((tm, d), BF16) + 2 * _nbytes((tm, tn), F32)
    return pl.pallas_call(
        _norm_proj_kernel,
        out_shape=jax.ShapeDtypeStruct((n_tok, n_out), out_dtype),
        grid=(n_tok // tm, n_out // tn),
        in_specs=[
            pl.BlockSpec((tm, d), lambda i, j: (i, 0)),
            pl.BlockSpec((1, d), lambda i, j: (0, 0)),
            pl.BlockSpec((d, tn), lambda i, j: (0, j)),
            pl.BlockSpec((1, tn), lambda i, j: (0, j)),
        ],
        out_specs=pl.BlockSpec((tm, tn), lambda i, j: (i, j)),
        scratch_shapes=[pltpu.VMEM((tm, d), BF16)],
        compiler_params=pltpu.CompilerParams(
            dimension_semantics=("arbitrary", "arbitrary"),
            vmem_limit_bytes=_vmem_limit(pipelined, resident)),
        name=name,
    )(x2, gain, w, col_scale)


def _norm_proj_t_kernel(x_ref, g_ref, wt_ref, o_ref, *, key_block):
    h = _rms_norm(x_ref[...], g_ref[...]).astype(BF16)
    vt = _dot_nt(wt_ref[...], h)
    for kb in range(o_ref.shape[1]):
        o_ref[0, kb] = vt[:, kb * key_block:(kb + 1) * key_block].astype(o_ref.dtype)


def _norm_proj_t(x2, gain, wt, batch, seq, key_block, name):
    n_tok, d = x2.shape
    n_out = wt.shape[0]
    tm = PROJ_ROWS
    assert seq % tm == 0 and tm % key_block == 0
    tiles_per_seq = seq // tm
    kb_per_tile = tm // key_block
    pipelined = (_nbytes((tm, d), F32) + _nbytes((n_out, d), BF16) + _nbytes((n_out, tm), BF16)
                 + _nbytes((1, d), F32))
    resident = _nbytes((tm, d), BF16) + _nbytes((tm, d), F32) + 2 * _nbytes((n_out, tm), F32)
    return pl.pallas_call(
        functools.partial(_norm_proj_t_kernel, key_block=key_block),
        out_shape=jax.ShapeDtypeStruct((batch, seq // key_block, n_out, key_block), BF16),
        grid=(n_tok // tm,),
        in_specs=[
            pl.BlockSpec((tm, d), lambda i: (i, 0)),
            pl.BlockSpec((1, d), lambda i: (0, 0)),
            pl.BlockSpec((n_out, d), lambda i: (0, 0)),
        ],
        out_specs=pl.BlockSpec((1, kb_per_tile, n_out, key_block),
                               lambda i: (i // tiles_per_seq, i % tiles_per_seq, 0, 0)),
        compiler_params=pltpu.CompilerParams(
            dimension_semantics=("arbitrary",),
            vmem_limit_bytes=_vmem_limit(pipelined, resident)),
        name=name,
    )(x2, gain, wt)


def _split3_bf16(x):
    p1 = x.astype(BF16)
    r = x - p1.astype(F32)
    p2 = r.astype(BF16)
    r = r - p2.astype(F32)
    return p1, p2, r.astype(BF16)


def _sublane_broadcast(x, group, row):
    rows, lanes = x.shape
    x3 = x.reshape(rows // group, group, lanes)
    return jnp.broadcast_to(x3[:, row:row + 1, :], x3.shape).reshape(rows, lanes)


def _hgrn_kernel(lbl_ref, gn_ref, tri_ref, qa_ref, fa_ref, ia_ref, ga_ref, o_ref, st_ref, *, layer_row):
    c = HG_CHUNK
    n_levels = c.bit_length() - 1

    @pl.when(pl.program_id(2) == 0)
    def _():
        st_ref[...] = jnp.zeros_like(st_ref)

    logits = lbl_ref[...]
    e = jnp.exp(logits - jnp.max(logits, axis=0, keepdims=True))
    lower = jnp.sum(e[:layer_row + 1], axis=0, keepdims=True) / jnp.sum(e, axis=0, keepdims=True)

    row = lax.broadcasted_iota(jnp.int32, (c, HEAD_DIM), 0)
    t_idx = lax.broadcasted_iota(jnp.int32, (c, c), 0)
    s_idx = lax.broadcasted_iota(jnp.int32, (c, c), 1)
    t_xor_s = t_idx ^ s_idx
    tri = tri_ref[...]
    out_gain = gn_ref[...]

    def chunk_body(j, carry):
        sl = pl.ds(pl.multiple_of(j * c, c), c)
        q = _silu(qa_ref[0, sl, :])
        f = lower + (1.0 - lower) * jax.nn.sigmoid(fa_ref[0, sl, :])
        logf = jnp.log(f)
        k = 1.0 - f
        v = ia_ref[0, sl, :]
        v16 = v.astype(BF16)

        p1, p2, p3 = _split3_bf16(logf)
        g = _dot(tri, p1) + _dot(tri, p2) + _dot(tri, p3)

        att = jnp.zeros((c, c), F32)
        for lvl in range(n_levels):
            half = 1 << lvl
            upper = ((row >> lvl) & 1) == 1
            if lvl == 0:
                q_t = jnp.where(upper, q * f, 0.0)
                k_t = jnp.where(upper, 0.0, k)
            else:
                if 2 * half >= F32_SUBLANES:
                    g_ref_row = _sublane_broadcast(g, 2 * half, half - 1)
                else:
                    lo = _sublane_broadcast(g, F32_SUBLANES, half - 1)
                    hi = _sublane_broadcast(g, F32_SUBLANES, 2 * half + half - 1)
                    g_ref_row = jnp.where((row & (F32_SUBLANES - 1)) < 2 * half, lo, hi)
                d = g - g_ref_row
                ex = jnp.exp(jnp.where(upper, d, -d))
                q_t = jnp.where(upper, q * ex, 0.0)
                k_t = jnp.where(upper, 0.0, k * ex)
            a = _dot_nt(q_t.astype(BF16), k_t.astype(BF16))
            att = jnp.where((t_xor_s >> lvl) == 1, a, att)
        att = jnp.where(t_idx == s_idx, jnp.sum(q * k, axis=-1, keepdims=True), att)

        g_last = g[c - 1:c, :]
        st = st_ref[...]
        o = _dot_nt((q * jnp.exp(g)).astype(BF16), st.astype(BF16)) + _dot(att.astype(BF16), v16)
        k_dec = (k * jnp.exp(g_last - g)).astype(BF16)
        st_ref[...] = st * jnp.exp(g_last) + _dot(v.T.astype(BF16), k_dec)

        y = _rms_norm(o, out_gain) * _silu(ga_ref[0, sl, :])
        o_ref[0, sl, :] = y.astype(o_ref.dtype)
        return carry

    lax.fori_loop(0, HG_STEP_TOKENS // c, chunk_body, 0)


def _hgrn(proj, lb_logits, out_gain, batch, seq, layer_row):
    ts, c, hd = HG_STEP_TOKENS, HG_CHUNK, HEAD_DIM
    assert seq % ts == 0 and ts % c == 0
    n_rows = lb_logits.shape[0]
    tri = (lax.broadcasted_iota(jnp.int32, (c, c), 0) >= lax.broadcasted_iota(jnp.int32, (c, c), 1)).astype(BF16)

    def head_block(part):
        return pl.BlockSpec((1, ts, hd), lambda b, h, s: (b, s, part * HG_HEADS + h))

    pipelined = 4 * _nbytes((ts, hd), F32) + _nbytes((ts, hd), BF16) + _nbytes((c, c), BF16)
    resident = 64 * _nbytes((c, hd), F32)
    return pl.pallas_call(
        functools.partial(_hgrn_kernel, layer_row=layer_row),
        out_shape=jax.ShapeDtypeStruct((batch, seq, HG_HEADS * hd), BF16),
        grid=(batch, HG_HEADS, seq // ts),
        in_specs=[
            pl.BlockSpec((n_rows, hd), lambda b, h, s: (0, h)),
            pl.BlockSpec((1, hd), lambda b, h, s: (0, 0)),
            pl.BlockSpec((c, c), lambda b, h, s: (0, 0)),
            head_block(0), head_block(1), head_block(2), head_block(3),
        ],
        out_specs=pl.BlockSpec((1, ts, hd), lambda b, h, s: (b, s, h)),
        scratch_shapes=[pltpu.VMEM((hd, hd), F32)],
        compiler_params=pltpu.CompilerParams(
            dimension_semantics=("arbitrary", "arbitrary", "arbitrary"),
            vmem_limit_bytes=_vmem_limit(pipelined, resident)),
        name="hgrn2",
    )(lb_logits, out_gain, tri, proj, proj, proj, proj)


def _sb_kernel(sfx_ref, q_ref, k_ref, vt_ref, o_ref, acc_ref, carry_ref):
    blk = SB_BLOCK
    i = pl.program_id(2)
    q = q_ref[0]
    causal = (lax.broadcasted_iota(jnp.int32, (blk, blk), 0)
              < lax.broadcasted_iota(jnp.int32, (blk, blk), 1))

    def sweep(j, on_diagonal):
        ks = k_ref[0, pl.ds(pl.multiple_of(j * blk, blk), blk), :]
        z = _dot_nt(ks, q)
        l1p = jnp.log1p(jnp.exp(-jnp.abs(z)))
        lk = -jnp.maximum(z, 0.0) - l1p
        lbeta = jnp.minimum(z, 0.0) - l1p
        if on_diagonal:
            lk = jnp.where(causal, lk, 0.0)
        hi = lk.astype(BF16)
        lo = (lk - hi.astype(F32)).astype(BF16)
        sfx = sfx_ref[...]
        later = _dot(sfx, hi) + _dot(sfx, lo) + carry_ref[...]
        w = jnp.exp(lbeta + later)
        if on_diagonal:
            w = jnp.where(causal, w, 0.0)
        acc_ref[...] += _dot(vt_ref[0, j], w.astype(BF16))
        carry_ref[...] += jnp.sum(lk, axis=0, keepdims=True)

    acc_ref[...] = jnp.zeros_like(acc_ref)
    carry_ref[...] = jnp.zeros_like(carry_ref)
    sweep(i, True)

    def body(it, c):
        sweep(i - 1 - it, False)
        return c

    lax.fori_loop(0, i, body, 0)
    o_ref[0] = acc_ref[...].T.astype(o_ref.dtype)


def _stick_breaking(qk, vt, batch, seq):
    blk, hd = SB_BLOCK, HEAD_DIM
    assert seq % blk == 0
    n_blk = seq // blk
    sfx = (lax.broadcasted_iota(jnp.int32, (blk, blk), 1) > lax.broadcasted_iota(jnp.int32, (blk, blk), 0)).astype(BF16)
    pipelined = (_nbytes((blk, blk), BF16) + 2 * _nbytes((blk, hd), BF16) + 2 * _nbytes((seq, hd), BF16))
    resident = _nbytes((hd, blk), F32) + 16 * _nbytes((blk, blk), F32)
    return pl.pallas_call(
        _sb_kernel,
        out_shape=jax.ShapeDtypeStruct((batch, seq, SB_HEADS * hd), BF16),
        grid=(batch, SB_HEADS, n_blk),
        in_specs=[
            pl.BlockSpec((blk, blk), lambda b, h, i: (0, 0)),
            pl.BlockSpec((1, blk, hd), lambda b, h, i: (b, i, h)),
            pl.BlockSpec((1, seq, hd), lambda b, h, i: (b, 0, SB_HEADS + h)),
            pl.BlockSpec((1, n_blk, hd, blk), lambda b, h, i: (b, 0, h, 0)),
        ],
        out_specs=pl.BlockSpec((1, blk, hd), lambda b, h, i: (b, i, h)),
        scratch_shapes=[pltpu.VMEM((hd, blk), F32), pltpu.VMEM((1, blk), F32)],
        compiler_params=pltpu.CompilerParams(
            dimension_semantics=("arbitrary", "arbitrary", "arbitrary"),
            vmem_limit_bytes=_vmem_limit(pipelined, resident)),
        name="stick_breaking",
    )(sfx, qk, qk, vt)


def _out_proj_kernel(oa_ref, ob_ref, wa_ref, wb_ref, x_ref, o_ref):
    o_ref[...] = x_ref[...] + _dot(oa_ref[...], wa_ref[...]) + _dot(ob_ref[...], wb_ref[...])


def _out_proj(o_a, o_b, w_a, w_b, x2):
    n_tok, d = x2.shape
    ka, kb = o_a.shape[1], o_b.shape[1]
    tm = OUT_PROJ_ROWS
    assert n_tok % tm == 0
    pipelined = (_nbytes((tm, ka), BF16) + _nbytes((tm, kb), BF16) + _nbytes((ka, d), BF16)
                 + _nbytes((kb, d), BF16) + 2 * _nbytes((tm, d), F32))
    resident = 2 * _nbytes((tm, d), F32)
    return pl.pallas_call(
        _out_proj_kernel,
        out_shape=jax.ShapeDtypeStruct((n_tok, d), F32),
        grid=(n_tok // tm,),
        in_specs=[
            pl.BlockSpec((tm, ka), lambda i: (i, 0)),
            pl.BlockSpec((tm, kb), lambda i: (i, 0)),
            pl.BlockSpec((ka, d), lambda i: (0, 0)),
            pl.BlockSpec((kb, d), lambda i: (0, 0)),
            pl.BlockSpec((tm, d), lambda i: (i, 0)),
        ],
        out_specs=pl.BlockSpec((tm, d), lambda i: (i, 0)),
        compiler_params=pltpu.CompilerParams(
            dimension_semantics=("arbitrary",),
            vmem_limit_bytes=_vmem_limit(pipelined, resident)),
        name="out_proj",
    )(o_a, o_b, w_a, w_b, x2)


def _ffn_kernel(x_ref, g_ref, wg_ref, wu_ref, wd_ref, fg_ref, o_ref, h_ref, *, final_norm):
    j = pl.program_id(1)

    @pl.when(j == 0)
    def _():
        x = x_ref[...]
        h_ref[...] = _rms_norm(x, g_ref[...]).astype(BF16)
        o_ref[...] = x

    h = h_ref[...]
    act = _silu(_dot(h, wg_ref[...])) * _dot(h, wu_ref[...])
    o_ref[...] += _dot(act.astype(BF16), wd_ref[...])

    if final_norm:
        @pl.when(j == pl.num_programs(1) - 1)
        def _():
            o_ref[...] = _rms_norm(o_ref[...], fg_ref[...])


def _ffn(x2, gain, w_gate, w_up, w_down, final_gain, final_norm, name):
    n_tok, d = x2.shape
    hidden = w_gate.shape[1]
    tm, th = FFN_ROWS, FFN_HIDDEN_TILE
    assert n_tok % tm == 0 and hidden % th == 0
    pipelined = (2 * _nbytes((tm, d), F32) + 2 * _nbytes((d, th), BF16) + _nbytes((th, d), BF16)
                 + 2 * _nbytes((1, d), F32))
    resident = _nbytes((tm, d), BF16) + 4 * _nbytes((tm, th), F32) + _nbytes((tm, d), F32)
    return pl.pallas_call(
        functools.partial(_ffn_kernel, final_norm=final_norm),
        out_shape=jax.ShapeDtypeStruct((n_tok, d), F32),
        grid=(n_tok // tm, hidden // th),
        in_specs=[
            pl.BlockSpec((tm, d), lambda i, j: (i, 0)),
            pl.BlockSpec((1, d), lambda i, j: (0, 0)),
            pl.BlockSpec((d, th), lambda i, j: (0, j)),
            pl.BlockSpec((d, th), lambda i, j: (0, j)),
            pl.BlockSpec((th, d), lambda i, j: (j, 0)),
            pl.BlockSpec((1, d), lambda i, j: (0, 0)),
        ],
        out_specs=pl.BlockSpec((tm, d), lambda i, j: (i, 0)),
        scratch_shapes=[pltpu.VMEM((tm, d), BF16)],
        compiler_params=pltpu.CompilerParams(
            dimension_semantics=("arbitrary", "arbitrary"),
            vmem_limit_bytes=_vmem_limit(pipelined, resident)),
        name=name,
    )(x2, gain, w_gate, w_up, w_down, final_gain)


def _pool_kernel(x_ref, halo_ref, g_ref, pw_ref, sc_ref, o_ref, hs_ref):
    ts = x_ref.shape[1]
    halo = POOL_HALO
    i = pl.program_id(1)
    gain = g_ref[...]
    x = x_ref[0]
    hs_ref[0:halo, :] = jnp.where(i == 0, 0.0, _rms_norm(halo_ref[0], gain))
    hs_ref[halo:halo + ts, :] = _rms_norm(x, gain)

    pos = i * ts + lax.broadcasted_iota(jnp.int32, (ts, 1), 0)
    group = pw_ref.shape[1]
    for gi, win in enumerate(POOL_WINDOWS):
        cols = slice(gi * group, (gi + 1) * group)
        h = hs_ref[halo:halo + ts, cols]
        wsum = h
        for back in range(1, win):
            wsum = wsum + hs_ref[halo - back:halo - back + ts, cols]
        count = jnp.minimum(pos + 1, win).astype(F32)
        pooled = wsum / count - h
        mixed = _dot(pooled.astype(BF16), pw_ref[gi])
        o_ref[0, :, cols] = x[:, cols] + mixed * sc_ref[:, cols]


def _pool_mixer(x3, gain, pool_w, scale):
    batch, seq, d = x3.shape
    ts, halo = POOL_ROWS, POOL_HALO
    n_groups, group = pool_w.shape[0], pool_w.shape[1]
    assert seq % ts == 0 and ts % halo == 0 and max(POOL_WINDOWS) - 1 <= halo
    assert n_groups == len(POOL_WINDOWS) and n_groups * group == d
    halo_per_tile = ts // halo
    pipelined = (2 * _nbytes((ts, d), F32) + _nbytes((halo, d), F32) + _nbytes(pool_w.shape, BF16)
                 + 2 * _nbytes((1, d), F32))
    resident = _nbytes((ts + halo, d), F32) + 3 * _nbytes((ts, d), F32)
    return pl.pallas_call(
        _pool_kernel,
        out_shape=jax.ShapeDtypeStruct((batch, seq, d), F32),
        grid=(batch, seq // ts),
        in_specs=[
            pl.BlockSpec((1, ts, d), lambda b, i: (b, i, 0)),
            pl.BlockSpec((1, halo, d), lambda b, i: (b, jnp.maximum(i * halo_per_tile - 1, 0), 0)),
            pl.BlockSpec((1, d), lambda b, i: (0, 0)),
            pl.BlockSpec((n_groups, group, group), lambda b, i: (0, 0, 0)),
            pl.BlockSpec((1, d), lambda b, i: (0, 0)),
        ],
        out_specs=pl.BlockSpec((1, ts, d), lambda b, i: (b, i, 0)),
        scratch_shapes=[pltpu.VMEM((ts + halo, d), F32)],
        compiler_params=pltpu.CompilerParams(
            dimension_semantics=("arbitrary", "arbitrary"),
            vmem_limit_bytes=_vmem_limit(pipelined, resident)),
        name="pool_mixer",
    )(x3, x3, gain, pool_w, scale)


def kernel(x, mix_norm, ffn_norm, final_norm, ab_w_in, lb_logits, hg_out_norm, ab_w_out,
           pool_w, pool_scale, ffn_w_gate, ffn_w_up, ffn_w_down):
    batch, seq, d = x.shape
    n_tok = batch * seq
    hg_width = HG_HEADS * HEAD_DIM
    sb_width = SB_HEADS * HEAD_DIM
    row = lambda a: a.reshape(1, -1)

    x2 = x.reshape(n_tok, d)
    w_in = ab_w_in[0]
    n_a = 4 * hg_width
    proj_a = _norm_proj(x2, row(mix_norm[0]), w_in[:, :n_a].astype(BF16),
                        jnp.ones((1, n_a), F32), F32, "in_proj_hgrn")
    qk_scale = jnp.concatenate([jnp.full((1, sb_width), 1.0 / math.sqrt(HEAD_DIM), F32),
                                jnp.ones((1, sb_width), F32)], axis=1)
    qk = _norm_proj(x2, row(mix_norm[0]), w_in[:, n_a:n_a + 2 * sb_width].astype(BF16),
                    qk_scale, BF16, "in_proj_qk")
    vt = _norm_proj_t(x2, row(mix_norm[0]), w_in[:, n_a + 2 * sb_width:].T.astype(BF16),
                      batch, seq, SB_BLOCK, "in_proj_vt")

    o_a = _hgrn(proj_a.reshape(batch, seq, n_a), lb_logits, row(hg_out_norm[0]), batch, seq, layer_row=0)
    o_b = _stick_breaking(qk.reshape(batch, seq, 2 * sb_width), vt, batch, seq)

    w_out = ab_w_out[0].astype(BF16)
    x2 = _out_proj(o_a.reshape(n_tok, hg_width), o_b.reshape(n_tok, sb_width),
                   w_out[:hg_width], w_out[hg_width:], x2)
    x2 = _ffn(x2, row(ffn_norm[0]), ffn_w_gate[0].astype(BF16), ffn_w_up[0].astype(BF16),
              ffn_w_down[0].astype(BF16), row(final_norm), False, "ffn0")

    x3 = _pool_mixer(x2.reshape(batch, seq, d), row(mix_norm[1]), pool_w[0].astype(BF16), row(pool_scale[0]))
    x2 = _ffn(x3.reshape(n_tok, d), row(ffn_norm[1]), ffn_w_gate[1].astype(BF16), ffn_w_up[1].astype(BF16),
              ffn_w_down[1].astype(BF16), row(final_norm), True, "ffn1_final_norm")
    return x2.reshape(batch, seq, d)
```

```python
import functools
import math

import jax
import jax.numpy as jnp
from jax import lax
from jax.experimental import pallas as pl
from jax.experimental.pallas import tpu as pltpu

F32 = jnp.float32
BF16 = jnp.bfloat16

RMS_EPS = 1e-6
HEAD_DIM = 128
HG_HEADS = 8
SB_HEADS = 8
POOL_WINDOWS = (2, 4, 8, 16)

V7X_VMEM_BYTES = 64 * 1024 * 1024
V7X_VMEM_USABLE_BYTES = 56 * 1024 * 1024
F32_SUBLANES = 8
SIGN_BIT = -2 ** 31

HG_CHUNK = 128
HG_STEP_TOKENS = 512
SB_BLOCK = 256
SB_QUERY_BLOCKS = 4
PROJ_ROWS = 1024
PROJ_COLS = 1024
OUT_PROJ_ROWS = 512
FFN_ROWS = 512
FFN_HIDDEN_TILE = 512
POOL_ROWS = 512
POOL_HALO = 16


def _vmem_limit(pipelined_bytes, resident_bytes):
    need = 2 * pipelined_bytes + resident_bytes
    assert need <= V7X_VMEM_USABLE_BYTES, (need, V7X_VMEM_USABLE_BYTES)
    return int(min(V7X_VMEM_USABLE_BYTES, max(need, 16 * 1024 * 1024)))


def _nbytes(shape, dtype):
    return math.prod(shape) * jnp.dtype(dtype).itemsize


def _rms_norm(x, gain):
    ms = jnp.mean(x * x, axis=-1, keepdims=True)
    return x * lax.rsqrt(ms + RMS_EPS) * gain


def _dot(a, b):
    return jnp.dot(a, b, preferred_element_type=F32)


def _dot_nt(a, b):
    return lax.dot_general(a, b, (((1,), (1,)), ((), ())), preferred_element_type=F32)


def _silu(x):
    return x * jax.nn.sigmoid(x)


def _norm_proj_kernel(x_ref, g_ref, w_ref, cs_ref, o_ref, h_ref):
    @pl.when(pl.program_id(1) == 0)
    def _():
        h_ref[...] = _rms_norm(x_ref[...], g_ref[...]).astype(BF16)

    o_ref[...] = (_dot(h_ref[...], w_ref[...]) * cs_ref[...]).astype(o_ref.dtype)


def _norm_proj(x2, gain, w, col_scale, out_dtype, name):
    n_tok, d = x2.shape
    n_out = w.shape[1]
    tm, tn = PROJ_ROWS, PROJ_COLS
    assert n_tok % tm == 0 and n_out % tn == 0
    pipelined = (_nbytes((tm, d), F32) + _nbytes((d, tn), BF16) + _nbytes((tm, tn), out_dtype)
                 + _nbytes((1, d), F32) + _nbytes((1, tn), F32))
    resident = _nbytes((tm, d), BF16) + 2 * _nbytes((tm, tn), F32)
    return pl.pallas_call(
        _norm_proj_kernel,
        out_shape=jax.ShapeDtypeStruct((n_tok, n_out), out_dtype),
        grid=(n_tok // tm, n_out // tn),
        in_specs=[
            pl.BlockSpec((tm, d), lambda i, j: (i, 0)),
            pl.BlockSpec((1, d), lambda i, j: (0, 0)),
            pl.BlockSpec((d, tn), lambda i, j: (0, j)),
            pl.BlockSpec((1, tn), lambda i, j: (0, j)),
        ],
        out_specs=pl.BlockSpec((tm, tn), lambda i, j: (i, j)),
        scratch_shapes=[pltpu.VMEM((tm, d), BF16)],
        compiler_params=pltpu.CompilerParams(
            dimension_semantics=("arbitrary", "arbitrary"),
            vmem_limit_bytes=_vmem_limit(pipelined, resident)),
        name=name,
    )(x2, gain, w, col_scale)


def _norm_proj_t_kernel(x_ref, g_ref, wt_ref, o_ref, *, key_block):
    h = _rms_norm(x_ref[...], g_ref[...]).astype(BF16)
    vt = _dot_nt(wt_ref[...], h)
    for kb in range(o_ref.shape[1]):
        o_ref[0, kb] = vt[:, kb * key_block:(kb + 1) * key_block].astype(o_ref.dtype)


def _norm_proj_t(x2, gain, wt, batch, seq, key_block, name):
    n_tok, d = x2.shape
    n_out = wt.shape[0]
    tm = PROJ_ROWS
    assert seq % tm == 0 and tm % key_block == 0
    tiles_per_seq = seq // tm
    kb_per_tile = tm // key_block
    pipelined = (_nbytes((tm, d), F32) + _nbytes((n_out, d), BF16) + _nbytes((n_out, tm), BF16)
                 + _nbytes((1, d), F32))
    resident = _nbytes((tm, d), BF16) + _nbytes((tm, d), F32) + 2 * _nbytes((n_out, tm), F32)
    return pl.pallas_call(
        functools.partial(_norm_proj_t_kernel, key_block=key_block),
        out_shape=jax.ShapeDtypeStruct((batch, seq // key_block, n_out, key_block), BF16),
        grid=(n_tok // tm,),
        in_specs=[
            pl.BlockSpec((tm, d), lambda i: (i, 0)),
            pl.BlockSpec((1, d), lambda i: (0, 0)),
            pl.BlockSpec((n_out, d), lambda i: (0, 0)),
        ],
        out_specs=pl.BlockSpec((1, kb_per_tile, n_out, key_block),
                               lambda i: (i // tiles_per_seq, i % tiles_per_seq, 0, 0)),
        compiler_params=pltpu.CompilerParams(
            dimension_semantics=("arbitrary",),
            vmem_limit_bytes=_vmem_limit(pipelined, resident)),
        name=name,
    )(x2, gain, wt)


def _split3_bf16(x):
    p1 = x.astype(BF16)
    r = x - p1.astype(F32)
    p2 = r.astype(BF16)
    r = r - p2.astype(F32)
    return p1, p2, r.astype(BF16)


def _sublane_broadcast(x, group, row):
    rows, lanes = x.shape
    x3 = x.reshape(rows // group, group, lanes)
    return jnp.broadcast_to(x3[:, row:row + 1, :], x3.shape).reshape(rows, lanes)


def _hgrn_kernel(lbl_ref, gn_ref, tri_ref, qa_ref, fa_ref, ia_ref, ga_ref, o_ref, st_ref, *, layer_row):
    c = HG_CHUNK
    n_levels = c.bit_length() - 1

    @pl.when(pl.program_id(2) == 0)
    def _():
        st_ref[...] = jnp.zeros_like(st_ref)

    logits = lbl_ref[...]
    e = jnp.exp(logits - jnp.max(logits, axis=0, keepdims=True))
    lower = jnp.sum(e[:layer_row + 1], axis=0, keepdims=True) / jnp.sum(e, axis=0, keepdims=True)

    row = lax.broadcasted_iota(jnp.int32, (c, HEAD_DIM), 0)
    t_idx = lax.broadcasted_iota(jnp.int32, (c, c), 0)
    s_idx = lax.broadcasted_iota(jnp.int32, (c, c), 1)
    t_xor_s = t_idx ^ s_idx
    tri = tri_ref[...]
    out_gain = gn_ref[...]

    def chunk_body(j, carry):
        sl = pl.ds(pl.multiple_of(j * c, c), c)
        q = _silu(qa_ref[0, sl, :])
        f = lower + (1.0 - lower) * jax.nn.sigmoid(fa_ref[0, sl, :])
        logf = jnp.log(f)
        k = 1.0 - f
        v = ia_ref[0, sl, :]
        v16 = v.astype(BF16)

        p1, p2, p3 = _split3_bf16(logf)
        g = _dot(tri, p1) + _dot(tri, p2) + _dot(tri, p3)

        att = jnp.zeros((c, c), F32)
        for lvl in range(n_levels):
            half = 1 << lvl
            upper = ((row >> lvl) & 1) == 1
            if lvl == 0:
                q_t = jnp.where(upper, q * f, 0.0)
                k_t = jnp.where(upper, 0.0, k)
            else:
                if 2 * half >= F32_SUBLANES:
                    g_ref_row = _sublane_broadcast(g, 2 * half, half - 1)
                else:
                    lo = _sublane_broadcast(g, F32_SUBLANES, half - 1)
                    hi = _sublane_broadcast(g, F32_SUBLANES, 2 * half + half - 1)
                    g_ref_row = jnp.where((row & (F32_SUBLANES - 1)) < 2 * half, lo, hi)
                d = g - g_ref_row
                ex = jnp.exp(jnp.where(upper, d, -d))
                q_t = jnp.where(upper, q * ex, 0.0)
                k_t = jnp.where(upper, 0.0, k * ex)
            a = _dot_nt(q_t.astype(BF16), k_t.astype(BF16))
            att = jnp.where((t_xor_s >> lvl) == 1, a, att)
        att = jnp.where(t_idx == s_idx, jnp.sum(q * k, axis=-1, keepdims=True), att)

        g_last = g[c - 1:c, :]
        st = st_ref[...]
        o = _dot_nt((q * jnp.exp(g)).astype(BF16), st.astype(BF16)) + _dot(att.astype(BF16), v16)
        k_dec = (k * jnp.exp(g_last - g)).astype(BF16)
        st_ref[...] = st * jnp.exp(g_last) + _dot(v.T.astype(BF16), k_dec)

        y = _rms_norm(o, out_gain) * _silu(ga_ref[0, sl, :])
        o_ref[0, sl, :] = y.astype(o_ref.dtype)
        return carry

    lax.fori_loop(0, HG_STEP_TOKENS // c, chunk_body, 0)


def _hgrn(proj, lb_logits, out_gain, batch, seq, layer_row):
    ts, c, hd = HG_STEP_TOKENS, HG_CHUNK, HEAD_DIM
    assert seq % ts == 0 and ts % c == 0
    n_rows = lb_logits.shape[0]
    tri = (lax.broadcasted_iota(jnp.int32, (c, c), 0) >= lax.broadcasted_iota(jnp.int32, (c, c), 1)).astype(BF16)

    def head_block(part):
        return pl.BlockSpec((1, ts, hd), lambda b, h, s: (b, s, part * HG_HEADS + h))

    pipelined = 4 * _nbytes((ts, hd), F32) + _nbytes((ts, hd), BF16) + _nbytes((c, c), BF16)
    resident = 64 * _nbytes((c, hd), F32)
    return pl.pallas_call(
        functools.partial(_hgrn_kernel, layer_row=layer_row),
        out_shape=jax.ShapeDtypeStruct((batch, seq, HG_HEADS * hd), BF16),
        grid=(batch, HG_HEADS, seq // ts),
        in_specs=[
            pl.BlockSpec((n_rows, hd), lambda b, h, s: (0, h)),
            pl.BlockSpec((1, hd), lambda b, h, s: (0, 0)),
            pl.BlockSpec((c, c), lambda b, h, s: (0, 0)),
            head_block(0), head_block(1), head_block(2), head_block(3),
        ],
        out_specs=pl.BlockSpec((1, ts, hd), lambda b, h, s: (b, s, h)),
        scratch_shapes=[pltpu.VMEM((hd, hd), F32)],
        compiler_params=pltpu.CompilerParams(
            dimension_semantics=("arbitrary", "arbitrary", "arbitrary"),
            vmem_limit_bytes=_vmem_limit(pipelined, resident)),
        name="hgrn2",
    )(lb_logits, out_gain, tri, proj, proj, proj, proj)


def _sb_kernel(sfx_ref, q_ref, k_ref, vt_ref, o_ref, acc_ref, carry_ref, sp_ref, zc_ref, w_ref):
    blk, nq = SB_BLOCK, SB_QUERY_BLOCKS
    i = pl.program_id(2)
    causal = (lax.broadcasted_iota(jnp.int32, (blk, blk), 0)
              < lax.broadcasted_iota(jnp.int32, (blk, blk), 1))
    sfx = sfx_ref[...]
    all_blocks = tuple((h, False) for h in range(nq))

    def scores(j, slot, query_blocks):
        ks = k_ref[0, pl.ds(pl.multiple_of(j * blk, blk), blk), :]
        for h, on_diagonal in query_blocks:
            z = _dot_nt(ks, q_ref[0, h * blk:(h + 1) * blk, :])
            neg_abs = lax.bitcast_convert_type(lax.bitcast_convert_type(z, jnp.int32) | SIGN_BIT, F32)
            sp = jnp.maximum(z, 0.0) + jnp.log2(1.0 + jnp.exp2(neg_abs))
            if on_diagonal:
                sp = jnp.where(causal, sp, 0.0)
            sp_ref[slot, h] = sp.astype(BF16)
            zc_ref[slot, h] = z - carry_ref[h]
            carry_ref[h] += jnp.sum(sp, axis=0, keepdims=True)

    def weights(slot, query_blocks):
        for h, on_diagonal in query_blocks:
            w = jnp.exp2(zc_ref[slot, h] - _dot(sfx, sp_ref[slot, h]))
            if on_diagonal:
                w = jnp.where(causal, w, 0.0)
            w_ref[slot, h] = w.astype(BF16)

    def values(j, slot, query_blocks):
        vt = vt_ref[0, j]
        for h, _ in query_blocks:
            acc_ref[h] += _dot(vt, w_ref[slot, h])

    acc_ref[...] = jnp.zeros_like(acc_ref)
    carry_ref[...] = jnp.zeros_like(carry_ref)

    diag = [(nq * i + off, pos % 2, tuple((h, h == off) for h in range(off, nq)))
            for pos, off in enumerate(reversed(range(nq)))]
    for j, slot, qb in diag:
        scores(j, slot, qb)
        weights(slot, qb)
        values(j, slot, qb)

    n_left = nq * i
    last = n_left - 1

    @pl.when(i > 0)
    def _():
        scores(last, 0, all_blocks)
        scores(last - 1, 1, all_blocks)
        weights(0, all_blocks)

        def body(u, c):
            n = 2 * u + 2
            scores(last - n, 0, all_blocks)
            weights(1, all_blocks)
            values(last - n + 2, 0, all_blocks)
            scores(last - n - 1, 1, all_blocks)
            weights(0, all_blocks)
            values(last - n + 1, 1, all_blocks)
            return c

        lax.fori_loop(0, (n_left - 2) // 2, body, 0)
        weights(1, all_blocks)
        values(1, 0, all_blocks)
        values(0, 1, all_blocks)

    for h in range(nq):
        o_ref[0, h * blk:(h + 1) * blk, :] = acc_ref[h].T.astype(o_ref.dtype)


def _stick_breaking(qk, vt, batch, seq):
    blk, nq, hd = SB_BLOCK, SB_QUERY_BLOCKS, HEAD_DIM
    tq = nq * blk
    assert seq % tq == 0 and nq % 2 == 0
    n_blk = seq // blk
    sfx = (lax.broadcasted_iota(jnp.int32, (blk, blk), 1) >= lax.broadcasted_iota(jnp.int32, (blk, blk), 0)).astype(BF16)
    pipelined = (_nbytes((blk, blk), BF16) + 2 * _nbytes((tq, hd), BF16) + 2 * _nbytes((seq, hd), BF16))
    slots = 2 * nq * (2 * _nbytes((blk, blk), BF16) + _nbytes((blk, blk), F32))
    resident = nq * _nbytes((hd, blk), F32) + slots + nq * 16 * _nbytes((blk, blk), F32)
    return pl.pallas_call(
        _sb_kernel,
        out_shape=jax.ShapeDtypeStruct((batch, seq, SB_HEADS * hd), BF16),
        grid=(batch, SB_HEADS, seq // tq),
        in_specs=[
            pl.BlockSpec((blk, blk), lambda b, h, i: (0, 0)),
            pl.BlockSpec((1, tq, hd), lambda b, h, i: (b, i, h)),
            pl.BlockSpec((1, seq, hd), lambda b, h, i: (b, 0, SB_HEADS + h)),
            pl.BlockSpec((1, n_blk, hd, blk), lambda b, h, i: (b, 0, h, 0)),
        ],
        out_specs=pl.BlockSpec((1, tq, hd), lambda b, h, i: (b, i, h)),
        scratch_shapes=[
            pltpu.VMEM((nq, hd, blk), F32),
            pltpu.VMEM((nq, 1, blk), F32),
            pltpu.VMEM((2, nq, blk, blk), BF16),
            pltpu.VMEM((2, nq, blk, blk), F32),
            pltpu.VMEM((2, nq, blk, blk), BF16),
        ],
        compiler_params=pltpu.CompilerParams(
            dimension_semantics=("arbitrary", "arbitrary", "arbitrary"),
            vmem_limit_bytes=_vmem_limit(pipelined, resident)),
        name="stick_breaking",
    )(sfx, qk, qk, vt)


def _out_proj_kernel(oa_ref, ob_ref, wa_ref, wb_ref, x_ref, o_ref):
    o_ref[...] = x_ref[...] + _dot(oa_ref[...], wa_ref[...]) + _dot(ob_ref[...], wb_ref[...])


def _out_proj(o_a, o_b, w_a, w_b, x2):
    n_tok, d = x2.shape
    ka, kb = o_a.shape[1], o_b.shape[1]
    tm = OUT_PROJ_ROWS
    assert n_tok % tm == 0
    pipelined = (_nbytes((tm, ka), BF16) + _nbytes((tm, kb), BF16) + _nbytes((ka, d), BF16)
                 + _nbytes((kb, d), BF16) + 2 * _nbytes((tm, d), F32))
    resident = 2 * _nbytes((tm, d), F32)
    return pl.pallas_call(
        _out_proj_kernel,
        out_shape=jax.ShapeDtypeStruct((n_tok, d), F32),
        grid=(n_tok // tm,),
        in_specs=[
            pl.BlockSpec((tm, ka), lambda i: (i, 0)),
            pl.BlockSpec((tm, kb), lambda i: (i, 0)),
            pl.BlockSpec((ka, d), lambda i: (0, 0)),
            pl.BlockSpec((kb, d), lambda i: (0, 0)),
            pl.BlockSpec((tm, d), lambda i: (i, 0)),
        ],
        out_specs=pl.BlockSpec((tm, d), lambda i: (i, 0)),
        compiler_params=pltpu.CompilerParams(
            dimension_semantics=("arbitrary",),
            vmem_limit_bytes=_vmem_limit(pipelined, resident)),
        name="out_proj",
    )(o_a, o_b, w_a, w_b, x2)


def _ffn_kernel(x_ref, g_ref, wg_ref, wu_ref, wd_ref, fg_ref, o_ref, h_ref, *, final_norm):
    j = pl.program_id(1)

    @pl.when(j == 0)
    def _():
        x = x_ref[...]
        h_ref[...] = _rms_norm(x, g_ref[...]).astype(BF16)
        o_ref[...] = x

    h = h_ref[...]
    act = _silu(_dot(h, wg_ref[...])) * _dot(h, wu_ref[...])
    o_ref[...] += _dot(act.astype(BF16), wd_ref[...])

    if final_norm:
        @pl.when(j == pl.num_programs(1) - 1)
        def _():
            o_ref[...] = _rms_norm(o_ref[...], fg_ref[...])


def _ffn(x2, gain, w_gate, w_up, w_down, final_gain, final_norm, name):
    n_tok, d = x2.shape
    hidden = w_gate.shape[1]
    tm, th = FFN_ROWS, FFN_HIDDEN_TILE
    assert n_tok % tm == 0 and hidden % th == 0
    pipelined = (2 * _nbytes((tm, d), F32) + 2 * _nbytes((d, th), BF16) + _nbytes((th, d), BF16)
                 + 2 * _nbytes((1, d), F32))
    resident = _nbytes((tm, d), BF16) + 4 * _nbytes((tm, th), F32) + _nbytes((tm, d), F32)
    return pl.pallas_call(
        functools.partial(_ffn_kernel, final_norm=final_norm),
        out_shape=jax.ShapeDtypeStruct((n_tok, d), F32),
        grid=(n_tok // tm, hidden // th),
        in_specs=[
            pl.BlockSpec((tm, d), lambda i, j: (i, 0)),
            pl.BlockSpec((1, d), lambda i, j: (0, 0)),
            pl.BlockSpec((d, th), lambda i, j: (0, j)),
            pl.BlockSpec((d, th), lambda i, j: (0, j)),
            pl.BlockSpec((th, d), lambda i, j: (j, 0)),
            pl.BlockSpec((1, d), lambda i, j: (0, 0)),
        ],
        out_specs=pl.BlockSpec((tm, d), lambda i, j: (i, 0)),
        scratch_shapes=[pltpu.VMEM((tm, d), BF16)],
        compiler_params=pltpu.CompilerParams(
            dimension_semantics=("arbitrary", "arbitrary"),
            vmem_limit_bytes=_vmem_limit(pipelined, resident)),
        name=name,
    )(x2, gain, w_gate, w_up, w_down, final_gain)


def _pool_kernel(x_ref, halo_ref, g_ref, pw_ref, sc_ref, o_ref, hs_ref):
    ts = x_ref.shape[1]
    halo = POOL_HALO
    i = pl.program_id(1)
    gain = g_ref[...]
    x = x_ref[0]
    hs_ref[0:halo, :] = jnp.where(i == 0, 0.0, _rms_norm(halo_ref[0], gain))
    hs_ref[halo:halo + ts, :] = _rms_norm(x, gain)

    pos = i * ts + lax.broadcasted_iota(jnp.int32, (ts, 1), 0)
    group = pw_ref.shape[1]
    for gi, win in enumerate(POOL_WINDOWS):
        cols = slice(gi * group, (gi + 1) * group)
        h = hs_ref[halo:halo + ts, cols]
        wsum = h
        for back in range(1, win):
            wsum = wsum + hs_ref[halo - back:halo - back + ts, cols]
        count = jnp.minimum(pos + 1, win).astype(F32)
        pooled = wsum / count - h
        mixed = _dot(pooled.astype(BF16), pw_ref[gi])
        o_ref[0, :, cols] = x[:, cols] + mixed * sc_ref[:, cols]


def _pool_mixer(x3, gain, pool_w, scale):
    batch, seq, d = x3.shape
    ts, halo = POOL_ROWS, POOL_HALO
    n_groups, group = pool_w.shape[0], pool_w.shape[1]
    assert seq % ts == 0 and ts % halo == 0 and max(POOL_WINDOWS) - 1 <= halo
    assert n_groups == len(POOL_WINDOWS) and n_groups * group == d
    halo_per_tile = ts // halo
    pipelined = (2 * _nbytes((ts, d), F32) + _nbytes((halo, d), F32) + _nbytes(pool_w.shape, BF16)
                 + 2 * _nbytes((1, d), F32))
    resident = _nbytes((ts + halo, d), F32) + 3 * _nbytes((ts, d), F32)
    return pl.pallas_call(
        _pool_kernel,
        out_shape=jax.ShapeDtypeStruct((batch, seq, d), F32),
        grid=(batch, seq // ts),
        in_specs=[
            pl.BlockSpec((1, ts, d), lambda b, i: (b, i, 0)),
            pl.BlockSpec((1, halo, d), lambda b, i: (b, jnp.maximum(i * halo_per_tile - 1, 0), 0)),
            pl.BlockSpec((1, d), lambda b, i: (0, 0)),
            pl.BlockSpec((n_groups, group, group), lambda b, i: (0, 0, 0)),
            pl.BlockSpec((1, d), lambda b, i: (0, 0)),
        ],
        out_specs=pl.BlockSpec((1, ts, d), lambda b, i: (b, i, 0)),
        scratch_shapes=[pltpu.VMEM((ts + halo, d), F32)],
        compiler_params=pltpu.CompilerParams(
            dimension_semantics=("arbitrary", "arbitrary"),
            vmem_limit_bytes=_vmem_limit(pipelined, resident)),
        name="pool_mixer",
    )(x3, x3, gain, pool_w, scale)


def kernel(x, mix_norm, ffn_norm, final_norm, ab_w_in, lb_logits, hg_out_norm, ab_w_out,
           pool_w, pool_scale, ffn_w_gate, ffn_w_up, ffn_w_down):
    batch, seq, d = x.shape
    n_tok = batch * seq
    hg_width = HG_HEADS * HEAD_DIM
    sb_width = SB_HEADS * HEAD_DIM
    row = lambda a: a.reshape(1, -1)

    x2 = x.reshape(n_tok, d)
    w_in = ab_w_in[0]
    n_a = 4 * hg_width
    proj_a = _norm_proj(x2, row(mix_norm[0]), w_in[:, :n_a].astype(BF16),
                        jnp.ones((1, n_a), F32), F32, "in_proj_hgrn")
    qk_scale = jnp.concatenate([jnp.full((1, sb_width), math.log2(math.e) / math.sqrt(HEAD_DIM), F32),
                                jnp.ones((1, sb_width), F32)], axis=1)
    qk = _norm_proj(x2, row(mix_norm[0]), w_in[:, n_a:n_a + 2 * sb_width].astype(BF16),
                    qk_scale, BF16, "in_proj_qk")
    vt = _norm_proj_t(x2, row(mix_norm[0]), w_in[:, n_a + 2 * sb_width:].T.astype(BF16),
                      batch, seq, SB_BLOCK, "in_proj_vt")

    o_a = _hgrn(proj_a.reshape(batch, seq, n_a), lb_logits, row(hg_out_norm[0]), batch, seq, layer_row=0)
    o_b = _stick_breaking(qk.reshape(batch, seq, 2 * sb_width), vt, batch, seq)

    w_out = ab_w_out[0].astype(BF16)
    x2 = _out_proj(o_a.reshape(n_tok, hg_width), o_b.reshape(n_tok, sb_width),
                   w_out[:hg_width], w_out[hg_width:], x2)
    x2 = _ffn(x2, row(ffn_norm[0]), ffn_w_gate[0].astype(BF16), ffn_w_up[0].astype(BF16),
              ffn_w_down[0].astype(BF16), row(final_norm), False, "ffn0")

    x3 = _pool_mixer(x2.reshape(batch, seq, d), row(mix_norm[1]), pool_w[0].astype(BF16), row(pool_scale[0]))
    x2 = _ffn(x3.reshape(n_tok, d), row(ffn_norm[1]), ffn_w_gate[1].astype(BF16), ffn_w_up[1].astype(BF16),
              ffn_w_down[1].astype(BF16), row(final_norm), True, "ffn1_final_norm")
    return x2.reshape(batch, seq, d)
```

```python
import functools
import math

import jax
import jax.numpy as jnp
from jax import lax
from jax.experimental import pallas as pl
from jax.experimental.pallas import tpu as pltpu

F32 = jnp.float32
BF16 = jnp.bfloat16

RMS_EPS = 1e-6
HEAD_DIM = 128
HG_HEADS = 8
SB_HEADS = 8
POOL_WINDOWS = (2, 4, 8, 16)

V7X_VMEM_BYTES = 64 * 1024 * 1024
V7X_VMEM_USABLE_BYTES = 56 * 1024 * 1024
F32_SUBLANES = 8
LOG2_E = math.log2(math.e)

HG_CHUNK = 128
HG_STEP_TOKENS = 1024
SB_BLOCK = 256
SB_QUERY_BLOCKS = 4
SB_UNDERFLOW_BITS = 160.0
PROJ_ROWS = 1024
PROJ_COLS = 1024
OUT_PROJ_ROWS = 512
FFN_ROWS = 1024
FFN_HIDDEN_TILE = 512
POOL_ROWS = 512
POOL_HALO = 16


def _vmem_limit(pipelined_bytes, resident_bytes):
    need = 2 * pipelined_bytes + resident_bytes
    assert need <= V7X_VMEM_USABLE_BYTES, (need, V7X_VMEM_USABLE_BYTES)
    return int(min(V7X_VMEM_USABLE_BYTES, max(need, 16 * 1024 * 1024)))


def _nbytes(shape, dtype):
    return math.prod(shape) * jnp.dtype(dtype).itemsize


def _rms_norm(x, gain):
    ms = jnp.mean(x * x, axis=-1, keepdims=True)
    return x * lax.rsqrt(ms + RMS_EPS) * gain


def _dot(a, b):
    return jnp.dot(a, b, preferred_element_type=F32)


def _dot_nt(a, b):
    return lax.dot_general(a, b, (((1,), (1,)), ((), ())), preferred_element_type=F32)


def _silu(x):
    return x * jax.nn.sigmoid(x)


def _in_proj_kernel(x_ref, g_ref, w_ref, oa_ref, oqk_ref, ovt_ref, h_ref, *, n_a_tiles, q_scale, key_block):
    j = pl.program_id(1)

    @pl.when(j == 0)
    def _():
        h_ref[...] = _rms_norm(x_ref[...], g_ref[...]).astype(BF16)

    acc = _dot(h_ref[...], w_ref[0])

    @pl.when(j < n_a_tiles)
    def _():
        oa_ref[...] = acc

    @pl.when(j == n_a_tiles)
    def _():
        oqk_ref[...] = (acc * q_scale).astype(oqk_ref.dtype)

    @pl.when(j == n_a_tiles + 1)
    def _():
        oqk_ref[...] = acc.astype(oqk_ref.dtype)

    @pl.when(j == n_a_tiles + 2)
    def _():
        vt = acc.T.astype(ovt_ref.dtype)
        for kb in range(ovt_ref.shape[1]):
            ovt_ref[0, kb] = vt[:, kb * key_block:(kb + 1) * key_block]


def _in_proj(x2, gain, w_in, layer, batch, seq, key_block):
    n_tok, d = x2.shape
    tm, tn = PROJ_ROWS, PROJ_COLS
    hg_width, sb_width = HG_HEADS * HEAD_DIM, SB_HEADS * HEAD_DIM
    assert tn == hg_width == sb_width and w_in.shape[2] == 7 * tn
    assert seq % tm == 0 and tm % key_block == 0
    n_a_tiles = 4
    tiles_per_seq = seq // tm
    kb_per_tile = tm // key_block
    pipelined = (_nbytes((tm, d), F32) + _nbytes((d, tn), BF16) + _nbytes((tm, tn), F32)
                 + 2 * _nbytes((tm, tn), BF16) + _nbytes((1, d), F32))
    resident = _nbytes((tm, d), BF16) + 2 * _nbytes((tm, tn), F32)
    return pl.pallas_call(
        functools.partial(_in_proj_kernel, n_a_tiles=n_a_tiles, q_scale=LOG2_E / math.sqrt(HEAD_DIM),
                          key_block=key_block),
        out_shape=(jax.ShapeDtypeStruct((n_tok, n_a_tiles * tn), F32),
                   jax.ShapeDtypeStruct((n_tok, 2 * tn), BF16),
                   jax.ShapeDtypeStruct((batch, seq // key_block, tn, key_block), BF16)),
        grid=(n_tok // tm, 7),
        in_specs=[
            pl.BlockSpec((tm, d), lambda i, j: (i, 0)),
            pl.BlockSpec((1, d), lambda i, j: (0, 0)),
            pl.BlockSpec((1, d, tn), lambda i, j: (layer, 0, j)),
        ],
        out_specs=(
            pl.BlockSpec((tm, tn), lambda i, j: (i, jnp.minimum(j, n_a_tiles - 1))),
            pl.BlockSpec((tm, tn), lambda i, j: (i, jnp.clip(j - n_a_tiles, 0, 1))),
            pl.BlockSpec((1, kb_per_tile, tn, key_block),
                         lambda i, j: (i // tiles_per_seq, i % tiles_per_seq, 0, 0)),
        ),
        scratch_shapes=[pltpu.VMEM((tm, d), BF16)],
        compiler_params=pltpu.CompilerParams(
            dimension_semantics=("arbitrary", "arbitrary"),
            vmem_limit_bytes=_vmem_limit(pipelined, resident)),
        name="in_proj",
    )(x2, gain, w_in)


def _split2_bf16(x):
    hi = x.astype(BF16)
    return hi, (x - hi.astype(F32)).astype(BF16)


def _sublane_broadcast(x, group, row):
    rows, lanes = x.shape
    x3 = x.reshape(rows // group, group, lanes)
    return jnp.broadcast_to(x3[:, row:row + 1, :], x3.shape).reshape(rows, lanes)


def _hgrn_kernel(lbl_ref, gn_ref, tri_ref, lvl_ref, qa_ref, fa_ref, ia_ref, ga_ref, o_ref, st_ref, *, layer_row):
    c, ts = HG_CHUNK, HG_STEP_TOKENS
    n_levels = c.bit_length() - 1
    chunks = [slice(j * c, (j + 1) * c) for j in range(ts // c)]

    @pl.when(pl.program_id(2) == 0)
    def _():
        st_ref[...] = jnp.zeros_like(st_ref)

    logits = lbl_ref[...]
    e = jnp.exp(logits - jnp.max(logits, axis=0, keepdims=True))
    lower = jnp.sum(e[:layer_row + 1], axis=0, keepdims=True) / jnp.sum(e, axis=0, keepdims=True)

    row = lax.broadcasted_iota(jnp.int32, (ts, HEAD_DIM), 0)
    q = _silu(qa_ref[0])
    f = lower + (1.0 - lower) * jax.nn.sigmoid(fa_ref[0])
    k = 1.0 - f
    v = ia_ref[0]
    v16 = v.astype(BF16)

    tri = tri_ref[...]
    parts = _split2_bf16(jnp.log(f) * LOG2_E)
    g = jnp.concatenate([sum(_dot(tri, p[ch]) for p in parts) for ch in chunks], axis=0)

    level_of = lvl_ref[...]
    att = [jnp.zeros((c, c), F32) for _ in chunks]
    for lvl in range(n_levels):
        half = 1 << lvl
        if half >= F32_SUBLANES:
            shape4 = (ts // (2 * half), 2, half, HEAD_DIM)
            g4, q4, k4 = g.reshape(shape4), q.reshape(shape4), k.reshape(shape4)
            g_mid = g4[:, 0:1, half - 1:half, :]
            x_lo = k4[:, 0:1] * jnp.exp2(g_mid - g4[:, 0:1])
            x_hi = q4[:, 1:2] * jnp.exp2(g4[:, 1:2] - g_mid)
            x = jnp.concatenate([x_lo, x_hi], axis=1).reshape(ts, HEAD_DIM)
        else:
            upper = (row & half) != 0
            if lvl == 0:
                x = jnp.where(upper, q * f, k)
            else:
                g_mid = _sublane_broadcast(g, F32_SUBLANES, half - 1)
                if 2 * half < F32_SUBLANES:
                    second = _sublane_broadcast(g, F32_SUBLANES, 2 * half + half - 1)
                    g_mid = jnp.where((row & (F32_SUBLANES - 1)) < 2 * half, g_mid, second)
                x = jnp.where(upper, q, k) * jnp.exp2(-jnp.abs(g - g_mid))
        x16 = x.astype(BF16)
        for j, ch in enumerate(chunks):
            att[j] = jnp.where(level_of == lvl, _dot_nt(x16[ch], x16[ch]), att[j])
    qk_diag = jnp.sum(q * k, axis=-1, keepdims=True)

    g_last = _sublane_broadcast(g, c, c - 1)
    q_dec = (q * jnp.exp2(g)).astype(BF16)
    k_dec = (k * jnp.exp2(g_last - g)).astype(BF16)
    st = st_ref[...]
    outs = []
    for j, ch in enumerate(chunks):
        a = jnp.where(level_of == n_levels, qk_diag[ch], att[j])
        outs.append(_dot_nt(q_dec[ch], st.astype(BF16)) + _dot(a.astype(BF16), v16[ch]))
        st = st * jnp.exp2(g_last[ch][:1]) + _dot(v[ch].T.astype(BF16), k_dec[ch])
    st_ref[...] = st

    y = _rms_norm(jnp.concatenate(outs, axis=0), gn_ref[...]) * _silu(ga_ref[0])
    o_ref[0] = y.astype(o_ref.dtype)


def _hgrn(proj, lb_logits, out_gain, batch, seq, layer_row):
    ts, c, hd = HG_STEP_TOKENS, HG_CHUNK, HEAD_DIM
    assert seq % ts == 0 and ts % c == 0
    n_rows = lb_logits.shape[0]
    t_idx = lax.broadcasted_iota(jnp.int32, (c, c), 0)
    s_idx = lax.broadcasted_iota(jnp.int32, (c, c), 1)
    tri = (t_idx >= s_idx).astype(BF16)
    level_of = jnp.where(t_idx > s_idx, 31 - lax.clz(t_idx ^ s_idx),
                         jnp.where(t_idx == s_idx, c.bit_length() - 1, -1)).astype(jnp.int32)

    def head_block(part):
        return pl.BlockSpec((1, ts, hd), lambda b, h, s: (b, s, part * HG_HEADS + h))

    pipelined = (4 * _nbytes((ts, hd), F32) + _nbytes((ts, hd), BF16) + _nbytes((c, c), BF16)
                 + _nbytes((c, c), jnp.int32))
    resident = 24 * _nbytes((ts, hd), F32)
    return pl.pallas_call(
        functools.partial(_hgrn_kernel, layer_row=layer_row),
        out_shape=jax.ShapeDtypeStruct((batch, seq, HG_HEADS * hd), BF16),
        grid=(batch, HG_HEADS, seq // ts),
        in_specs=[
            pl.BlockSpec((n_rows, hd), lambda b, h, s: (0, h)),
            pl.BlockSpec((1, hd), lambda b, h, s: (0, 0)),
            pl.BlockSpec((c, c), lambda b, h, s: (0, 0)),
            pl.BlockSpec((c, c), lambda b, h, s: (0, 0)),
            head_block(0), head_block(1), head_block(2), head_block(3),
        ],
        out_specs=pl.BlockSpec((1, ts, hd), lambda b, h, s: (b, s, h)),
        scratch_shapes=[pltpu.VMEM((hd, hd), F32)],
        compiler_params=pltpu.CompilerParams(
            dimension_semantics=("arbitrary", "arbitrary", "arbitrary"),
            vmem_limit_bytes=_vmem_limit(pipelined, resident)),
        name="hgrn2",
    )(lb_logits, out_gain, tri, level_of, proj, proj, proj, proj)


def _sb_kernel(sfx_ref, q_ref, k_ref, vt_ref, o_ref, acc_ref, carry_ref, sp_ref, lbc_ref, w_ref):
    blk, nq = SB_BLOCK, SB_QUERY_BLOCKS
    i = pl.program_id(2)
    causal = (lax.broadcasted_iota(jnp.int32, (blk, blk), 0)
              < lax.broadcasted_iota(jnp.int32, (blk, blk), 1))
    sfx = sfx_ref[...]
    all_blocks = tuple((h, False) for h in range(nq))

    def scores(j, slot, query_blocks):
        ks = k_ref[0, pl.ds(pl.multiple_of(j * blk, blk), blk), :]
        for h, on_diagonal in query_blocks:
            z = _dot_nt(ks, q_ref[0, h * blk:(h + 1) * blk, :])
            sp = jnp.maximum(z, 0.0) + jnp.log2(1.0 + jnp.exp2(-jnp.abs(z)))
            lbc_ref[slot, h] = (z - sp) - carry_ref[h]
            if on_diagonal:
                sp = jnp.where(causal, sp, 0.0)
            sp_ref[slot, h] = sp.astype(BF16)
            carry_ref[h] += jnp.sum(sp, axis=0, keepdims=True)

    def weights(slot, query_blocks):
        for h, on_diagonal in query_blocks:
            w = jnp.exp2(lbc_ref[slot, h] - _dot(sfx, sp_ref[slot, h]))
            if on_diagonal:
                w = jnp.where(causal, w, 0.0)
            w_ref[slot, h] = w.astype(BF16)

    def values(j, slot, query_blocks):
        vt = vt_ref[0, j]
        for h, _ in query_blocks:
            acc_ref[h] += _dot(vt, w_ref[slot, h])

    acc_ref[...] = jnp.zeros_like(acc_ref)
    carry_ref[...] = jnp.zeros_like(carry_ref)

    diag = [(nq * i + off, pos % 2, tuple((h, h == off) for h in range(off, nq)))
            for pos, off in enumerate(reversed(range(nq)))]
    for pos, (j, slot, qb) in enumerate(diag):
        scores(j, slot, qb)
        if pos >= 1:
            weights(diag[pos - 1][1], diag[pos - 1][2])
        if pos >= 2:
            values(*diag[pos - 2])
    (j_a, slot_a, qb_a), (j_b, slot_b, qb_b) = diag[nq - 2], diag[nq - 1]

    @pl.when(i == 0)
    def _():
        weights(slot_b, qb_b)
        values(j_a, slot_a, qb_a)
        values(j_b, slot_b, qb_b)

    n_left = nq * i
    last = n_left - 1

    @pl.when(i > 0)
    def _():
        scores(last, 0, all_blocks)
        weights(slot_b, qb_b)
        values(j_a, slot_a, qb_a)
        scores(last - 1, 1, all_blocks)
        weights(0, all_blocks)
        values(j_b, slot_b, qb_b)

        def keep_going(state):
            u, carry_min = state
            return jnp.logical_and(u < (n_left - 2) // 2, carry_min < SB_UNDERFLOW_BITS)

        def body(state):
            u, _ = state
            n = 2 * u + 2
            scores(last - n, 0, all_blocks)
            weights(1, all_blocks)
            values(last - n + 2, 0, all_blocks)
            scores(last - n - 1, 1, all_blocks)
            weights(0, all_blocks)
            values(last - n + 1, 1, all_blocks)
            return u + 1, jnp.min(carry_ref[...])

        pairs_done, _ = lax.while_loop(keep_going, body, (jnp.int32(0), jnp.min(carry_ref[...])))
        scored = 2 * pairs_done + 2
        weights(1, all_blocks)
        values(last - scored + 2, 0, all_blocks)
        values(last - scored + 1, 1, all_blocks)

    for h in range(nq):
        o_ref[0, h * blk:(h + 1) * blk, :] = acc_ref[h].T.astype(o_ref.dtype)


def _stick_breaking(qk, vt, batch, seq):
    blk, nq, hd = SB_BLOCK, SB_QUERY_BLOCKS, HEAD_DIM
    tq = nq * blk
    assert seq % tq == 0 and nq % 2 == 0
    n_blk = seq // blk
    sfx = (lax.broadcasted_iota(jnp.int32, (blk, blk), 1) > lax.broadcasted_iota(jnp.int32, (blk, blk), 0)).astype(BF16)
    pipelined = (_nbytes((blk, blk), BF16) + 2 * _nbytes((tq, hd), BF16) + 2 * _nbytes((seq, hd), BF16))
    slots = 2 * nq * (2 * _nbytes((blk, blk), BF16) + _nbytes((blk, blk), F32))
    resident = nq * _nbytes((hd, blk), F32) + slots + nq * 16 * _nbytes((blk, blk), F32)
    return pl.pallas_call(
        _sb_kernel,
        out_shape=jax.ShapeDtypeStruct((batch, seq, SB_HEADS * hd), BF16),
        grid=(batch, SB_HEADS, seq // tq),
        in_specs=[
            pl.BlockSpec((blk, blk), lambda b, h, i: (0, 0)),
            pl.BlockSpec((1, tq, hd), lambda b, h, i: (b, i, h)),
            pl.BlockSpec((1, seq, hd), lambda b, h, i: (b, 0, SB_HEADS + h)),
            pl.BlockSpec((1, n_blk, hd, blk), lambda b, h, i: (b, 0, h, 0)),
        ],
        out_specs=pl.BlockSpec((1, tq, hd), lambda b, h, i: (b, i, h)),
        scratch_shapes=[
            pltpu.VMEM((nq, hd, blk), F32),
            pltpu.VMEM((nq, 1, blk), F32),
            pltpu.VMEM((2, nq, blk, blk), BF16),
            pltpu.VMEM((2, nq, blk, blk), F32),
            pltpu.VMEM((2, nq, blk, blk), BF16),
        ],
        compiler_params=pltpu.CompilerParams(
            dimension_semantics=("arbitrary", "arbitrary", "arbitrary"),
            vmem_limit_bytes=_vmem_limit(pipelined, resident)),
        name="stick_breaking",
    )(sfx, qk, qk, vt)


def _out_proj_kernel(oa_ref, ob_ref, wa_ref, wb_ref, x_ref, o_ref):
    o_ref[...] = x_ref[...] + _dot(oa_ref[...], wa_ref[0]) + _dot(ob_ref[...], wb_ref[0])


def _out_proj(o_a, o_b, w_out, layer, x2):
    n_tok, d = x2.shape
    ka, kb = o_a.shape[1], o_b.shape[1]
    tm = OUT_PROJ_ROWS
    assert n_tok % tm == 0 and ka == kb and w_out.shape[1] == ka + kb
    pipelined = (_nbytes((tm, ka), BF16) + _nbytes((tm, kb), BF16) + _nbytes((ka, d), BF16)
                 + _nbytes((kb, d), BF16) + 2 * _nbytes((tm, d), F32))
    resident = 2 * _nbytes((tm, d), F32)
    return pl.pallas_call(
        _out_proj_kernel,
        out_shape=jax.ShapeDtypeStruct((n_tok, d), F32),
        grid=(n_tok // tm,),
        in_specs=[
            pl.BlockSpec((tm, ka), lambda i: (i, 0)),
            pl.BlockSpec((tm, kb), lambda i: (i, 0)),
            pl.BlockSpec((1, ka, d), lambda i: (layer, 0, 0)),
            pl.BlockSpec((1, kb, d), lambda i: (layer, 1, 0)),
            pl.BlockSpec((tm, d), lambda i: (i, 0)),
        ],
        out_specs=pl.BlockSpec((tm, d), lambda i: (i, 0)),
        compiler_params=pltpu.CompilerParams(
            dimension_semantics=("arbitrary",),
            vmem_limit_bytes=_vmem_limit(pipelined, resident)),
        name="out_proj",
    )(o_a, o_b, w_out, w_out, x2)


def _ffn_kernel(x_ref, g_ref, wg_ref, wu_ref, wd_ref, fg_ref, o_ref, h_ref, *, final_norm):
    j = pl.program_id(1)

    @pl.when(j == 0)
    def _():
        x = x_ref[...]
        h_ref[...] = _rms_norm(x, g_ref[...]).astype(BF16)
        o_ref[...] = x

    h = h_ref[...]
    act = _silu(_dot(h, wg_ref[0])) * _dot(h, wu_ref[0])
    o_ref[...] += _dot(act.astype(BF16), wd_ref[0])

    if final_norm:
        @pl.when(j == pl.num_programs(1) - 1)
        def _():
            o_ref[...] = _rms_norm(o_ref[...], fg_ref[...])


def _ffn(x2, gain, w_gate, w_up, w_down, layer, final_gain, final_norm, name):
    n_tok, d = x2.shape
    hidden = w_gate.shape[2]
    tm, th = FFN_ROWS, FFN_HIDDEN_TILE
    assert n_tok % tm == 0 and hidden % th == 0
    pipelined = (_nbytes((tm, d), F32) + 2 * _nbytes((d, th), BF16) + _nbytes((th, d), BF16)
                 + 2 * _nbytes((1, d), F32))
    resident = _nbytes((tm, d), F32) + _nbytes((tm, d), BF16) + 4 * _nbytes((tm, th), F32)
    return pl.pallas_call(
        functools.partial(_ffn_kernel, final_norm=final_norm),
        out_shape=jax.ShapeDtypeStruct((n_tok, d), F32),
        grid=(n_tok // tm, hidden // th),
        in_specs=[
            pl.BlockSpec((tm, d), lambda i, j: (i, 0), pipeline_mode=pl.Buffered(1)),
            pl.BlockSpec((1, d), lambda i, j: (0, 0)),
            pl.BlockSpec((1, d, th), lambda i, j: (layer, 0, j)),
            pl.BlockSpec((1, d, th), lambda i, j: (layer, 0, j)),
            pl.BlockSpec((1, th, d), lambda i, j: (layer, j, 0)),
            pl.BlockSpec((1, d), lambda i, j: (0, 0)),
        ],
        out_specs=pl.BlockSpec((tm, d), lambda i, j: (i, 0)),
        scratch_shapes=[pltpu.VMEM((tm, d), BF16)],
        compiler_params=pltpu.CompilerParams(
            dimension_semantics=("arbitrary", "arbitrary"),
            vmem_limit_bytes=_vmem_limit(pipelined, resident)),
        name=name,
    )(x2, gain, w_gate, w_up, w_down, final_gain)


def _pool_kernel(x_ref, halo_ref, g_ref, pw_ref, sc_ref, o_ref, hs_ref):
    ts = x_ref.shape[1]
    halo = POOL_HALO
    i = pl.program_id(1)
    gain = g_ref[...]
    x = x_ref[0]
    hs_ref[0:halo, :] = jnp.where(i == 0, 0.0, _rms_norm(halo_ref[0], gain))
    hs_ref[halo:halo + ts, :] = _rms_norm(x, gain)

    pos = i * ts + lax.broadcasted_iota(jnp.int32, (ts, 1), 0)
    group = pw_ref.shape[1]
    for gi, win in enumerate(POOL_WINDOWS):
        cols = slice(gi * group, (gi + 1) * group)
        h = hs_ref[halo:halo + ts, cols]
        wsum = h
        for back in range(1, win):
            wsum = wsum + hs_ref[halo - back:halo - back + ts, cols]
        count = jnp.minimum(pos + 1, win).astype(F32)
        pooled = wsum / count - h
        mixed = _dot(pooled.astype(BF16), pw_ref[gi])
        o_ref[0, :, cols] = x[:, cols] + mixed * sc_ref[:, cols]


def _pool_mixer(x3, gain, pool_w, scale):
    batch, seq, d = x3.shape
    ts, halo = POOL_ROWS, POOL_HALO
    n_groups, group = pool_w.shape[0], pool_w.shape[1]
    assert seq % ts == 0 and ts % halo == 0 and max(POOL_WINDOWS) - 1 <= halo
    assert n_groups == len(POOL_WINDOWS) and n_groups * group == d
    halo_per_tile = ts // halo
    pipelined = (2 * _nbytes((ts, d), F32) + _nbytes((halo, d), F32) + _nbytes(pool_w.shape, BF16)
                 + 2 * _nbytes((1, d), F32))
    resident = _nbytes((ts + halo, d), F32) + 3 * _nbytes((ts, d), F32)
    return pl.pallas_call(
        _pool_kernel,
        out_shape=jax.ShapeDtypeStruct((batch, seq, d), F32),
        grid=(batch, seq // ts),
        in_specs=[
            pl.BlockSpec((1, ts, d), lambda b, i: (b, i, 0)),
            pl.BlockSpec((1, halo, d), lambda b, i: (b, jnp.maximum(i * halo_per_tile - 1, 0), 0)),
            pl.BlockSpec((1, d), lambda b, i: (0, 0)),
            pl.BlockSpec((n_groups, group, group), lambda b, i: (0, 0, 0)),
            pl.BlockSpec((1, d), lambda b, i: (0, 0)),
        ],
        out_specs=pl.BlockSpec((1, ts, d), lambda b, i: (b, i, 0)),
        scratch_shapes=[pltpu.VMEM((ts + halo, d), F32)],
        compiler_params=pltpu.CompilerParams(
            dimension_semantics=("arbitrary", "arbitrary"),
            vmem_limit_bytes=_vmem_limit(pipelined, resident)),
        name="pool_mixer",
    )(x3, x3, gain, pool_w, scale)


def kernel(x, mix_norm, ffn_norm, final_norm, ab_w_in, lb_logits, hg_out_norm, ab_w_out,
           pool_w, pool_scale, ffn_w_gate, ffn_w_up, ffn_w_down):
    batch, seq, d = x.shape
    n_tok = batch * seq
    hg_width = HG_HEADS * HEAD_DIM
    sb_width = SB_HEADS * HEAD_DIM
    row = lambda a: a.reshape(1, -1)

    x2 = x.reshape(n_tok, d)
    proj_a, qk, vt = _in_proj(x2, row(mix_norm[0]), ab_w_in.astype(BF16), 0, batch, seq, SB_BLOCK)
    o_a = _hgrn(proj_a.reshape(batch, seq, 4 * hg_width), lb_logits, row(hg_out_norm[0]), batch, seq, layer_row=0)
    o_b = _stick_breaking(qk.reshape(batch, seq, 2 * sb_width), vt, batch, seq)
    x2 = _out_proj(o_a.reshape(n_tok, hg_width), o_b.reshape(n_tok, sb_width), ab_w_out.astype(BF16), 0, x2)

    w_gate, w_up, w_down = ffn_w_gate.astype(BF16), ffn_w_up.astype(BF16), ffn_w_down.astype(BF16)
    x2 = _ffn(x2, row(ffn_norm[0]), w_gate, w_up, w_down, 0, row(final_norm), False, "ffn0")

    x3 = _pool_mixer(x2.reshape(batch, seq, d), row(mix_norm[1]), pool_w[0].astype(BF16), row(pool_scale[0]))
    x2 = _ffn(x3.reshape(n_tok, d), row(ffn_norm[1]), w_gate, w_up, w_down, 1, row(final_norm), True,
              "ffn1_final_norm")
    return x2.reshape(batch, seq, d)
```

```python
import functools
import math

import jax
import jax.numpy as jnp
from jax import lax
from jax.experimental import pallas as pl
from jax.experimental.pallas import tpu as pltpu

F32 = jnp.float32
BF16 = jnp.bfloat16

RMS_EPS = 1e-6
HEAD_DIM = 128
HG_HEADS = 8
SB_HEADS = 8
POOL_WINDOWS = (2, 4, 8, 16)

V7X_VMEM_BYTES = 64 * 1024 * 1024
V7X_VMEM_USABLE_BYTES = 56 * 1024 * 1024
F32_SUBLANES = 8
LOG2_E = math.log2(math.e)

HG_CHUNK = 128
HG_STEP_TOKENS = 1024
SB_BLOCK = 256
SB_QUERY_BLOCKS = 8
SB_UNDERFLOW_BITS = 160.0
PROJ_ROWS = 1024
PROJ_COLS = 1024
OUT_PROJ_ROWS = 512
FFN_ROWS = 512
FFN_HIDDEN_TILE = 512
POOL_ROWS = 512
POOL_HALO = 16


def _vmem_limit(pipelined_bytes, resident_bytes):
    need = 2 * pipelined_bytes + resident_bytes
    assert need <= V7X_VMEM_USABLE_BYTES, (need, V7X_VMEM_USABLE_BYTES)
    return int(min(V7X_VMEM_USABLE_BYTES, max(need, 16 * 1024 * 1024)))


def _nbytes(shape, dtype):
    return math.prod(shape) * jnp.dtype(dtype).itemsize


def _rms_norm(x, gain):
    ms = jnp.mean(x * x, axis=-1, keepdims=True)
    return x * lax.rsqrt(ms + RMS_EPS) * gain


def _dot(a, b):
    return jnp.dot(a, b, preferred_element_type=F32)


def _dot_nt(a, b):
    return lax.dot_general(a, b, (((1,), (1,)), ((), ())), preferred_element_type=F32)


def _silu(x):
    return x * jax.nn.sigmoid(x)


def _in_proj_kernel(x_ref, g_ref, w_ref, oa_ref, oqk_ref, ovt_ref, h_ref, *, n_a_tiles, q_scale, key_block):
    j = pl.program_id(1)

    @pl.when(j == 0)
    def _():
        h_ref[...] = _rms_norm(x_ref[...], g_ref[...]).astype(BF16)

    acc = _dot(h_ref[...], w_ref[0])

    @pl.when(j < n_a_tiles)
    def _():
        oa_ref[...] = acc

    @pl.when(j == n_a_tiles)
    def _():
        oqk_ref[...] = (acc * q_scale).astype(oqk_ref.dtype)

    @pl.when(j == n_a_tiles + 1)
    def _():
        oqk_ref[...] = acc.astype(oqk_ref.dtype)

    @pl.when(j == n_a_tiles + 2)
    def _():
        vt = acc.T.astype(ovt_ref.dtype)
        for kb in range(ovt_ref.shape[1]):
            ovt_ref[0, kb] = vt[:, kb * key_block:(kb + 1) * key_block]


def _in_proj(x2, gain, w_in, layer, batch, seq, key_block):
    n_tok, d = x2.shape
    tm, tn = PROJ_ROWS, PROJ_COLS
    hg_width, sb_width = HG_HEADS * HEAD_DIM, SB_HEADS * HEAD_DIM
    assert tn == hg_width == sb_width and w_in.shape[2] == 7 * tn
    assert seq % tm == 0 and tm % key_block == 0
    n_a_tiles = 4
    tiles_per_seq = seq // tm
    kb_per_tile = tm // key_block
    pipelined = (_nbytes((tm, d), F32) + _nbytes((d, tn), BF16) + _nbytes((tm, tn), F32)
                 + 2 * _nbytes((tm, tn), BF16) + _nbytes((1, d), F32))
    resident = _nbytes((tm, d), BF16) + 2 * _nbytes((tm, tn), F32)
    return pl.pallas_call(
        functools.partial(_in_proj_kernel, n_a_tiles=n_a_tiles, q_scale=LOG2_E / math.sqrt(HEAD_DIM),
                          key_block=key_block),
        out_shape=(jax.ShapeDtypeStruct((n_tok, n_a_tiles * tn), F32),
                   jax.ShapeDtypeStruct((n_tok, 2 * tn), BF16),
                   jax.ShapeDtypeStruct((batch, seq // key_block, tn, key_block), BF16)),
        grid=(n_tok // tm, 7),
        in_specs=[
            pl.BlockSpec((tm, d), lambda i, j: (i, 0)),
            pl.BlockSpec((1, d), lambda i, j: (0, 0)),
            pl.BlockSpec((1, d, tn), lambda i, j: (layer, 0, j)),
        ],
        out_specs=(
            pl.BlockSpec((tm, tn), lambda i, j: (i, jnp.minimum(j, n_a_tiles - 1))),
            pl.BlockSpec((tm, tn), lambda i, j: (i, jnp.clip(j - n_a_tiles, 0, 1))),
            pl.BlockSpec((1, kb_per_tile, tn, key_block),
                         lambda i, j: (i // tiles_per_seq, i % tiles_per_seq, 0, 0)),
        ),
        scratch_shapes=[pltpu.VMEM((tm, d), BF16)],
        compiler_params=pltpu.CompilerParams(
            dimension_semantics=("arbitrary", "arbitrary"),
            vmem_limit_bytes=_vmem_limit(pipelined, resident)),
        name="in_proj",
    )(x2, gain, w_in)


def _split2_bf16(x):
    hi = x.astype(BF16)
    return hi, (x - hi.astype(F32)).astype(BF16)


def _sublane_broadcast(x, group, row):
    rows, lanes = x.shape
    x3 = x.reshape(rows // group, group, lanes)
    return jnp.broadcast_to(x3[:, row:row + 1, :], x3.shape).reshape(rows, lanes)


def _hgrn_kernel(lbl_ref, gn_ref, tri_ref, lvl_ref, qa_ref, fa_ref, ia_ref, ga_ref, o_ref, st_ref, *, layer_row):
    c, ts = HG_CHUNK, HG_STEP_TOKENS
    n_levels = c.bit_length() - 1
    chunks = [slice(j * c, (j + 1) * c) for j in range(ts // c)]

    @pl.when(pl.program_id(2) == 0)
    def _():
        st_ref[...] = jnp.zeros_like(st_ref)

    logits = lbl_ref[...]
    e = jnp.exp(logits - jnp.max(logits, axis=0, keepdims=True))
    lower = jnp.sum(e[:layer_row + 1], axis=0, keepdims=True) / jnp.sum(e, axis=0, keepdims=True)

    row = lax.broadcasted_iota(jnp.int32, (ts, HEAD_DIM), 0)
    q = _silu(qa_ref[0])
    f = lower + (1.0 - lower) * jax.nn.sigmoid(fa_ref[0])
    k = 1.0 - f
    v = ia_ref[0]
    v16 = v.astype(BF16)

    tri = tri_ref[...]
    parts = _split2_bf16(jnp.log(f) * LOG2_E)
    g = jnp.concatenate([sum(_dot(tri, p[ch]) for p in parts) for ch in chunks], axis=0)

    level_of = lvl_ref[...]
    att = [jnp.zeros((c, c), F32) for _ in chunks]
    for lvl in range(n_levels):
        half = 1 << lvl
        if half >= F32_SUBLANES:
            shape4 = (ts // (2 * half), 2, half, HEAD_DIM)
            g4, q4, k4 = g.reshape(shape4), q.reshape(shape4), k.reshape(shape4)
            g_mid = g4[:, 0:1, half - 1:half, :]
            x_lo = k4[:, 0:1] * jnp.exp2(g_mid - g4[:, 0:1])
            x_hi = q4[:, 1:2] * jnp.exp2(g4[:, 1:2] - g_mid)
            x = jnp.concatenate([x_lo, x_hi], axis=1).reshape(ts, HEAD_DIM)
        else:
            upper = (row & half) != 0
            if lvl == 0:
                x = jnp.where(upper, q * f, k)
            else:
                g_mid = _sublane_broadcast(g, F32_SUBLANES, half - 1)
                if 2 * half < F32_SUBLANES:
                    second = _sublane_broadcast(g, F32_SUBLANES, 2 * half + half - 1)
                    g_mid = jnp.where((row & (F32_SUBLANES - 1)) < 2 * half, g_mid, second)
                x = jnp.where(upper, q, k) * jnp.exp2(-jnp.abs(g - g_mid))
        x16 = x.astype(BF16)
        for j, ch in enumerate(chunks):
            att[j] = jnp.where(level_of == lvl, _dot_nt(x16[ch], x16[ch]), att[j])
    qk_diag = jnp.sum(q * k, axis=-1, keepdims=True)

    g_last = _sublane_broadcast(g, c, c - 1)
    q_dec = (q * jnp.exp2(g)).astype(BF16)
    k_dec = (k * jnp.exp2(g_last - g)).astype(BF16)
    st = st_ref[...]
    outs = []
    for j, ch in enumerate(chunks):
        a = jnp.where(level_of == n_levels, qk_diag[ch], att[j])
        outs.append(_dot_nt(q_dec[ch], st.astype(BF16)) + _dot(a.astype(BF16), v16[ch]))
        st = st * jnp.exp2(g_last[ch][:1]) + _dot(v[ch].T.astype(BF16), k_dec[ch])
    st_ref[...] = st

    y = _rms_norm(jnp.concatenate(outs, axis=0), gn_ref[...]) * _silu(ga_ref[0])
    o_ref[0] = y.astype(o_ref.dtype)


def _hgrn(proj, lb_logits, out_gain, batch, seq, layer_row):
    ts, c, hd = HG_STEP_TOKENS, HG_CHUNK, HEAD_DIM
    assert seq % ts == 0 and ts % c == 0
    n_rows = lb_logits.shape[0]
    t_idx = lax.broadcasted_iota(jnp.int32, (c, c), 0)
    s_idx = lax.broadcasted_iota(jnp.int32, (c, c), 1)
    tri = (t_idx >= s_idx).astype(BF16)
    level_of = jnp.where(t_idx > s_idx, 31 - lax.clz(t_idx ^ s_idx),
                         jnp.where(t_idx == s_idx, c.bit_length() - 1, -1)).astype(jnp.int32)

    def head_block(part):
        return pl.BlockSpec((1, ts, hd), lambda b, h, s: (b, s, part * HG_HEADS + h))

    pipelined = (4 * _nbytes((ts, hd), F32) + _nbytes((ts, hd), BF16) + _nbytes((c, c), BF16)
                 + _nbytes((c, c), jnp.int32))
    resident = 24 * _nbytes((ts, hd), F32)
    return pl.pallas_call(
        functools.partial(_hgrn_kernel, layer_row=layer_row),
        out_shape=jax.ShapeDtypeStruct((batch, seq, HG_HEADS * hd), BF16),
        grid=(batch, HG_HEADS, seq // ts),
        in_specs=[
            pl.BlockSpec((n_rows, hd), lambda b, h, s: (0, h)),
            pl.BlockSpec((1, hd), lambda b, h, s: (0, 0)),
            pl.BlockSpec((c, c), lambda b, h, s: (0, 0)),
            pl.BlockSpec((c, c), lambda b, h, s: (0, 0)),
            head_block(0), head_block(1), head_block(2), head_block(3),
        ],
        out_specs=pl.BlockSpec((1, ts, hd), lambda b, h, s: (b, s, h)),
        scratch_shapes=[pltpu.VMEM((hd, hd), F32)],
        compiler_params=pltpu.CompilerParams(
            dimension_semantics=("arbitrary", "arbitrary", "arbitrary"),
            vmem_limit_bytes=_vmem_limit(pipelined, resident)),
        name="hgrn2",
    )(lb_logits, out_gain, tri, level_of, proj, proj, proj, proj)


def _sb_kernel(sfx_ref, q_ref, k_ref, vt_ref, o_ref, acc_ref, carry_ref, sp_ref, lbc_ref, w_ref):
    blk, nq = SB_BLOCK, SB_QUERY_BLOCKS
    first = nq * pl.program_id(2)
    n_depths = first + nq
    causal = (lax.broadcasted_iota(jnp.int32, (blk, blk), 0)
              < lax.broadcasted_iota(jnp.int32, (blk, blk), 1))
    sfx = sfx_ref[...]

    def key_block(h, depth, always_live):
        kb = first + h - depth
        return (kb, None) if always_live else (jnp.maximum(kb, 0), kb >= 0)

    def scores(depth, slot, on_diagonal=False):
        for h in range(nq):
            j, live = key_block(h, depth, on_diagonal)
            ks = k_ref[0, pl.ds(pl.multiple_of(j * blk, blk), blk), :]
            z = _dot_nt(ks, q_ref[0, h * blk:(h + 1) * blk, :])
            sp = jnp.maximum(z, 0.0) + jnp.log2(1.0 + jnp.exp2(-jnp.abs(z)))
            lbc_ref[slot, h] = (z - sp) - carry_ref[h]
            if on_diagonal:
                sp = jnp.where(causal, sp, 0.0)
            sp_ref[slot, h] = sp.astype(BF16)
            total = jnp.sum(sp, axis=0, keepdims=True)
            carry_ref[h] += total if live is None else jnp.where(live, total, 0.0)

    def weights(slot, on_diagonal=False):
        for h in range(nq):
            w = jnp.exp2(lbc_ref[slot, h] - _dot(sfx, sp_ref[slot, h]))
            if on_diagonal:
                w = jnp.where(causal, w, 0.0)
            w_ref[slot, h] = w.astype(BF16)

    def values(depth, slot, on_diagonal=False):
        for h in range(nq):
            j, live = key_block(h, depth, on_diagonal)
            pv = _dot(vt_ref[0, j], w_ref[slot, h])
            acc_ref[h] += pv if live is None else jnp.where(live, pv, 0.0)

    def min_open_carry(next_depth):
        h_idx = lax.broadcasted_iota(jnp.int32, carry_ref.shape, 0)
        return jnp.min(jnp.where(first + h_idx - next_depth >= 0, carry_ref[...], SB_UNDERFLOW_BITS))

    acc_ref[...] = jnp.zeros_like(acc_ref)
    carry_ref[...] = jnp.zeros_like(carry_ref)

    scores(0, 0, on_diagonal=True)
    scores(1, 1)
    weights(0, on_diagonal=True)

    def keep_going(state):
        pairs, open_carry = state
        return jnp.logical_and(pairs < (n_depths - 2) // 2, open_carry < SB_UNDERFLOW_BITS)

    def body(state):
        pairs, _ = state
        d = 2 * pairs + 2
        scores(d, 0)
        weights(1)
        values(d - 2, 0)
        scores(d + 1, 1)
        weights(0)
        values(d - 1, 1)
        return pairs + 1, min_open_carry(d + 2)

    pairs_done, _ = lax.while_loop(keep_going, body, (jnp.int32(0), min_open_carry(2)))
    scored = 2 * pairs_done + 2
    weights(1)
    values(scored - 2, 0)
    values(scored - 1, 1)

    for h in range(nq):
        o_ref[0, h * blk:(h + 1) * blk, :] = acc_ref[h].T.astype(o_ref.dtype)


def _stick_breaking(qk, vt, batch, seq):
    blk, nq, hd = SB_BLOCK, SB_QUERY_BLOCKS, HEAD_DIM
    tq = nq * blk
    assert seq % tq == 0 and nq % 2 == 0
    n_blk = seq // blk
    sfx = (lax.broadcasted_iota(jnp.int32, (blk, blk), 1) > lax.broadcasted_iota(jnp.int32, (blk, blk), 0)).astype(BF16)
    pipelined = (_nbytes((blk, blk), BF16) + 2 * _nbytes((tq, hd), BF16) + 2 * _nbytes((seq, hd), BF16))
    slots = 2 * nq * (2 * _nbytes((blk, blk), BF16) + _nbytes((blk, blk), F32))
    resident = nq * _nbytes((hd, blk), F32) + slots + nq * 16 * _nbytes((blk, blk), F32)
    return pl.pallas_call(
        _sb_kernel,
        out_shape=jax.ShapeDtypeStruct((batch, seq, SB_HEADS * hd), BF16),
        grid=(batch, SB_HEADS, seq // tq),
        in_specs=[
            pl.BlockSpec((blk, blk), lambda b, h, i: (0, 0)),
            pl.BlockSpec((1, tq, hd), lambda b, h, i: (b, i, h)),
            pl.BlockSpec((1, seq, hd), lambda b, h, i: (b, 0, SB_HEADS + h)),
            pl.BlockSpec((1, n_blk, hd, blk), lambda b, h, i: (b, 0, h, 0)),
        ],
        out_specs=pl.BlockSpec((1, tq, hd), lambda b, h, i: (b, i, h)),
        scratch_shapes=[
            pltpu.VMEM((nq, hd, blk), F32),
            pltpu.VMEM((nq, 1, blk), F32),
            pltpu.VMEM((2, nq, blk, blk), BF16),
            pltpu.VMEM((2, nq, blk, blk), F32),
            pltpu.VMEM((2, nq, blk, blk), BF16),
        ],
        compiler_params=pltpu.CompilerParams(
            dimension_semantics=("arbitrary", "arbitrary", "arbitrary"),
            vmem_limit_bytes=_vmem_limit(pipelined, resident)),
        name="stick_breaking",
    )(sfx, qk, qk, vt)


def _out_proj_kernel(oa_ref, ob_ref, wa_ref, wb_ref, x_ref, o_ref):
    o_ref[...] = x_ref[...] + _dot(oa_ref[...], wa_ref[0]) + _dot(ob_ref[...], wb_ref[0])


def _out_proj(o_a, o_b, w_out, layer, x2):
    n_tok, d = x2.shape
    ka, kb = o_a.shape[1], o_b.shape[1]
    tm = OUT_PROJ_ROWS
    assert n_tok % tm == 0 and ka == kb and w_out.shape[1] == ka + kb
    pipelined = (_nbytes((tm, ka), BF16) + _nbytes((tm, kb), BF16) + _nbytes((ka, d), BF16)
                 + _nbytes((kb, d), BF16) + 2 * _nbytes((tm, d), F32))
    resident = 2 * _nbytes((tm, d), F32)
    return pl.pallas_call(
        _out_proj_kernel,
        out_shape=jax.ShapeDtypeStruct((n_tok, d), F32),
        grid=(n_tok // tm,),
        in_specs=[
            pl.BlockSpec((tm, ka), lambda i: (i, 0)),
            pl.BlockSpec((tm, kb), lambda i: (i, 0)),
            pl.BlockSpec((1, ka, d), lambda i: (layer, 0, 0)),
            pl.BlockSpec((1, kb, d), lambda i: (layer, 1, 0)),
            pl.BlockSpec((tm, d), lambda i: (i, 0)),
        ],
        out_specs=pl.BlockSpec((tm, d), lambda i: (i, 0)),
        compiler_params=pltpu.CompilerParams(
            dimension_semantics=("arbitrary",),
            vmem_limit_bytes=_vmem_limit(pipelined, resident)),
        name="out_proj",
    )(o_a, o_b, w_out, w_out, x2)


def _ffn_kernel(x_ref, g_ref, wg_ref, wu_ref, wd_ref, fg_ref, o_ref, h_ref, *, final_norm):
    j = pl.program_id(1)

    @pl.when(j == 0)
    def _():
        x = x_ref[...]
        h_ref[...] = _rms_norm(x, g_ref[...]).astype(BF16)
        o_ref[...] = x

    h = h_ref[...]
    act = _silu(_dot(h, wg_ref[0])) * _dot(h, wu_ref[0])
    o_ref[...] += _dot(act.astype(BF16), wd_ref[0])

    if final_norm:
        @pl.when(j == pl.num_programs(1) - 1)
        def _():
            o_ref[...] = _rms_norm(o_ref[...], fg_ref[...])


def _ffn(x2, gain, w_gate, w_up, w_down, layer, final_gain, final_norm, name):
    n_tok, d = x2.shape
    hidden = w_gate.shape[2]
    tm, th = FFN_ROWS, FFN_HIDDEN_TILE
    assert n_tok % tm == 0 and hidden % th == 0
    pipelined = (2 * _nbytes((tm, d), F32) + 2 * _nbytes((d, th), BF16) + _nbytes((th, d), BF16)
                 + 2 * _nbytes((1, d), F32))
    resident = _nbytes((tm, d), BF16) + 4 * _nbytes((tm, th), F32) + _nbytes((tm, d), F32)
    return pl.pallas_call(
        functools.partial(_ffn_kernel, final_norm=final_norm),
        out_shape=jax.ShapeDtypeStruct((n_tok, d), F32),
        grid=(n_tok // tm, hidden // th),
        in_specs=[
            pl.BlockSpec((tm, d), lambda i, j: (i, 0)),
            pl.BlockSpec((1, d), lambda i, j: (0, 0)),
            pl.BlockSpec((1, d, th), lambda i, j: (layer, 0, j)),
            pl.BlockSpec((1, d, th), lambda i, j: (layer, 0, j)),
            pl.BlockSpec((1, th, d), lambda i, j: (layer, j, 0)),
            pl.BlockSpec((1, d), lambda i, j: (0, 0)),
        ],
        out_specs=pl.BlockSpec((tm, d), lambda i, j: (i, 0)),
        scratch_shapes=[pltpu.VMEM((tm, d), BF16)],
        compiler_params=pltpu.CompilerParams(
            dimension_semantics=("arbitrary", "arbitrary"),
            vmem_limit_bytes=_vmem_limit(pipelined, resident)),
        name=name,
    )(x2, gain, w_gate, w_up, w_down, final_gain)


def _pool_kernel(x_ref, halo_ref, g_ref, pw_ref, sc_ref, o_ref, hs_ref):
    ts = x_ref.shape[1]
    halo = POOL_HALO
    i = pl.program_id(1)
    gain = g_ref[...]
    x = x_ref[0]
    hs_ref[0:halo, :] = jnp.where(i == 0, 0.0, _rms_norm(halo_ref[0], gain))
    hs_ref[halo:halo + ts, :] = _rms_norm(x, gain)

    pos = i * ts + lax.broadcasted_iota(jnp.int32, (ts, 1), 0)
    group = pw_ref.shape[1]
    for gi, win in enumerate(POOL_WINDOWS):
        cols = slice(gi * group, (gi + 1) * group)
        wsum = hs_ref[:, cols]
        for s in range(win.bit_length() - 1):
            wsum = wsum + pltpu.roll(wsum, 1 << s, axis=0)
        h = hs_ref[halo:halo + ts, cols]
        inv_count = 1.0 / jnp.minimum(pos + 1, win).astype(F32)
        pooled = wsum[halo:] * inv_count - h
        mixed = _dot(pooled.astype(BF16), pw_ref[gi])
        o_ref[0, :, cols] = x[:, cols] + mixed * sc_ref[:, cols]


def _pool_mixer(x3, gain, pool_w, scale):
    batch, seq, d = x3.shape
    ts, halo = POOL_ROWS, POOL_HALO
    n_groups, group = pool_w.shape[0], pool_w.shape[1]
    assert seq % ts == 0 and ts % halo == 0 and max(POOL_WINDOWS) - 1 <= halo
    assert all(w & (w - 1) == 0 for w in POOL_WINDOWS)
    assert n_groups == len(POOL_WINDOWS) and n_groups * group == d
    halo_per_tile = ts // halo
    pipelined = (2 * _nbytes((ts, d), F32) + _nbytes((halo, d), F32) + _nbytes(pool_w.shape, BF16)
                 + 2 * _nbytes((1, d), F32))
    resident = _nbytes((ts + halo, d), F32) + 3 * _nbytes((ts, d), F32)
    return pl.pallas_call(
        _pool_kernel,
        out_shape=jax.ShapeDtypeStruct((batch, seq, d), F32),
        grid=(batch, seq // ts),
        in_specs=[
            pl.BlockSpec((1, ts, d), lambda b, i: (b, i, 0)),
            pl.BlockSpec((1, halo, d), lambda b, i: (b, jnp.maximum(i * halo_per_tile - 1, 0), 0)),
            pl.BlockSpec((1, d), lambda b, i: (0, 0)),
            pl.BlockSpec((n_groups, group, group), lambda b, i: (0, 0, 0)),
            pl.BlockSpec((1, d), lambda b, i: (0, 0)),
        ],
        out_specs=pl.BlockSpec((1, ts, d), lambda b, i: (b, i, 0)),
        scratch_shapes=[pltpu.VMEM((ts + halo, d), F32)],
        compiler_params=pltpu.CompilerParams(
            dimension_semantics=("arbitrary", "arbitrary"),
            vmem_limit_bytes=_vmem_limit(pipelined, resident)),
        name="pool_mixer",
    )(x3, x3, gain, pool_w, scale)


def kernel(x, mix_norm, ffn_norm, final_norm, ab_w_in, lb_logits, hg_out_norm, ab_w_out,
           pool_w, pool_scale, ffn_w_gate, ffn_w_up, ffn_w_down):
    batch, seq, d = x.shape
    n_tok = batch * seq
    hg_width = HG_HEADS * HEAD_DIM
    sb_width = SB_HEADS * HEAD_DIM
    row = lambda a: a.reshape(1, -1)

    x2 = x.reshape(n_tok, d)
    proj_a, qk, vt = _in_proj(x2, row(mix_norm[0]), ab_w_in.astype(BF16), 0, batch, seq, SB_BLOCK)
    o_a = _hgrn(proj_a.reshape(batch, seq, 4 * hg_width), lb_logits, row(hg_out_norm[0]), batch, seq, layer_row=0)
    o_b = _stick_breaking(qk.reshape(batch, seq, 2 * sb_width), vt, batch, seq)
    x2 = _out_proj(o_a.reshape(n_tok, hg_width), o_b.reshape(n_tok, sb_width), ab_w_out.astype(BF16), 0, x2)

    w_gate, w_up, w_down = ffn_w_gate.astype(BF16), ffn_w_up.astype(BF16), ffn_w_down.astype(BF16)
    x2 = _ffn(x2, row(ffn_norm[0]), w_gate, w_up, w_down, 0, row(final_norm), False, "ffn0")

    x3 = _pool_mixer(x2.reshape(batch, seq, d), row(mix_norm[1]), pool_w[0].astype(BF16), row(pool_scale[0]))
    x2 = _ffn(x3.reshape(n_tok, d), row(ffn_norm[1]), w_gate, w_up, w_down, 1, row(final_norm), True,
              "ffn1_final_norm")
    return x2.reshape(batch, seq, d)
```

```python
import functools
import math

import jax
import jax.numpy as jnp
from jax import lax
from jax.experimental import pallas as pl
from jax.experimental.pallas import tpu as pltpu

F32 = jnp.float32
BF16 = jnp.bfloat16

RMS_EPS = 1e-6
HEAD_DIM = 128
HG_HEADS = 8
SB_HEADS = 8
POOL_WINDOWS = (2, 4, 8, 16)

V7X_VMEM_BYTES = 64 * 1024 * 1024
V7X_VMEM_USABLE_BYTES = 56 * 1024 * 1024
F32_SUBLANES = 8
LOG2_E = math.log2(math.e)

HG_CHUNK = 128
HG_STEP_TOKENS = 1024
SB_BLOCK = 256
SB_QUERY_BLOCKS = 8
SB_UNDERFLOW_BITS = 160.0
PROJ_ROWS = 1024
PROJ_COLS = 1024
OUT_PROJ_ROWS = 512
FFN_ROWS = 512
FFN_HIDDEN_TILE = 512
POOL_ROWS = 512
POOL_HALO = 16


def _vmem_limit(pipelined_bytes, resident_bytes):
    need = 2 * pipelined_bytes + resident_bytes
    assert need <= V7X_VMEM_USABLE_BYTES, (need, V7X_VMEM_USABLE_BYTES)
    return int(min(V7X_VMEM_USABLE_BYTES, max(need, 16 * 1024 * 1024)))


def _nbytes(shape, dtype):
    return math.prod(shape) * jnp.dtype(dtype).itemsize


def _rms_norm(x, gain):
    ms = jnp.mean(x * x, axis=-1, keepdims=True)
    return x * lax.rsqrt(ms + RMS_EPS) * gain


def _dot(a, b):
    return jnp.dot(a, b, preferred_element_type=F32)


def _dot_nt(a, b):
    return lax.dot_general(a, b, (((1,), (1,)), ((), ())), preferred_element_type=F32)


def _silu(x):
    return x * jax.nn.sigmoid(x)


def _in_proj_kernel(x_ref, g_ref, w_ref, oa_ref, oqk_ref, ovt_ref, h_ref, *, n_a_tiles, q_scale, key_block):
    j = pl.program_id(1)

    @pl.when(j == 0)
    def _():
        h_ref[...] = _rms_norm(x_ref[...], g_ref[...]).astype(BF16)

    def tile():
        return _dot(h_ref[...], w_ref[0])

    @pl.when(j < n_a_tiles)
    def _():
        oa_ref[...] = tile()

    @pl.when(j == n_a_tiles)
    def _():
        oqk_ref[...] = (tile() * q_scale).astype(oqk_ref.dtype)

    @pl.when(j == n_a_tiles + 1)
    def _():
        oqk_ref[...] = tile().astype(oqk_ref.dtype)

    @pl.when(j == n_a_tiles + 2)
    def _():
        vt = tile().T.astype(ovt_ref.dtype)
        for kb in range(ovt_ref.shape[1]):
            ovt_ref[0, kb] = vt[:, kb * key_block:(kb + 1) * key_block]


def _in_proj(x2, gain, w_in, layer, batch, seq, key_block):
    n_tok, d = x2.shape
    tm, tn = PROJ_ROWS, PROJ_COLS
    hg_width, sb_width = HG_HEADS * HEAD_DIM, SB_HEADS * HEAD_DIM
    assert tn == hg_width == sb_width and w_in.shape[2] == 7 * tn
    assert seq % tm == 0 and tm % key_block == 0
    n_a_tiles = 4
    tiles_per_seq = seq // tm
    kb_per_tile = tm // key_block
    pipelined = (_nbytes((tm, d), F32) + _nbytes((d, tn), BF16) + _nbytes((tm, tn), F32)
                 + 2 * _nbytes((tm, tn), BF16) + _nbytes((1, d), F32))
    resident = _nbytes((tm, d), BF16) + 2 * _nbytes((tm, tn), F32)
    return pl.pallas_call(
        functools.partial(_in_proj_kernel, n_a_tiles=n_a_tiles, q_scale=LOG2_E / math.sqrt(HEAD_DIM),
                          key_block=key_block),
        out_shape=(jax.ShapeDtypeStruct((n_tok, n_a_tiles * tn), F32),
                   jax.ShapeDtypeStruct((n_tok, 2 * tn), BF16),
                   jax.ShapeDtypeStruct((batch, seq // key_block, tn, key_block), BF16)),
        grid=(n_tok // tm, 7),
        in_specs=[
            pl.BlockSpec((tm, d), lambda i, j: (i, 0)),
            pl.BlockSpec((1, d), lambda i, j: (0, 0)),
            pl.BlockSpec((1, d, tn), lambda i, j: (layer, 0, j)),
        ],
        out_specs=(
            pl.BlockSpec((tm, tn), lambda i, j: (i, jnp.minimum(j, n_a_tiles - 1))),
            pl.BlockSpec((tm, tn), lambda i, j: (i, jnp.clip(j - n_a_tiles, 0, 1))),
            pl.BlockSpec((1, kb_per_tile, tn, key_block),
                         lambda i, j: (i // tiles_per_seq, i % tiles_per_seq, 0, 0)),
        ),
        scratch_shapes=[pltpu.VMEM((tm, d), BF16)],
        compiler_params=pltpu.CompilerParams(
            dimension_semantics=("arbitrary", "arbitrary"),
            vmem_limit_bytes=_vmem_limit(pipelined, resident)),
        name="in_proj",
    )(x2, gain, w_in)


def _split2_bf16(x):
    hi = x.astype(BF16)
    return hi, (x - hi.astype(F32)).astype(BF16)


def _sublane_broadcast(x, group, row):
    rows, lanes = x.shape
    x3 = x.reshape(rows // group, group, lanes)
    return jnp.broadcast_to(x3[:, row:row + 1, :], x3.shape).reshape(rows, lanes)


def _hgrn_kernel(lbl_ref, gn_ref, tri_ref, lvl_ref, qa_ref, fa_ref, ia_ref, ga_ref, o_ref, st_ref, *, layer_row):
    c, ts = HG_CHUNK, HG_STEP_TOKENS
    n_levels = c.bit_length() - 1
    chunks = [slice(j * c, (j + 1) * c) for j in range(ts // c)]

    @pl.when(pl.program_id(2) == 0)
    def _():
        st_ref[...] = jnp.zeros_like(st_ref)

    logits = lbl_ref[...]
    e = jnp.exp(logits - jnp.max(logits, axis=0, keepdims=True))
    lower = jnp.sum(e[:layer_row + 1], axis=0, keepdims=True) / jnp.sum(e, axis=0, keepdims=True)

    row = lax.broadcasted_iota(jnp.int32, (ts, HEAD_DIM), 0)
    q = _silu(qa_ref[0])
    f = lower + (1.0 - lower) * jax.nn.sigmoid(fa_ref[0])
    k = 1.0 - f
    v = ia_ref[0]
    v16 = v.astype(BF16)

    tri = tri_ref[...]
    parts = _split2_bf16(jnp.log(f) * LOG2_E)
    g = jnp.concatenate([sum(_dot(tri, p[ch]) for p in parts) for ch in chunks], axis=0)

    level_of = lvl_ref[...]
    att = [jnp.zeros((c, c), F32) for _ in chunks]
    for lvl in range(n_levels):
        half = 1 << lvl
        if half >= F32_SUBLANES:
            shape4 = (ts // (2 * half), 2, half, HEAD_DIM)
            g4, q4, k4 = g.reshape(shape4), q.reshape(shape4), k.reshape(shape4)
            g_mid = g4[:, 0:1, half - 1:half, :]
            x_lo = k4[:, 0:1] * jnp.exp2(g_mid - g4[:, 0:1])
            x_hi = q4[:, 1:2] * jnp.exp2(g4[:, 1:2] - g_mid)
            x = jnp.concatenate([x_lo, x_hi], axis=1).reshape(ts, HEAD_DIM)
        else:
            upper = (row & half) != 0
            if lvl == 0:
                x = jnp.where(upper, q * f, k)
            else:
                g_mid = _sublane_broadcast(g, F32_SUBLANES, half - 1)
                if 2 * half < F32_SUBLANES:
                    second = _sublane_broadcast(g, F32_SUBLANES, 2 * half + half - 1)
                    g_mid = jnp.where((row & (F32_SUBLANES - 1)) < 2 * half, g_mid, second)
                x = jnp.where(upper, q, k) * jnp.exp2(-jnp.abs(g - g_mid))
        x16 = x.astype(BF16)
        for j, ch in enumerate(chunks):
            att[j] = jnp.where(level_of == lvl, _dot_nt(x16[ch], x16[ch]), att[j])
    qk_diag = jnp.sum(q * k, axis=-1, keepdims=True)

    g_last = _sublane_broadcast(g, c, c - 1)
    q_dec = (q * jnp.exp2(g)).astype(BF16)
    k_dec = (k * jnp.exp2(g_last - g)).astype(BF16)
    st = st_ref[...]
    outs = []
    for j, ch in enumerate(chunks):
        a = jnp.where(level_of == n_levels, qk_diag[ch], att[j])
        outs.append(_dot_nt(q_dec[ch], st.astype(BF16)) + _dot(a.astype(BF16), v16[ch]))
        st = st * jnp.exp2(g_last[ch][:1]) + _dot(v[ch].T.astype(BF16), k_dec[ch])
    st_ref[...] = st

    y = _rms_norm(jnp.concatenate(outs, axis=0), gn_ref[...]) * _silu(ga_ref[0])
    o_ref[0] = y.astype(o_ref.dtype)


def _hgrn(proj, lb_logits, out_gain, batch, seq, layer_row):
    ts, c, hd = HG_STEP_TOKENS, HG_CHUNK, HEAD_DIM
    assert seq % ts == 0 and ts % c == 0
    n_rows = lb_logits.shape[0]
    t_idx = lax.broadcasted_iota(jnp.int32, (c, c), 0)
    s_idx = lax.broadcasted_iota(jnp.int32, (c, c), 1)
    tri = (t_idx >= s_idx).astype(BF16)
    level_of = jnp.where(t_idx > s_idx, 31 - lax.clz(t_idx ^ s_idx),
                         jnp.where(t_idx == s_idx, c.bit_length() - 1, -1)).astype(jnp.int32)

    def head_block(part):
        return pl.BlockSpec((1, ts, hd), lambda b, h, s: (b, s, part * HG_HEADS + h))

    pipelined = (4 * _nbytes((ts, hd), F32) + _nbytes((ts, hd), BF16) + _nbytes((c, c), BF16)
                 + _nbytes((c, c), jnp.int32))
    resident = 24 * _nbytes((ts, hd), F32)
    return pl.pallas_call(
        functools.partial(_hgrn_kernel, layer_row=layer_row),
        out_shape=jax.ShapeDtypeStruct((batch, seq, HG_HEADS * hd), BF16),
        grid=(batch, HG_HEADS, seq // ts),
        in_specs=[
            pl.BlockSpec((n_rows, hd), lambda b, h, s: (0, h)),
            pl.BlockSpec((1, hd), lambda b, h, s: (0, 0)),
            pl.BlockSpec((c, c), lambda b, h, s: (0, 0)),
            pl.BlockSpec((c, c), lambda b, h, s: (0, 0)),
            head_block(0), head_block(1), head_block(2), head_block(3),
        ],
        out_specs=pl.BlockSpec((1, ts, hd), lambda b, h, s: (b, s, h)),
        scratch_shapes=[pltpu.VMEM((hd, hd), F32)],
        compiler_params=pltpu.CompilerParams(
            dimension_semantics=("arbitrary", "arbitrary", "arbitrary"),
            vmem_limit_bytes=_vmem_limit(pipelined, resident)),
        name="hgrn2",
    )(lb_logits, out_gain, tri, level_of, proj, proj, proj, proj)


def _sb_kernel(sfx_ref, q_ref, k_ref, vt_ref, o_ref, acc_ref, carry_ref, sp_ref, lbc_ref, w_ref):
    blk, nq = SB_BLOCK, SB_QUERY_BLOCKS
    first = nq * pl.program_id(2)
    n_depths = first + nq
    causal = (lax.broadcasted_iota(jnp.int32, (blk, blk), 0)
              < lax.broadcasted_iota(jnp.int32, (blk, blk), 1))
    sfx = sfx_ref[...]

    def key_block(h, depth, always_live):
        kb = first + h - depth
        return (kb, None) if always_live else (jnp.maximum(kb, 0), kb >= 0)

    def scores(depth, slot, on_diagonal=False):
        for h in range(nq):
            j, live = key_block(h, depth, on_diagonal)
            ks = k_ref[0, pl.ds(pl.multiple_of(j * blk, blk), blk), :]
            z = _dot_nt(ks, q_ref[0, h * blk:(h + 1) * blk, :])
            sp = jnp.maximum(z, 0.0) + jnp.log2(1.0 + jnp.exp2(-jnp.abs(z)))
            lbc_ref[slot, h] = (z - sp) - carry_ref[h]
            if on_diagonal:
                sp = jnp.where(causal, sp, 0.0)
            sp_ref[slot, h] = sp.astype(BF16)
            total = jnp.sum(sp, axis=0, keepdims=True)
            carry_ref[h] += total if live is None else jnp.where(live, total, 0.0)

    def weights(slot, on_diagonal=False):
        for h in range(nq):
            w = jnp.exp2(lbc_ref[slot, h] - _dot(sfx, sp_ref[slot, h]))
            if on_diagonal:
                w = jnp.where(causal, w, 0.0)
            w_ref[slot, h] = w.astype(BF16)

    def values(depth, slot, on_diagonal=False):
        for h in range(nq):
            j, live = key_block(h, depth, on_diagonal)
            pv = _dot(vt_ref[0, j], w_ref[slot, h])
            acc_ref[h] += pv if live is None else jnp.where(live, pv, 0.0)

    def min_open_carry(next_depth):
        h_idx = lax.broadcasted_iota(jnp.int32, carry_ref.shape, 0)
        return jnp.min(jnp.where(first + h_idx - next_depth >= 0, carry_ref[...], SB_UNDERFLOW_BITS))

    acc_ref[...] = jnp.zeros_like(acc_ref)
    carry_ref[...] = jnp.zeros_like(carry_ref)

    scores(0, 0, on_diagonal=True)
    scores(1, 1)
    weights(0, on_diagonal=True)

    def keep_going(state):
        pairs, open_carry = state
        return jnp.logical_and(pairs < (n_depths - 2) // 2, open_carry < SB_UNDERFLOW_BITS)

    def body(state):
        pairs, _ = state
        d = 2 * pairs + 2
        scores(d, 0)
        weights(1)
        values(d - 2, 0)
        scores(d + 1, 1)
        weights(0)
        values(d - 1, 1)
        return pairs + 1, min_open_carry(d + 2)

    pairs_done, _ = lax.while_loop(keep_going, body, (jnp.int32(0), min_open_carry(2)))
    scored = 2 * pairs_done + 2
    weights(1)
    values(scored - 2, 0)
    values(scored - 1, 1)

    for h in range(nq):
        o_ref[0, h * blk:(h + 1) * blk, :] = acc_ref[h].T.astype(o_ref.dtype)


def _stick_breaking(qk, vt, batch, seq):
    blk, nq, hd = SB_BLOCK, SB_QUERY_BLOCKS, HEAD_DIM
    tq = nq * blk
    assert seq % tq == 0 and nq % 2 == 0
    n_blk = seq // blk
    sfx = (lax.broadcasted_iota(jnp.int32, (blk, blk), 1) > lax.broadcasted_iota(jnp.int32, (blk, blk), 0)).astype(BF16)
    pipelined = (_nbytes((blk, blk), BF16) + 2 * _nbytes((tq, hd), BF16) + 2 * _nbytes((seq, hd), BF16))
    slots = 2 * nq * (2 * _nbytes((blk, blk), BF16) + _nbytes((blk, blk), F32))
    resident = nq * _nbytes((hd, blk), F32) + slots + nq * 16 * _nbytes((blk, blk), F32)
    return pl.pallas_call(
        _sb_kernel,
        out_shape=jax.ShapeDtypeStruct((batch, seq, SB_HEADS * hd), BF16),
        grid=(batch, SB_HEADS, seq // tq),
        in_specs=[
            pl.BlockSpec((blk, blk), lambda b, h, i: (0, 0)),
            pl.BlockSpec((1, tq, hd), lambda b, h, i: (b, i, h)),
            pl.BlockSpec((1, seq, hd), lambda b, h, i: (b, 0, SB_HEADS + h)),
            pl.BlockSpec((1, n_blk, hd, blk), lambda b, h, i: (b, 0, h, 0)),
        ],
        out_specs=pl.BlockSpec((1, tq, hd), lambda b, h, i: (b, i, h)),
        scratch_shapes=[
            pltpu.VMEM((nq, hd, blk), F32),
            pltpu.VMEM((nq, 1, blk), F32),
            pltpu.VMEM((2, nq, blk, blk), BF16),
            pltpu.VMEM((2, nq, blk, blk), F32),
            pltpu.VMEM((2, nq, blk, blk), BF16),
        ],
        compiler_params=pltpu.CompilerParams(
            dimension_semantics=("arbitrary", "arbitrary", "arbitrary"),
            vmem_limit_bytes=_vmem_limit(pipelined, resident)),
        name="stick_breaking",
    )(sfx, qk, qk, vt)


def _out_proj_kernel(oa_ref, ob_ref, wa_ref, wb_ref, x_ref, o_ref):
    o_ref[...] = x_ref[...] + _dot(oa_ref[...], wa_ref[0]) + _dot(ob_ref[...], wb_ref[0])


def _out_proj(o_a, o_b, w_out, layer, x2):
    n_tok, d = x2.shape
    ka, kb = o_a.shape[1], o_b.shape[1]
    tm = OUT_PROJ_ROWS
    assert n_tok % tm == 0 and ka == kb and w_out.shape[1] == ka + kb
    pipelined = (_nbytes((tm, ka), BF16) + _nbytes((tm, kb), BF16) + _nbytes((ka, d), BF16)
                 + _nbytes((kb, d), BF16) + 2 * _nbytes((tm, d), F32))
    resident = 2 * _nbytes((tm, d), F32)
    return pl.pallas_call(
        _out_proj_kernel,
        out_shape=jax.ShapeDtypeStruct((n_tok, d), F32),
        grid=(n_tok // tm,),
        in_specs=[
            pl.BlockSpec((tm, ka), lambda i: (i, 0)),
            pl.BlockSpec((tm, kb), lambda i: (i, 0)),
            pl.BlockSpec((1, ka, d), lambda i: (layer, 0, 0)),
            pl.BlockSpec((1, kb, d), lambda i: (layer, 1, 0)),
            pl.BlockSpec((tm, d), lambda i: (i, 0)),
        ],
        out_specs=pl.BlockSpec((tm, d), lambda i: (i, 0)),
        compiler_params=pltpu.CompilerParams(
            dimension_semantics=("arbitrary",),
            vmem_limit_bytes=_vmem_limit(pipelined, resident)),
        name="out_proj",
    )(o_a, o_b, w_out, w_out, x2)


def _ffn_kernel(x_ref, g_ref, wg_ref, wu_ref, wd_ref, fg_ref, o_ref, h_ref, act_ref, *, final_norm):
    j = pl.program_id(1)
    last = pl.num_programs(1) - 1

    def activations():
        h = h_ref[...]
        act_ref[...] = (_silu(_dot(h, wg_ref[0])) * _dot(h, wu_ref[0])).astype(BF16)

    def down_projection():
        o_ref[...] += _dot(act_ref[...], wd_ref[0])

    @pl.when(j == 0)
    def _():
        x = x_ref[...]
        h_ref[...] = _rms_norm(x, g_ref[...]).astype(BF16)
        o_ref[...] = x
        activations()

    @pl.when(jnp.logical_and(j > 0, j < last))
    def _():
        down_projection()
        activations()

    @pl.when(j == last)
    def _():
        down_projection()
        if final_norm:
            o_ref[...] = _rms_norm(o_ref[...], fg_ref[...])


def _ffn(x2, gain, w_gate, w_up, w_down, layer, final_gain, final_norm, name):
    n_tok, d = x2.shape
    hidden = w_gate.shape[2]
    tm, th = FFN_ROWS, FFN_HIDDEN_TILE
    assert n_tok % tm == 0 and hidden % th == 0
    n_hidden = hidden // th
    pipelined = (2 * _nbytes((tm, d), F32) + 2 * _nbytes((d, th), BF16) + _nbytes((th, d), BF16)
                 + 2 * _nbytes((1, d), F32))
    resident = (_nbytes((tm, d), BF16) + _nbytes((tm, th), BF16) + 3 * _nbytes((tm, th), F32)
                + _nbytes((tm, d), F32))
    return pl.pallas_call(
        functools.partial(_ffn_kernel, final_norm=final_norm),
        out_shape=jax.ShapeDtypeStruct((n_tok, d), F32),
        grid=(n_tok // tm, n_hidden + 1),
        in_specs=[
            pl.BlockSpec((tm, d), lambda i, j: (i, 0)),
            pl.BlockSpec((1, d), lambda i, j: (0, 0)),
            pl.BlockSpec((1, d, th), lambda i, j: (layer, 0, jnp.minimum(j, n_hidden - 1))),
            pl.BlockSpec((1, d, th), lambda i, j: (layer, 0, jnp.minimum(j, n_hidden - 1))),
            pl.BlockSpec((1, th, d), lambda i, j: (layer, jnp.maximum(j - 1, 0), 0)),
            pl.BlockSpec((1, d), lambda i, j: (0, 0)),
        ],
        out_specs=pl.BlockSpec((tm, d), lambda i, j: (i, 0)),
        scratch_shapes=[pltpu.VMEM((tm, d), BF16), pltpu.VMEM((tm, th), BF16)],
        compiler_params=pltpu.CompilerParams(
            dimension_semantics=("arbitrary", "arbitrary"),
            vmem_limit_bytes=_vmem_limit(pipelined, resident)),
        name=name,
    )(x2, gain, w_gate, w_up, w_down, final_gain)


def _pool_kernel(x_ref, halo_ref, g_ref, pw_ref, sc_ref, o_ref, hs_ref):
    ts = x_ref.shape[1]
    halo = POOL_HALO
    i = pl.program_id(1)
    gain = g_ref[...]
    x = x_ref[0]
    hs_ref[0:halo, :] = jnp.where(i == 0, 0.0, _rms_norm(halo_ref[0], gain))
    hs_ref[halo:halo + ts, :] = _rms_norm(x, gain)

    pos = i * ts + lax.broadcasted_iota(jnp.int32, (ts, 1), 0)
    group = pw_ref.shape[1]
    for gi, win in enumerate(POOL_WINDOWS):
        cols = slice(gi * group, (gi + 1) * group)
        wsum = hs_ref[:, cols]
        for s in range(win.bit_length() - 1):
            wsum = wsum + pltpu.roll(wsum, 1 << s, axis=0)
        h = hs_ref[halo:halo + ts, cols]
        inv_count = 1.0 / jnp.minimum(pos + 1, win).astype(F32)
        pooled = wsum[halo:] * inv_count - h
        mixed = _dot(pooled.astype(BF16), pw_ref[gi])
        o_ref[0, :, cols] = x[:, cols] + mixed * sc_ref[:, cols]


def _pool_mixer(x3, gain, pool_w, scale):
    batch, seq, d = x3.shape
    ts, halo = POOL_ROWS, POOL_HALO
    n_groups, group = pool_w.shape[0], pool_w.shape[1]
    assert seq % ts == 0 and ts % halo == 0 and max(POOL_WINDOWS) - 1 <= halo
    assert all(w & (w - 1) == 0 for w in POOL_WINDOWS)
    assert n_groups == len(POOL_WINDOWS) and n_groups * group == d
    halo_per_tile = ts // halo
    pipelined = (2 * _nbytes((ts, d), F32) + _nbytes((halo, d), F32) + _nbytes(pool_w.shape, BF16)
                 + 2 * _nbytes((1, d), F32))
    resident = _nbytes((ts + halo, d), F32) + 3 * _nbytes((ts, d), F32)
    return pl.pallas_call(
        _pool_kernel,
        out_shape=jax.ShapeDtypeStruct((batch, seq, d), F32),
        grid=(batch, seq // ts),
        in_specs=[
            pl.BlockSpec((1, ts, d), lambda b, i: (b, i, 0)),
            pl.BlockSpec((1, halo, d), lambda b, i: (b, jnp.maximum(i * halo_per_tile - 1, 0), 0)),
            pl.BlockSpec((1, d), lambda b, i: (0, 0)),
            pl.BlockSpec((n_groups, group, group), lambda b, i: (0, 0, 0)),
            pl.BlockSpec((1, d), lambda b, i: (0, 0)),
        ],
        out_specs=pl.BlockSpec((1, ts, d), lambda b, i: (b, i, 0)),
        scratch_shapes=[pltpu.VMEM((ts + halo, d), F32)],
        compiler_params=pltpu.CompilerParams(
            dimension_semantics=("arbitrary", "arbitrary"),
            vmem_limit_bytes=_vmem_limit(pipelined, resident)),
        name="pool_mixer",
    )(x3, x3, gain, pool_w, scale)


def kernel(x, mix_norm, ffn_norm, final_norm, ab_w_in, lb_logits, hg_out_norm, ab_w_out,
           pool_w, pool_scale, ffn_w_gate, ffn_w_up, ffn_w_down):
    batch, seq, d = x.shape
    n_tok = batch * seq
    hg_width = HG_HEADS * HEAD_DIM
    sb_width = SB_HEADS * HEAD_DIM
    row = lambda a: a.reshape(1, -1)

    x2 = x.reshape(n_tok, d)
    proj_a, qk, vt = _in_proj(x2, row(mix_norm[0]), ab_w_in.astype(BF16), 0, batch, seq, SB_BLOCK)
    o_a = _hgrn(proj_a.reshape(batch, seq, 4 * hg_width), lb_logits, row(hg_out_norm[0]), batch, seq, layer_row=0)
    o_b = _stick_breaking(qk.reshape(batch, seq, 2 * sb_width), vt, batch, seq)
    x2 = _out_proj(o_a.reshape(n_tok, hg_width), o_b.reshape(n_tok, sb_width), ab_w_out.astype(BF16), 0, x2)

    w_gate, w_up, w_down = ffn_w_gate.astype(BF16), ffn_w_up.astype(BF16), ffn_w_down.astype(BF16)
    x2 = _ffn(x2, row(ffn_norm[0]), w_gate, w_up, w_down, 0, row(final_norm), False, "ffn0")

    x3 = _pool_mixer(x2.reshape(batch, seq, d), row(mix_norm[1]), pool_w[0].astype(BF16), row(pool_scale[0]))
    x2 = _ffn(x3.reshape(n_tok, d), row(ffn_norm[1]), w_gate, w_up, w_down, 1, row(final_norm), True,
              "ffn1_final_norm")
    return x2.reshape(batch, seq, d)
```

```python
import functools
import math

import jax
import jax.numpy as jnp
from jax import lax
from jax.experimental import pallas as pl
from jax.experimental.pallas import tpu as pltpu

F32 = jnp.float32
BF16 = jnp.bfloat16

RMS_EPS = 1e-6
HEAD_DIM = 128
HG_HEADS = 8
SB_HEADS = 8
POOL_WINDOWS = (2, 4, 8, 16)

V7X_VMEM_BYTES = 64 * 1024 * 1024
V7X_VMEM_USABLE_BYTES = 56 * 1024 * 1024
F32_SUBLANES = 8
LOG2_E = math.log2(math.e)

HG_CHUNK = 128
HG_STEP_TOKENS = 1024
SB_BLOCK = 256
SB_QUERY_BLOCKS = 8
SB_UNDERFLOW_BITS = 160.0
PROJ_ROWS = 1024
PROJ_COLS = 1024
OUT_PROJ_ROWS = 512
FFN_ROWS = 512
FFN_HIDDEN_TILE = 512
POOL_ROWS = 512
CAST_SLAB = 128
POOL_HALO = 16


def _vmem_limit(pipelined_bytes, resident_bytes):
    need = 2 * pipelined_bytes + resident_bytes
    assert need <= V7X_VMEM_USABLE_BYTES, (need, V7X_VMEM_USABLE_BYTES)
    return int(min(V7X_VMEM_USABLE_BYTES, max(need, 16 * 1024 * 1024)))


def _nbytes(shape, dtype):
    return math.prod(shape) * jnp.dtype(dtype).itemsize


def _rms_norm(x, gain):
    ms = jnp.mean(x * x, axis=-1, keepdims=True)
    return x * lax.rsqrt(ms + RMS_EPS) * gain


def _dot(a, b):
    return jnp.dot(a, b, preferred_element_type=F32)


def _dot_nt(a, b):
    return lax.dot_general(a, b, (((1,), (1,)), ((), ())), preferred_element_type=F32)


def _silu(x):
    return x * jax.nn.sigmoid(x)


def _in_proj_kernel(x_ref, g_ref, w_ref, oa_ref, oqk_ref, ovt_ref, h_ref, *, n_a_tiles, q_scale, key_block):
    j = pl.program_id(1)

    @pl.when(j == 0)
    def _():
        h_ref[...] = _rms_norm(x_ref[...], g_ref[...]).astype(BF16)

    def tile():
        return _dot(h_ref[...], w_ref[0])

    @pl.when(j < n_a_tiles)
    def _():
        oa_ref[...] = tile()

    @pl.when(j == n_a_tiles)
    def _():
        oqk_ref[...] = (tile() * q_scale).astype(oqk_ref.dtype)

    @pl.when(j == n_a_tiles + 1)
    def _():
        oqk_ref[...] = tile().astype(oqk_ref.dtype)

    @pl.when(j == n_a_tiles + 2)
    def _():
        vt = tile().T.astype(ovt_ref.dtype)
        for kb in range(ovt_ref.shape[1]):
            ovt_ref[0, kb] = vt[:, kb * key_block:(kb + 1) * key_block]


def _in_proj(x2, gain, w_in, layer, batch, seq, key_block):
    n_tok, d = x2.shape
    tm, tn = PROJ_ROWS, PROJ_COLS
    hg_width, sb_width = HG_HEADS * HEAD_DIM, SB_HEADS * HEAD_DIM
    assert tn == hg_width == sb_width and w_in.shape[2] == 7 * tn
    assert seq % tm == 0 and tm % key_block == 0
    n_a_tiles = 4
    tiles_per_seq = seq // tm
    kb_per_tile = tm // key_block
    pipelined = (_nbytes((tm, d), F32) + _nbytes((d, tn), BF16) + _nbytes((tm, tn), F32)
                 + 2 * _nbytes((tm, tn), BF16) + _nbytes((1, d), F32))
    resident = _nbytes((tm, d), BF16) + 2 * _nbytes((tm, tn), F32)
    return pl.pallas_call(
        functools.partial(_in_proj_kernel, n_a_tiles=n_a_tiles, q_scale=LOG2_E / math.sqrt(HEAD_DIM),
                          key_block=key_block),
        out_shape=(jax.ShapeDtypeStruct((n_tok, n_a_tiles * tn), F32),
                   jax.ShapeDtypeStruct((n_tok, 2 * tn), BF16),
                   jax.ShapeDtypeStruct((batch, seq // key_block, tn, key_block), BF16)),
        grid=(n_tok // tm, 7),
        in_specs=[
            pl.BlockSpec((tm, d), lambda i, j: (i, 0)),
            pl.BlockSpec((1, d), lambda i, j: (0, 0)),
            pl.BlockSpec((1, d, tn), lambda i, j: (layer, 0, j)),
        ],
        out_specs=(
            pl.BlockSpec((tm, tn), lambda i, j: (i, jnp.minimum(j, n_a_tiles - 1))),
            pl.BlockSpec((tm, tn), lambda i, j: (i, jnp.clip(j - n_a_tiles, 0, 1))),
            pl.BlockSpec((1, kb_per_tile, tn, key_block),
                         lambda i, j: (i // tiles_per_seq, i % tiles_per_seq, 0, 0)),
        ),
        scratch_shapes=[pltpu.VMEM((tm, d), BF16)],
        compiler_params=pltpu.CompilerParams(
            dimension_semantics=("arbitrary", "arbitrary"),
            vmem_limit_bytes=_vmem_limit(pipelined, resident)),
        name="in_proj",
    )(x2, gain, w_in)


def _cast_rider_specs(weights, layer, grid, step_of):
    w_gate, w_up, w_down = weights
    _, d, hidden = w_gate.shape
    assert w_up.shape == w_gate.shape and w_down.shape[1:] == (hidden, d) and hidden % CAST_SLAB == 0
    n_slabs = max(n for n in range(1, hidden // CAST_SLAB + 1)
                  if (hidden // CAST_SLAB) % n == 0 and n <= math.prod(grid))
    width = hidden // n_slabs

    def slab(*grid_idx):
        return jnp.minimum(step_of(*grid_idx), n_slabs - 1)

    col_in = pl.BlockSpec((1, d, width), lambda *g: (layer, 0, slab(*g)))
    col_out = pl.BlockSpec((1, d, width), lambda *g: (0, 0, slab(*g)))
    row_in = pl.BlockSpec((1, width, d), lambda *g: (layer, slab(*g), 0))
    row_out = pl.BlockSpec((1, width, d), lambda *g: (0, slab(*g), 0))
    out_shapes = [jax.ShapeDtypeStruct((1, d, hidden), BF16), jax.ShapeDtypeStruct((1, d, hidden), BF16),
                  jax.ShapeDtypeStruct((1, hidden, d), BF16)]
    slab_bytes = 3 * (_nbytes((d, width), F32) + _nbytes((d, width), BF16))
    return [col_in, col_in, row_in], [col_out, col_out, row_out], out_shapes, n_slabs, slab_bytes


def _cast_rider(step, n_slabs, src_refs, dst_refs):
    @pl.when(step < n_slabs)
    def _():
        for src, dst in zip(src_refs, dst_refs):
            dst[...] = src[...].astype(dst.dtype)


def _split2_bf16(x):
    hi = x.astype(BF16)
    return hi, (x - hi.astype(F32)).astype(BF16)


def _sublane_broadcast(x, group, row):
    rows, lanes = x.shape
    x3 = x.reshape(rows // group, group, lanes)
    return jnp.broadcast_to(x3[:, row:row + 1, :], x3.shape).reshape(rows, lanes)


def _hgrn_kernel(lbl_ref, gn_ref, tri_ref, lvl_ref, qa_ref, fa_ref, ia_ref, ga_ref, wg_ref, wu_ref, wd_ref,
                 o_ref, wg16_ref, wu16_ref, wd16_ref, st_ref, *, layer_row, n_slabs):
    c, ts = HG_CHUNK, HG_STEP_TOKENS
    n_levels = c.bit_length() - 1
    chunks = [slice(j * c, (j + 1) * c) for j in range(ts // c)]

    step = (pl.program_id(0) * pl.num_programs(1) + pl.program_id(1)) * pl.num_programs(2) + pl.program_id(2)
    _cast_rider(step, n_slabs, (wg_ref, wu_ref, wd_ref), (wg16_ref, wu16_ref, wd16_ref))

    @pl.when(pl.program_id(2) == 0)
    def _():
        st_ref[...] = jnp.zeros_like(st_ref)

    logits = lbl_ref[...]
    e = jnp.exp(logits - jnp.max(logits, axis=0, keepdims=True))
    lower = jnp.sum(e[:layer_row + 1], axis=0, keepdims=True) / jnp.sum(e, axis=0, keepdims=True)

    row = lax.broadcasted_iota(jnp.int32, (ts, HEAD_DIM), 0)
    q = _silu(qa_ref[0])
    f = lower + (1.0 - lower) * jax.nn.sigmoid(fa_ref[0])
    k = 1.0 - f
    v = ia_ref[0]
    v16 = v.astype(BF16)

    tri = tri_ref[...]
    parts = _split2_bf16(jnp.log(f) * LOG2_E)
    g = jnp.concatenate([sum(_dot(tri, p[ch]) for p in parts) for ch in chunks], axis=0)

    level_of = lvl_ref[...]
    att = [jnp.zeros((c, c), F32) for _ in chunks]
    for lvl in range(n_levels):
        half = 1 << lvl
        if half >= F32_SUBLANES:
            shape4 = (ts // (2 * half), 2, half, HEAD_DIM)
            g4, q4, k4 = g.reshape(shape4), q.reshape(shape4), k.reshape(shape4)
            g_mid = g4[:, 0:1, half - 1:half, :]
            x_lo = k4[:, 0:1] * jnp.exp2(g_mid - g4[:, 0:1])
            x_hi = q4[:, 1:2] * jnp.exp2(g4[:, 1:2] - g_mid)
            x = jnp.concatenate([x_lo, x_hi], axis=1).reshape(ts, HEAD_DIM)
        else:
            upper = (row & half) != 0
            if lvl == 0:
                x = jnp.where(upper, q * f, k)
            else:
                g_mid = _sublane_broadcast(g, F32_SUBLANES, half - 1)
                if 2 * half < F32_SUBLANES:
                    second = _sublane_broadcast(g, F32_SUBLANES, 2 * half + half - 1)
                    g_mid = jnp.where((row & (F32_SUBLANES - 1)) < 2 * half, g_mid, second)
                x = jnp.where(upper, q, k) * jnp.exp2(-jnp.abs(g - g_mid))
        x16 = x.astype(BF16)
        for j, ch in enumerate(chunks):
            att[j] = jnp.where(level_of == lvl, _dot_nt(x16[ch], x16[ch]), att[j])
    qk_diag = jnp.sum(q * k, axis=-1, keepdims=True)

    g_last = _sublane_broadcast(g, c, c - 1)
    q_dec = (q * jnp.exp2(g)).astype(BF16)
    k_dec = (k * jnp.exp2(g_last - g)).astype(BF16)
    st = st_ref[...]
    outs = []
    for j, ch in enumerate(chunks):
        a = jnp.where(level_of == n_levels, qk_diag[ch], att[j])
        outs.append(_dot_nt(q_dec[ch], st.astype(BF16)) + _dot(a.astype(BF16), v16[ch]))
        st = st * jnp.exp2(g_last[ch][:1]) + _dot(v[ch].T.astype(BF16), k_dec[ch])
    st_ref[...] = st

    y = _rms_norm(jnp.concatenate(outs, axis=0), gn_ref[...]) * _silu(ga_ref[0])
    o_ref[0] = y.astype(o_ref.dtype)


def _hgrn(proj, lb_logits, out_gain, batch, seq, layer_row, cast_weights, cast_layer):
    ts, c, hd = HG_STEP_TOKENS, HG_CHUNK, HEAD_DIM
    assert seq % ts == 0 and ts % c == 0
    n_rows = lb_logits.shape[0]
    grid = (batch, HG_HEADS, seq // ts)
    cast_in, cast_out, cast_shapes, n_slabs, slab_bytes = _cast_rider_specs(
        cast_weights, cast_layer, grid, lambda b, h, s: (b * grid[1] + h) * grid[2] + s)
    t_idx = lax.broadcasted_iota(jnp.int32, (c, c), 0)
    s_idx = lax.broadcasted_iota(jnp.int32, (c, c), 1)
    tri = (t_idx >= s_idx).astype(BF16)
    level_of = jnp.where(t_idx > s_idx, 31 - lax.clz(t_idx ^ s_idx),
                         jnp.where(t_idx == s_idx, c.bit_length() - 1, -1)).astype(jnp.int32)

    def head_block(part):
        return pl.BlockSpec((1, ts, hd), lambda b, h, s: (b, s, part * HG_HEADS + h))

    pipelined = (4 * _nbytes((ts, hd), F32) + _nbytes((ts, hd), BF16) + _nbytes((c, c), BF16)
                 + _nbytes((c, c), jnp.int32) + slab_bytes)
    resident = 24 * _nbytes((ts, hd), F32)
    return pl.pallas_call(
        functools.partial(_hgrn_kernel, layer_row=layer_row, n_slabs=n_slabs),
        out_shape=[jax.ShapeDtypeStruct((batch, seq, HG_HEADS * hd), BF16)] + cast_shapes,
        grid=grid,
        in_specs=[
            pl.BlockSpec((n_rows, hd), lambda b, h, s: (0, h)),
            pl.BlockSpec((1, hd), lambda b, h, s: (0, 0)),
            pl.BlockSpec((c, c), lambda b, h, s: (0, 0)),
            pl.BlockSpec((c, c), lambda b, h, s: (0, 0)),
            head_block(0), head_block(1), head_block(2), head_block(3),
        ] + cast_in,
        out_specs=[pl.BlockSpec((1, ts, hd), lambda b, h, s: (b, s, h))] + cast_out,
        scratch_shapes=[pltpu.VMEM((hd, hd), F32)],
        compiler_params=pltpu.CompilerParams(
            dimension_semantics=("arbitrary", "arbitrary", "arbitrary"),
            vmem_limit_bytes=_vmem_limit(pipelined, resident)),
        name="hgrn2",
    )(lb_logits, out_gain, tri, level_of, proj, proj, proj, proj, *cast_weights)


def _sb_kernel(sfx_ref, q_ref, k_ref, vt_ref, wg_ref, wu_ref, wd_ref, o_ref, wg16_ref, wu16_ref, wd16_ref,
               acc_ref, carry_ref, sp_ref, lbc_ref, w_ref, *, n_slabs):
    blk, nq = SB_BLOCK, SB_QUERY_BLOCKS
    step = (pl.program_id(0) * pl.num_programs(1) + pl.program_id(1)) * pl.num_programs(2) + pl.program_id(2)
    _cast_rider(step, n_slabs, (wg_ref, wu_ref, wd_ref), (wg16_ref, wu16_ref, wd16_ref))

    first = nq * pl.program_id(2)
    n_depths = first + nq
    causal = (lax.broadcasted_iota(jnp.int32, (blk, blk), 0)
              < lax.broadcasted_iota(jnp.int32, (blk, blk), 1))
    sfx = sfx_ref[...]

    def key_block(h, depth, always_live):
        kb = first + h - depth
        return (kb, None) if always_live else (jnp.maximum(kb, 0), kb >= 0)

    def scores(depth, slot, on_diagonal=False):
        for h in range(nq):
            j, live = key_block(h, depth, on_diagonal)
            ks = k_ref[0, pl.ds(pl.multiple_of(j * blk, blk), blk), :]
            z = _dot_nt(ks, q_ref[0, h * blk:(h + 1) * blk, :])
            sp = jnp.maximum(z, 0.0) + jnp.log2(1.0 + jnp.exp2(-jnp.abs(z)))
            lbc_ref[slot, h] = (z - sp) - carry_ref[h]
            if on_diagonal:
                sp = jnp.where(causal, sp, 0.0)
            sp_ref[slot, h] = sp.astype(BF16)
            total = jnp.sum(sp, axis=0, keepdims=True)
            carry_ref[h] += total if live is None else jnp.where(live, total, 0.0)

    def weights(slot, on_diagonal=False):
        for h in range(nq):
            w = jnp.exp2(lbc_ref[slot, h] - _dot(sfx, sp_ref[slot, h]))
            if on_diagonal:
                w = jnp.where(causal, w, 0.0)
            w_ref[slot, h] = w.astype(BF16)

    def values(depth, slot, on_diagonal=False):
        for h in range(nq):
            j, live = key_block(h, depth, on_diagonal)
            pv = _dot(vt_ref[0, j], w_ref[slot, h])
            acc_ref[h] += pv if live is None else jnp.where(live, pv, 0.0)

    def min_open_carry(next_depth):
        h_idx = lax.broadcasted_iota(jnp.int32, carry_ref.shape, 0)
        return jnp.min(jnp.where(first + h_idx - next_depth >= 0, carry_ref[...], SB_UNDERFLOW_BITS))

    acc_ref[...] = jnp.zeros_like(acc_ref)
    carry_ref[...] = jnp.zeros_like(carry_ref)

    scores(0, 0, on_diagonal=True)
    scores(1, 1)
    weights(0, on_diagonal=True)

    def keep_going(state):
        pairs, open_carry = state
        return jnp.logical_and(pairs < (n_depths - 2) // 2, open_carry < SB_UNDERFLOW_BITS)

    def body(state):
        pairs, _ = state
        d = 2 * pairs + 2
        scores(d, 0)
        weights(1)
        values(d - 2, 0)
        scores(d + 1, 1)
        weights(0)
        values(d - 1, 1)
        return pairs + 1, min_open_carry(d + 2)

    pairs_done, _ = lax.while_loop(keep_going, body, (jnp.int32(0), min_open_carry(2)))
    scored = 2 * pairs_done + 2
    weights(1)
    values(scored - 2, 0)
    values(scored - 1, 1)

    for h in range(nq):
        o_ref[0, h * blk:(h + 1) * blk, :] = acc_ref[h].T.astype(o_ref.dtype)


def _stick_breaking(qk, vt, batch, seq, cast_weights, cast_layer):
    blk, nq, hd = SB_BLOCK, SB_QUERY_BLOCKS, HEAD_DIM
    tq = nq * blk
    assert seq % tq == 0 and nq % 2 == 0
    n_blk = seq // blk
    grid = (batch, SB_HEADS, seq // tq)
    cast_in, cast_out, cast_shapes, n_slabs, slab_bytes = _cast_rider_specs(
        cast_weights, cast_layer, grid, lambda b, h, i: (b * grid[1] + h) * grid[2] + i)
    sfx = (lax.broadcasted_iota(jnp.int32, (blk, blk), 1) > lax.broadcasted_iota(jnp.int32, (blk, blk), 0)).astype(BF16)
    pipelined = (_nbytes((blk, blk), BF16) + 2 * _nbytes((tq, hd), BF16) + 2 * _nbytes((seq, hd), BF16)
                 + slab_bytes)
    slots = 2 * nq * (2 * _nbytes((blk, blk), BF16) + _nbytes((blk, blk), F32))
    resident = nq * _nbytes((hd, blk), F32) + slots + nq * 4 * _nbytes((blk, blk), F32)
    return pl.pallas_call(
        functools.partial(_sb_kernel, n_slabs=n_slabs),
        out_shape=[jax.ShapeDtypeStruct((batch, seq, SB_HEADS * hd), BF16)] + cast_shapes,
        grid=grid,
        in_specs=[
            pl.BlockSpec((blk, blk), lambda b, h, i: (0, 0)),
            pl.BlockSpec((1, tq, hd), lambda b, h, i: (b, i, h)),
            pl.BlockSpec((1, seq, hd), lambda b, h, i: (b, 0, SB_HEADS + h)),
            pl.BlockSpec((1, n_blk, hd, blk), lambda b, h, i: (b, 0, h, 0)),
        ] + cast_in,
        out_specs=[pl.BlockSpec((1, tq, hd), lambda b, h, i: (b, i, h))] + cast_out,
        scratch_shapes=[
            pltpu.VMEM((nq, hd, blk), F32),
            pltpu.VMEM((nq, 1, blk), F32),
            pltpu.VMEM((2, nq, blk, blk), BF16),
            pltpu.VMEM((2, nq, blk, blk), F32),
            pltpu.VMEM((2, nq, blk, blk), BF16),
        ],
        compiler_params=pltpu.CompilerParams(
            dimension_semantics=("arbitrary", "arbitrary", "arbitrary"),
            vmem_limit_bytes=_vmem_limit(pipelined, resident)),
        name="stick_breaking",
    )(sfx, qk, qk, vt, *cast_weights)


def _out_proj_kernel(oa_ref, ob_ref, wa_ref, wb_ref, x_ref, o_ref):
    o_ref[...] = x_ref[...] + _dot(oa_ref[...], wa_ref[0]) + _dot(ob_ref[...], wb_ref[0])


def _out_proj(o_a, o_b, w_out, layer, x2):
    n_tok, d = x2.shape
    ka, kb = o_a.shape[1], o_b.shape[1]
    tm = OUT_PROJ_ROWS
    assert n_tok % tm == 0 and ka == kb and w_out.shape[1] == ka + kb
    pipelined = (_nbytes((tm, ka), BF16) + _nbytes((tm, kb), BF16) + _nbytes((ka, d), BF16)
                 + _nbytes((kb, d), BF16) + 2 * _nbytes((tm, d), F32))
    resident = 2 * _nbytes((tm, d), F32)
    return pl.pallas_call(
        _out_proj_kernel,
        out_shape=jax.ShapeDtypeStruct((n_tok, d), F32),
        grid=(n_tok // tm,),
        in_specs=[
            pl.BlockSpec((tm, ka), lambda i: (i, 0)),
            pl.BlockSpec((tm, kb), lambda i: (i, 0)),
            pl.BlockSpec((1, ka, d), lambda i: (layer, 0, 0)),
            pl.BlockSpec((1, kb, d), lambda i: (layer, 1, 0)),
            pl.BlockSpec((tm, d), lambda i: (i, 0)),
        ],
        out_specs=pl.BlockSpec((tm, d), lambda i: (i, 0)),
        compiler_params=pltpu.CompilerParams(
            dimension_semantics=("arbitrary",),
            vmem_limit_bytes=_vmem_limit(pipelined, resident)),
        name="out_proj",
    )(o_a, o_b, w_out, w_out, x2)


def _ffn_kernel(x_ref, g_ref, wg_ref, wu_ref, wd_ref, fg_ref, o_ref, h_ref, *, final_norm):
    j = pl.program_id(1)

    @pl.when(j == 0)
    def _():
        x = x_ref[...]
        h_ref[...] = _rms_norm(x, g_ref[...]).astype(BF16)
        o_ref[...] = x

    h = h_ref[...]
    act = _silu(_dot(h, wg_ref[0])) * _dot(h, wu_ref[0])
    o_ref[...] += _dot(act.astype(BF16), wd_ref[0])

    if final_norm:
        @pl.when(j == pl.num_programs(1) - 1)
        def _():
            o_ref[...] = _rms_norm(o_ref[...], fg_ref[...])


def _ffn(x2, gain, w_gate, w_up, w_down, layer, final_gain, final_norm, name):
    n_tok, d = x2.shape
    hidden = w_gate.shape[2]
    tm, th = FFN_ROWS, FFN_HIDDEN_TILE
    assert n_tok % tm == 0 and hidden % th == 0
    pipelined = (2 * _nbytes((tm, d), F32) + 2 * _nbytes((d, th), BF16) + _nbytes((th, d), BF16)
                 + 2 * _nbytes((1, d), F32))
    resident = _nbytes((tm, d), BF16) + 4 * _nbytes((tm, th), F32) + _nbytes((tm, d), F32)
    return pl.pallas_call(
        functools.partial(_ffn_kernel, final_norm=final_norm),
        out_shape=jax.ShapeDtypeStruct((n_tok, d), F32),
        grid=(n_tok // tm, hidden // th),
        in_specs=[
            pl.BlockSpec((tm, d), lambda i, j: (i, 0)),
            pl.BlockSpec((1, d), lambda i, j: (0, 0)),
            pl.BlockSpec((1, d, th), lambda i, j: (layer, 0, j)),
            pl.BlockSpec((1, d, th), lambda i, j: (layer, 0, j)),
            pl.BlockSpec((1, th, d), lambda i, j: (layer, j, 0)),
            pl.BlockSpec((1, d), lambda i, j: (0, 0)),
        ],
        out_specs=pl.BlockSpec((tm, d), lambda i, j: (i, 0)),
        scratch_shapes=[pltpu.VMEM((tm, d), BF16)],
        compiler_params=pltpu.CompilerParams(
            dimension_semantics=("arbitrary", "arbitrary"),
            vmem_limit_bytes=_vmem_limit(pipelined, resident)),
        name=name,
    )(x2, gain, w_gate, w_up, w_down, final_gain)


def _pool_kernel(x_ref, halo_ref, g_ref, pw_ref, sc_ref, o_ref, hs_ref):
    ts = x_ref.shape[1]
    halo = POOL_HALO
    i = pl.program_id(1)
    gain = g_ref[...]
    x = x_ref[0]
    hs_ref[0:halo, :] = jnp.where(i == 0, 0.0, _rms_norm(halo_ref[0], gain))
    hs_ref[halo:halo + ts, :] = _rms_norm(x, gain)

    pos = i * ts + lax.broadcasted_iota(jnp.int32, (ts, 1), 0)
    group = pw_ref.shape[1]
    for gi, win in enumerate(POOL_WINDOWS):
        cols = slice(gi * group, (gi + 1) * group)
        wsum = hs_ref[:, cols]
        for s in range(win.bit_length() - 1):
            wsum = wsum + pltpu.roll(wsum, 1 << s, axis=0)
        h = hs_ref[halo:halo + ts, cols]
        inv_count = 1.0 / jnp.minimum(pos + 1, win).astype(F32)
        pooled = wsum[halo:] * inv_count - h
        mixed = _dot(pooled.astype(BF16), pw_ref[gi])
        o_ref[0, :, cols] = x[:, cols] + mixed * sc_ref[:, cols]


def _pool_mixer(x3, gain, pool_w, scale):
    batch, seq, d = x3.shape
    ts, halo = POOL_ROWS, POOL_HALO
    n_groups, group = pool_w.shape[0], pool_w.shape[1]
    assert seq % ts == 0 and ts % halo == 0 and max(POOL_WINDOWS) - 1 <= halo
    assert all(w & (w - 1) == 0 for w in POOL_WINDOWS)
    assert n_groups == len(POOL_WINDOWS) and n_groups * group == d
    halo_per_tile = ts // halo
    pipelined = (2 * _nbytes((ts, d), F32) + _nbytes((halo, d), F32) + _nbytes(pool_w.shape, BF16)
                 + 2 * _nbytes((1, d), F32))
    resident = _nbytes((ts + halo, d), F32) + 3 * _nbytes((ts, d), F32)
    return pl.pallas_call(
        _pool_kernel,
        out_shape=jax.ShapeDtypeStruct((batch, seq, d), F32),
        grid=(batch, seq // ts),
        in_specs=[
            pl.BlockSpec((1, ts, d), lambda b, i: (b, i, 0)),
            pl.BlockSpec((1, halo, d), lambda b, i: (b, jnp.maximum(i * halo_per_tile - 1, 0), 0)),
            pl.BlockSpec((1, d), lambda b, i: (0, 0)),
            pl.BlockSpec((n_groups, group, group), lambda b, i: (0, 0, 0)),
            pl.BlockSpec((1, d), lambda b, i: (0, 0)),
        ],
        out_specs=pl.BlockSpec((1, ts, d), lambda b, i: (b, i, 0)),
        scratch_shapes=[pltpu.VMEM((ts + halo, d), F32)],
        compiler_params=pltpu.CompilerParams(
            dimension_semantics=("arbitrary", "arbitrary"),
            vmem_limit_bytes=_vmem_limit(pipelined, resident)),
        name="pool_mixer",
    )(x3, x3, gain, pool_w, scale)


def kernel(x, mix_norm, ffn_norm, final_norm, ab_w_in, lb_logits, hg_out_norm, ab_w_out,
           pool_w, pool_scale, ffn_w_gate, ffn_w_up, ffn_w_down):
    batch, seq, d = x.shape
    n_tok = batch * seq
    hg_width = HG_HEADS * HEAD_DIM
    sb_width = SB_HEADS * HEAD_DIM
    row = lambda a: a.reshape(1, -1)

    x2 = x.reshape(n_tok, d)
    ffn_weights = (ffn_w_gate, ffn_w_up, ffn_w_down)
    proj_a, qk, vt = _in_proj(x2, row(mix_norm[0]), ab_w_in.astype(BF16), 0, batch, seq, SB_BLOCK)
    o_a, *ffn1_weights = _hgrn(proj_a.reshape(batch, seq, 4 * hg_width), lb_logits, row(hg_out_norm[0]),
                               batch, seq, 0, ffn_weights, 1)
    o_b, *ffn0_weights = _stick_breaking(qk.reshape(batch, seq, 2 * sb_width), vt, batch, seq, ffn_weights, 0)
    x2 = _out_proj(o_a.reshape(n_tok, hg_width), o_b.reshape(n_tok, sb_width), ab_w_out.astype(BF16), 0, x2)
    x2 = _ffn(x2, row(ffn_norm[0]), *ffn0_weights, 0, row(final_norm), False, "ffn0")

    x3 = _pool_mixer(x2.reshape(batch, seq, d), row(mix_norm[1]), pool_w[0].astype(BF16), row(pool_scale[0]))
    x2 = _ffn(x3.reshape(n_tok, d), row(ffn_norm[1]), *ffn1_weights, 0, row(final_norm), True, "ffn1_final_norm")
    return x2.reshape(batch, seq, d)
```

```python
import functools
import math

import jax
import jax.numpy as jnp
from jax import lax
from jax.experimental import pallas as pl
from jax.experimental.pallas import tpu as pltpu

F32 = jnp.float32
BF16 = jnp.bfloat16

RMS_EPS = 1e-6
HEAD_DIM = 128
HG_HEADS = 8
SB_HEADS = 8
POOL_WINDOWS = (2, 4, 8, 16)

V7X_VMEM_BYTES = 64 * 1024 * 1024
V7X_VMEM_USABLE_BYTES = V7X_VMEM_BYTES * 13 // 16
V7X_DEFAULT_SCOPED_VMEM_BYTES = 32 * 1024 * 1024
F32_SUBLANES = 8
LOG2_E = math.log2(math.e)

HG_CHUNK = 128
HG_STEP_TOKENS = 1024
SB_BLOCK = 256
SB_QUERY_BLOCKS = 8
SB_UNDERFLOW_BITS = 160.0
PROJ_ROWS = 1024
PROJ_COLS = 1024
OUT_PROJ_ROWS = 512
FFN_ROWS = 512
FFN_HIDDEN_TILE = 512
POOL_ROWS = 512
CAST_SLAB = 128
POOL_HALO = 16


def _vmem_limit(pipelined_bytes, resident_bytes):
    need = 2 * pipelined_bytes + resident_bytes
    assert need <= V7X_VMEM_USABLE_BYTES, (need, V7X_VMEM_USABLE_BYTES)
    return int(max(need, V7X_DEFAULT_SCOPED_VMEM_BYTES))


def _nbytes(shape, dtype):
    return math.prod(shape) * jnp.dtype(dtype).itemsize


def _rms_norm(x, gain):
    ms = jnp.mean(x * x, axis=-1, keepdims=True)
    return x * lax.rsqrt(ms + RMS_EPS) * gain


def _dot(a, b):
    return jnp.dot(a, b, preferred_element_type=F32)


def _dot_nt(a, b):
    return lax.dot_general(a, b, (((1,), (1,)), ((), ())), preferred_element_type=F32)


def _sigmoid(x):
    return 0.5 * jnp.tanh(0.5 * x) + 0.5


def _silu(x):
    return x * _sigmoid(x)


def _in_proj_kernel(x_ref, g_ref, w_ref, oa_ref, oqk_ref, ovt_ref, h_ref, *, n_a_tiles, q_scale, key_block):
    j = pl.program_id(1)

    @pl.when(j == 0)
    def _():
        h_ref[...] = _rms_norm(x_ref[...], g_ref[...]).astype(BF16)

    def tile():
        return _dot(h_ref[...], w_ref[0])

    @pl.when(j < n_a_tiles)
    def _():
        oa_ref[...] = tile()

    @pl.when(j == n_a_tiles)
    def _():
        oqk_ref[...] = (tile() * q_scale).astype(oqk_ref.dtype)

    @pl.when(j == n_a_tiles + 1)
    def _():
        oqk_ref[...] = tile().astype(oqk_ref.dtype)

    @pl.when(j == n_a_tiles + 2)
    def _():
        vt = tile().T.astype(ovt_ref.dtype)
        for kb in range(ovt_ref.shape[1]):
            ovt_ref[0, kb] = vt[:, kb * key_block:(kb + 1) * key_block]


def _in_proj(x2, gain, w_in, layer, batch, seq, key_block):
    n_tok, d = x2.shape
    tm, tn = PROJ_ROWS, PROJ_COLS
    hg_width, sb_width = HG_HEADS * HEAD_DIM, SB_HEADS * HEAD_DIM
    n_a_tiles, n_qk_tiles, n_v_tiles = 4 * hg_width // tn, 2 * sb_width // tn, sb_width // tn
    assert tn == hg_width == sb_width and w_in.shape[2] == (n_a_tiles + n_qk_tiles + n_v_tiles) * tn
    assert seq % tm == 0 and tm % key_block == 0
    tiles_per_seq = seq // tm
    kb_per_tile = tm // key_block
    pipelined = (_nbytes((tm, d), F32) + _nbytes((d, tn), BF16) + _nbytes((tm, tn), F32)
                 + 2 * _nbytes((tm, tn), BF16) + _nbytes((1, d), F32))
    resident = _nbytes((tm, d), BF16) + _nbytes((tm, tn), F32)
    return pl.pallas_call(
        functools.partial(_in_proj_kernel, n_a_tiles=n_a_tiles, q_scale=LOG2_E / math.sqrt(HEAD_DIM),
                          key_block=key_block),
        out_shape=(jax.ShapeDtypeStruct((n_tok, n_a_tiles * tn), F32),
                   jax.ShapeDtypeStruct((n_tok, 2 * tn), BF16),
                   jax.ShapeDtypeStruct((batch, seq // key_block, tn, key_block), BF16)),
        grid=(n_tok // tm, n_a_tiles + n_qk_tiles + n_v_tiles),
        in_specs=[
            pl.BlockSpec((tm, d), lambda i, j: (i, 0)),
            pl.BlockSpec((1, d), lambda i, j: (0, 0)),
            pl.BlockSpec((1, d, tn), lambda i, j: (layer, 0, j)),
        ],
        out_specs=(
            pl.BlockSpec((tm, tn), lambda i, j: (i, jnp.minimum(j, n_a_tiles - 1))),
            pl.BlockSpec((tm, tn), lambda i, j: (i, jnp.clip(j - n_a_tiles, 0, n_qk_tiles - 1))),
            pl.BlockSpec((1, kb_per_tile, tn, key_block),
                         lambda i, j: (i // tiles_per_seq, i % tiles_per_seq, 0, 0)),
        ),
        scratch_shapes=[pltpu.VMEM((tm, d), BF16)],
        compiler_params=pltpu.CompilerParams(
            dimension_semantics=("arbitrary", "arbitrary"),
            vmem_limit_bytes=_vmem_limit(pipelined, resident)),
        name="in_proj",
    )(x2, gain, w_in)


def _cast_rider_specs(weights, layer, grid, step_of):
    w_gate, w_up, w_down = weights
    _, d, hidden = w_gate.shape
    assert w_up.shape == w_gate.shape and w_down.shape[1:] == (hidden, d) and hidden % CAST_SLAB == 0
    n_slabs = max(n for n in range(1, hidden // CAST_SLAB + 1)
                  if (hidden // CAST_SLAB) % n == 0 and n <= math.prod(grid))
    width = hidden // n_slabs

    def slab(*grid_idx):
        return jnp.minimum(step_of(*grid_idx), n_slabs - 1)

    col_in = pl.BlockSpec((1, d, width), lambda *g: (layer, 0, slab(*g)))
    col_out = pl.BlockSpec((1, d, width), lambda *g: (0, 0, slab(*g)))
    row_in = pl.BlockSpec((1, width, d), lambda *g: (layer, slab(*g), 0))
    row_out = pl.BlockSpec((1, width, d), lambda *g: (0, slab(*g), 0))
    out_shapes = [jax.ShapeDtypeStruct((1, d, hidden), BF16), jax.ShapeDtypeStruct((1, d, hidden), BF16),
                  jax.ShapeDtypeStruct((1, hidden, d), BF16)]
    slab_bytes = 3 * (_nbytes((d, width), F32) + _nbytes((d, width), BF16))
    return [col_in, col_in, row_in], [col_out, col_out, row_out], out_shapes, n_slabs, slab_bytes


def _cast_rider(step, n_slabs, src_refs, dst_refs):
    @pl.when(step < n_slabs)
    def _():
        for src, dst in zip(src_refs, dst_refs):
            dst[...] = src[...].astype(dst.dtype)


def _split2_bf16(x):
    hi = x.astype(BF16)
    return hi, (x - hi.astype(F32)).astype(BF16)


def _sublane_broadcast(x, group, row):
    rows, lanes = x.shape
    x3 = x.reshape(rows // group, group, lanes)
    return jnp.broadcast_to(x3[:, row:row + 1, :], x3.shape).reshape(rows, lanes)


def _hgrn_kernel(lbl_ref, gn_ref, tri_ref, lvl_ref, qa_ref, fa_ref, ia_ref, ga_ref, wg_ref, wu_ref, wd_ref,
                 o_ref, wg16_ref, wu16_ref, wd16_ref, st_ref, *, layer_row, n_slabs):
    c, ts = HG_CHUNK, HG_STEP_TOKENS
    n_levels = c.bit_length() - 1
    chunks = [slice(j * c, (j + 1) * c) for j in range(ts // c)]

    step = (pl.program_id(0) * pl.num_programs(1) + pl.program_id(1)) * pl.num_programs(2) + pl.program_id(2)
    _cast_rider(step, n_slabs, (wg_ref, wu_ref, wd_ref), (wg16_ref, wu16_ref, wd16_ref))

    @pl.when(pl.program_id(2) == 0)
    def _():
        st_ref[...] = jnp.zeros_like(st_ref)

    logits = lbl_ref[...]
    e = jnp.exp(logits - jnp.max(logits, axis=0, keepdims=True))
    lower = jnp.sum(e[:layer_row + 1], axis=0, keepdims=True) / jnp.sum(e, axis=0, keepdims=True)

    row = lax.broadcasted_iota(jnp.int32, (ts, HEAD_DIM), 0)
    q = _silu(qa_ref[0])
    f = lower + (1.0 - lower) * _sigmoid(fa_ref[0])
    k = 1.0 - f
    v = ia_ref[0]
    v16 = v.astype(BF16)

    tri = tri_ref[...]
    parts = _split2_bf16(jnp.log(f) * LOG2_E)
    g = jnp.concatenate([sum(_dot(tri, p[ch]) for p in parts) for ch in chunks], axis=0)

    level_of = lvl_ref[...]
    att = [jnp.zeros((c, c), F32) for _ in chunks]
    for lvl in range(n_levels):
        half = 1 << lvl
        if half >= F32_SUBLANES:
            shape4 = (ts // (2 * half), 2, half, HEAD_DIM)
            g4, q4, k4 = g.reshape(shape4), q.reshape(shape4), k.reshape(shape4)
            g_mid = g4[:, 0:1, half - 1:half, :]
            x_lo = k4[:, 0:1] * jnp.exp2(g_mid - g4[:, 0:1])
            x_hi = q4[:, 1:2] * jnp.exp2(g4[:, 1:2] - g_mid)
            x = jnp.concatenate([x_lo, x_hi], axis=1).reshape(ts, HEAD_DIM)
        else:
            upper = (row & half) != 0
            if lvl == 0:
                x = jnp.where(upper, q * f, k)
            else:
                g_mid = _sublane_broadcast(g, F32_SUBLANES, half - 1)
                if 2 * half < F32_SUBLANES:
                    second = _sublane_broadcast(g, F32_SUBLANES, 2 * half + half - 1)
                    g_mid = jnp.where((row & (F32_SUBLANES - 1)) < 2 * half, g_mid, second)
                x = jnp.where(upper, q, k) * jnp.exp2(-jnp.abs(g - g_mid))
        x16 = x.astype(BF16)
        for j, ch in enumerate(chunks):
            att[j] = jnp.where(level_of == lvl, _dot_nt(x16[ch], x16[ch]), att[j])
    qk_diag = jnp.sum(q * k, axis=-1, keepdims=True)

    g_last = _sublane_broadcast(g, c, c - 1)
    q_dec = (q * jnp.exp2(g)).astype(BF16)
    k_dec = (k * jnp.exp2(g_last - g)).astype(BF16)
    st = st_ref[...]
    outs = []
    for j, ch in enumerate(chunks):
        a = jnp.where(level_of == n_levels, qk_diag[ch], att[j])
        outs.append(_dot_nt(q_dec[ch], st.astype(BF16)) + _dot(a.astype(BF16), v16[ch]))
        st = st * jnp.exp2(g_last[ch][:1]) + _dot(v[ch].T.astype(BF16), k_dec[ch])
    st_ref[...] = st

    y = _rms_norm(jnp.concatenate(outs, axis=0), gn_ref[...]) * _silu(ga_ref[0])
    o_ref[0] = y.astype(o_ref.dtype)


def _hgrn(proj, lb_logits, out_gain, batch, seq, layer_row, cast_weights, cast_layer):
    ts, c, hd = HG_STEP_TOKENS, HG_CHUNK, HEAD_DIM
    assert seq % ts == 0 and ts % c == 0
    n_rows = lb_logits.shape[0]
    grid = (batch, HG_HEADS, seq // ts)
    cast_in, cast_out, cast_shapes, n_slabs, slab_bytes = _cast_rider_specs(
        cast_weights, cast_layer, grid, lambda b, h, s: (b * grid[1] + h) * grid[2] + s)
    t_idx = lax.broadcasted_iota(jnp.int32, (c, c), 0)
    s_idx = lax.broadcasted_iota(jnp.int32, (c, c), 1)
    tri = (t_idx >= s_idx).astype(BF16)
    level_of = jnp.where(t_idx > s_idx, 31 - lax.clz(t_idx ^ s_idx),
                         jnp.where(t_idx == s_idx, c.bit_length() - 1, -1)).astype(jnp.int32)

    def head_block(part):
        return pl.BlockSpec((1, ts, hd), lambda b, h, s: (b, s, part * HG_HEADS + h))

    pipelined = (4 * _nbytes((ts, hd), F32) + _nbytes((ts, hd), BF16) + _nbytes((c, c), BF16)
                 + _nbytes((c, c), jnp.int32) + slab_bytes)
    resident = 24 * _nbytes((ts, hd), F32)
    return pl.pallas_call(
        functools.partial(_hgrn_kernel, layer_row=layer_row, n_slabs=n_slabs),
        out_shape=[jax.ShapeDtypeStruct((batch, seq, HG_HEADS * hd), BF16)] + cast_shapes,
        grid=grid,
        in_specs=[
            pl.BlockSpec((n_rows, hd), lambda b, h, s: (0, h)),
            pl.BlockSpec((1, hd), lambda b, h, s: (0, 0)),
            pl.BlockSpec((c, c), lambda b, h, s: (0, 0)),
            pl.BlockSpec((c, c), lambda b, h, s: (0, 0)),
            head_block(0), head_block(1), head_block(2), head_block(3),
        ] + cast_in,
        out_specs=[pl.BlockSpec((1, ts, hd), lambda b, h, s: (b, s, h))] + cast_out,
        scratch_shapes=[pltpu.VMEM((hd, hd), F32)],
        compiler_params=pltpu.CompilerParams(
            dimension_semantics=("arbitrary", "arbitrary", "arbitrary"),
            vmem_limit_bytes=_vmem_limit(pipelined, resident)),
        name="hgrn2",
    )(lb_logits, out_gain, tri, level_of, proj, proj, proj, proj, *cast_weights)


def _sb_kernel(sfx_ref, q_ref, k_ref, vt_ref, wg_ref, wu_ref, wd_ref, o_ref, wg16_ref, wu16_ref, wd16_ref,
               acc_ref, carry_ref, sp_ref, lbc_ref, w_ref, *, n_slabs):
    blk, nq = SB_BLOCK, SB_QUERY_BLOCKS
    step = (pl.program_id(0) * pl.num_programs(1) + pl.program_id(1)) * pl.num_programs(2) + pl.program_id(2)
    _cast_rider(step, n_slabs, (wg_ref, wu_ref, wd_ref), (wg16_ref, wu16_ref, wd16_ref))

    first = nq * pl.program_id(2)
    n_depths = first + nq
    causal = (lax.broadcasted_iota(jnp.int32, (blk, blk), 0)
              < lax.broadcasted_iota(jnp.int32, (blk, blk), 1))
    sfx = sfx_ref[...]

    def key_block(h, depth, always_live):
        kb = first + h - depth
        return (kb, None) if always_live else (jnp.maximum(kb, 0), kb >= 0)

    def scores(depth, slot, on_diagonal=False):
        for h in range(nq):
            j, live = key_block(h, depth, on_diagonal)
            ks = k_ref[0, pl.ds(pl.multiple_of(j * blk, blk), blk), :]
            z = _dot_nt(ks, q_ref[0, h * blk:(h + 1) * blk, :])
            sp = jnp.maximum(z, 0.0) + jnp.log2(1.0 + jnp.exp2(-jnp.abs(z)))
            lbc_ref[slot, h] = (z - sp) - carry_ref[h]
            if on_diagonal:
                sp = jnp.where(causal, sp, 0.0)
            sp_ref[slot, h] = sp.astype(BF16)
            total = jnp.sum(sp, axis=0, keepdims=True)
            carry_ref[h] += total if live is None else jnp.where(live, total, 0.0)

    def weights(slot, on_diagonal=False):
        for h in range(nq):
            w = jnp.exp2(lbc_ref[slot, h] - _dot(sfx, sp_ref[slot, h]))
            if on_diagonal:
                w = jnp.where(causal, w, 0.0)
            w_ref[slot, h] = w.astype(BF16)

    def values(depth, slot, on_diagonal=False):
        for h in range(nq):
            j, live = key_block(h, depth, on_diagonal)
            pv = _dot(vt_ref[0, j], w_ref[slot, h])
            acc_ref[h] += pv if live is None else jnp.where(live, pv, 0.0)

    def min_open_carry(next_depth):
        h_idx = lax.broadcasted_iota(jnp.int32, carry_ref.shape, 0)
        return jnp.min(jnp.where(first + h_idx - next_depth >= 0, carry_ref[...], SB_UNDERFLOW_BITS))

    acc_ref[...] = jnp.zeros_like(acc_ref)
    carry_ref[...] = jnp.zeros_like(carry_ref)

    scores(0, 0, on_diagonal=True)
    scores(1, 1)
    weights(0, on_diagonal=True)

    def keep_going(state):
        pairs, open_carry = state
        return jnp.logical_and(pairs < (n_depths - 2) // 2, open_carry < SB_UNDERFLOW_BITS)

    def body(state):
        pairs, _ = state
        d = 2 * pairs + 2
        scores(d, 0)
        weights(1)
        values(d - 2, 0)
        scores(d + 1, 1)
        weights(0)
        values(d - 1, 1)
        return pairs + 1, min_open_carry(d + 2)

    pairs_done, _ = lax.while_loop(keep_going, body, (jnp.int32(0), min_open_carry(2)))
    scored = 2 * pairs_done + 2
    weights(1)
    values(scored - 2, 0)
    values(scored - 1, 1)

    for h in range(nq):
        o_ref[0, h * blk:(h + 1) * blk, :] = acc_ref[h].T.astype(o_ref.dtype)


def _stick_breaking(qk, vt, batch, seq, cast_weights, cast_layer):
    blk, nq, hd = SB_BLOCK, SB_QUERY_BLOCKS, HEAD_DIM
    tq = nq * blk
    assert seq % tq == 0 and nq % 2 == 0
    n_blk = seq // blk
    grid = (batch, SB_HEADS, seq // tq)
    cast_in, cast_out, cast_shapes, n_slabs, slab_bytes = _cast_rider_specs(
        cast_weights, cast_layer, grid, lambda b, h, i: (b * grid[1] + h) * grid[2] + i)
    sfx = (lax.broadcasted_iota(jnp.int32, (blk, blk), 1) > lax.broadcasted_iota(jnp.int32, (blk, blk), 0)).astype(BF16)
    pipelined = (_nbytes((blk, blk), BF16) + 2 * _nbytes((tq, hd), BF16) + 2 * _nbytes((seq, hd), BF16)
                 + slab_bytes)
    slots = 2 * nq * (2 * _nbytes((blk, blk), BF16) + _nbytes((blk, blk), F32))
    resident = nq * _nbytes((hd, blk), F32) + slots + nq * 4 * _nbytes((blk, blk), F32)
    return pl.pallas_call(
        functools.partial(_sb_kernel, n_slabs=n_slabs),
        out_shape=[jax.ShapeDtypeStruct((batch, seq, SB_HEADS * hd), BF16)] + cast_shapes,
        grid=grid,
        in_specs=[
            pl.BlockSpec((blk, blk), lambda b, h, i: (0, 0)),
            pl.BlockSpec((1, tq, hd), lambda b, h, i: (b, i, h)),
            pl.BlockSpec((1, seq, hd), lambda b, h, i: (b, 0, SB_HEADS + h)),
            pl.BlockSpec((1, n_blk, hd, blk), lambda b, h, i: (b, 0, h, 0)),
        ] + cast_in,
        out_specs=[pl.BlockSpec((1, tq, hd), lambda b, h, i: (b, i, h))] + cast_out,
        scratch_shapes=[
            pltpu.VMEM((nq, hd, blk), F32),
            pltpu.VMEM((nq, 1, blk), F32),
            pltpu.VMEM((2, nq, blk, blk), BF16),
            pltpu.VMEM((2, nq, blk, blk), F32),
            pltpu.VMEM((2, nq, blk, blk), BF16),
        ],
        compiler_params=pltpu.CompilerParams(
            dimension_semantics=("arbitrary", "arbitrary", "arbitrary"),
            vmem_limit_bytes=_vmem_limit(pipelined, resident)),
        name="stick_breaking",
    )(sfx, qk, qk, vt, *cast_weights)


def _out_proj_kernel(oa_ref, ob_ref, wa_ref, wb_ref, x_ref, o_ref):
    o_ref[...] = x_ref[...] + _dot(oa_ref[...], wa_ref[0]) + _dot(ob_ref[...], wb_ref[0])


def _out_proj(o_a, o_b, w_out, layer, x2):
    n_tok, d = x2.shape
    ka, kb = o_a.shape[1], o_b.shape[1]
    tm = OUT_PROJ_ROWS
    assert n_tok % tm == 0 and ka == kb and w_out.shape[1] == ka + kb
    pipelined = (_nbytes((tm, ka), BF16) + _nbytes((tm, kb), BF16) + _nbytes((ka, d), BF16)
                 + _nbytes((kb, d), BF16) + 2 * _nbytes((tm, d), F32))
    resident = 2 * _nbytes((tm, d), F32)
    return pl.pallas_call(
        _out_proj_kernel,
        out_shape=jax.ShapeDtypeStruct((n_tok, d), F32),
        grid=(n_tok // tm,),
        in_specs=[
            pl.BlockSpec((tm, ka), lambda i: (i, 0)),
            pl.BlockSpec((tm, kb), lambda i: (i, 0)),
            pl.BlockSpec((1, ka, d), lambda i: (layer, 0, 0)),
            pl.BlockSpec((1, kb, d), lambda i: (layer, 1, 0)),
            pl.BlockSpec((tm, d), lambda i: (i, 0)),
        ],
        out_specs=pl.BlockSpec((tm, d), lambda i: (i, 0)),
        compiler_params=pltpu.CompilerParams(
            dimension_semantics=("arbitrary",),
            vmem_limit_bytes=_vmem_limit(pipelined, resident)),
        name="out_proj",
    )(o_a, o_b, w_out, w_out, x2)


def _ffn_kernel(x_ref, g_ref, wg_a, wu_a, wd_a, wg_b, wu_b, wd_b, fg_ref, o_ref, h_ref, *, final_norm, odd_tiles):
    j = pl.program_id(1)
    last = pl.num_programs(1) - 1

    @pl.when(j == 0)
    def _():
        x = x_ref[...]
        h_ref[...] = _rms_norm(x, g_ref[...]).astype(BF16)
        o_ref[...] = x

    def add_tiles(tiles):
        h = h_ref[...]
        down = None
        for wg_ref, wu_ref, wd_ref in tiles:
            act = _silu(_dot(h, wg_ref[0])) * _dot(h, wu_ref[0])
            part = _dot(act.astype(BF16), wd_ref[0])
            down = part if down is None else down + part
        o_ref[...] += down

    tile_a, tile_b = (wg_a, wu_a, wd_a), (wg_b, wu_b, wd_b)
    if odd_tiles:
        pl.when(j < last)(lambda: add_tiles([tile_a, tile_b]))
        pl.when(j == last)(lambda: add_tiles([tile_a]))
    else:
        add_tiles([tile_a, tile_b])

    if final_norm:
        @pl.when(j == last)
        def _():
            o_ref[...] = _rms_norm(o_ref[...], fg_ref[...])


def _ffn(x2, gain, w_gate, w_up, w_down, layer, final_gain, final_norm, name):
    n_tok, d = x2.shape
    hidden = w_gate.shape[2]
    tm, th = FFN_ROWS, FFN_HIDDEN_TILE
    assert n_tok % tm == 0 and hidden % th == 0
    n_tiles = hidden // th
    n_steps = -(-n_tiles // 2)
    tile_a = lambda j: 2 * j
    tile_b = lambda j: jnp.minimum(2 * j + 1, n_tiles - 1)
    pipelined = (2 * _nbytes((tm, d), F32) + 2 * (2 * _nbytes((d, th), BF16) + _nbytes((th, d), BF16))
                 + 2 * _nbytes((1, d), F32))
    resident = _nbytes((tm, d), BF16) + 4 * _nbytes((tm, th), F32) + _nbytes((tm, d), F32)

    def weight_specs(tile_of):
        return [pl.BlockSpec((1, d, th), lambda i, j: (layer, 0, tile_of(j))),
                pl.BlockSpec((1, d, th), lambda i, j: (layer, 0, tile_of(j))),
                pl.BlockSpec((1, th, d), lambda i, j: (layer, tile_of(j), 0))]

    return pl.pallas_call(
        functools.partial(_ffn_kernel, final_norm=final_norm, odd_tiles=n_tiles % 2 == 1),
        out_shape=jax.ShapeDtypeStruct((n_tok, d), F32),
        grid=(n_tok // tm, n_steps),
        in_specs=[
            pl.BlockSpec((tm, d), lambda i, j: (i, 0)),
            pl.BlockSpec((1, d), lambda i, j: (0, 0)),
            *weight_specs(tile_a), *weight_specs(tile_b),
            pl.BlockSpec((1, d), lambda i, j: (0, 0)),
        ],
        out_specs=pl.BlockSpec((tm, d), lambda i, j: (i, 0)),
        scratch_shapes=[pltpu.VMEM((tm, d), BF16)],
        compiler_params=pltpu.CompilerParams(
            dimension_semantics=("arbitrary", "arbitrary"),
            vmem_limit_bytes=_vmem_limit(pipelined, resident)),
        name=name,
    )(x2, gain, w_gate, w_up, w_down, w_gate, w_up, w_down, final_gain)


def _pool_kernel(x_ref, halo_ref, g_ref, pw_ref, sc_ref, o_ref, hs_ref):
    ts = x_ref.shape[1]
    halo = POOL_HALO
    i = pl.program_id(1)
    gain = g_ref[...]
    x = x_ref[0]
    hs_ref[0:halo, :] = jnp.where(i == 0, 0.0, _rms_norm(halo_ref[0], gain))
    hs_ref[halo:halo + ts, :] = _rms_norm(x, gain)

    pos = i * ts + lax.broadcasted_iota(jnp.int32, (ts, 1), 0)
    group = pw_ref.shape[1]
    for gi, win in enumerate(POOL_WINDOWS):
        cols = slice(gi * group, (gi + 1) * group)
        wsum = hs_ref[:, cols]
        for s in range(win.bit_length() - 1):
            wsum = wsum + pltpu.roll(wsum, 1 << s, axis=0)
        h = hs_ref[halo:halo + ts, cols]
        inv_count = 1.0 / jnp.minimum(pos + 1, win).astype(F32)
        pooled = wsum[halo:] * inv_count - h
        mixed = _dot(pooled.astype(BF16), pw_ref[gi])
        o_ref[0, :, cols] = x[:, cols] + mixed * sc_ref[:, cols]


def _pool_mixer(x3, gain, pool_w, scale):
    batch, seq, d = x3.shape
    ts, halo = POOL_ROWS, POOL_HALO
    n_groups, group = pool_w.shape[0], pool_w.shape[1]
    assert seq % ts == 0 and ts % halo == 0 and max(POOL_WINDOWS) - 1 <= halo
    assert all(w & (w - 1) == 0 for w in POOL_WINDOWS)
    assert n_groups == len(POOL_WINDOWS) and n_groups * group == d
    halo_per_tile = ts // halo
    pipelined = (2 * _nbytes((ts, d), F32) + _nbytes((halo, d), F32) + _nbytes(pool_w.shape, BF16)
                 + 2 * _nbytes((1, d), F32))
    resident = _nbytes((ts + halo, d), F32) + 3 * _nbytes((ts, d), F32)
    return pl.pallas_call(
        _pool_kernel,
        out_shape=jax.ShapeDtypeStruct((batch, seq, d), F32),
        grid=(batch, seq // ts),
        in_specs=[
            pl.BlockSpec((1, ts, d), lambda b, i: (b, i, 0)),
            pl.BlockSpec((1, halo, d), lambda b, i: (b, jnp.maximum(i * halo_per_tile - 1, 0), 0)),
            pl.BlockSpec((1, d), lambda b, i: (0, 0)),
            pl.BlockSpec((n_groups, group, group), lambda b, i: (0, 0, 0)),
            pl.BlockSpec((1, d), lambda b, i: (0, 0)),
        ],
        out_specs=pl.BlockSpec((1, ts, d), lambda b, i: (b, i, 0)),
        scratch_shapes=[pltpu.VMEM((ts + halo, d), F32)],
        compiler_params=pltpu.CompilerParams(
            dimension_semantics=("arbitrary", "arbitrary"),
            vmem_limit_bytes=_vmem_limit(pipelined, resident)),
        name="pool_mixer",
    )(x3, x3, gain, pool_w, scale)


def kernel(x, mix_norm, ffn_norm, final_norm, ab_w_in, lb_logits, hg_out_norm, ab_w_out,
           pool_w, pool_scale, ffn_w_gate, ffn_w_up, ffn_w_down):
    batch, seq, d = x.shape
    n_tok = batch * seq
    hg_width = HG_HEADS * HEAD_DIM
    sb_width = SB_HEADS * HEAD_DIM
    row = lambda a: a.reshape(1, -1)

    x2 = x.reshape(n_tok, d)
    ffn_weights = (ffn_w_gate, ffn_w_up, ffn_w_down)
    proj_a, qk, vt = _in_proj(x2, row(mix_norm[0]), ab_w_in.astype(BF16), 0, batch, seq, SB_BLOCK)
    o_a, *ffn1_weights = _hgrn(proj_a.reshape(batch, seq, 4 * hg_width), lb_logits, row(hg_out_norm[0]),
                               batch, seq, 0, ffn_weights, 1)
    o_b, *ffn0_weights = _stick_breaking(qk.reshape(batch, seq, 2 * sb_width), vt, batch, seq, ffn_weights, 0)
    x2 = _out_proj(o_a.reshape(n_tok, hg_width), o_b.reshape(n_tok, sb_width), ab_w_out.astype(BF16), 0, x2)
    x2 = _ffn(x2, row(ffn_norm[0]), *ffn0_weights, 0, row(final_norm), False, "ffn0")

    x3 = _pool_mixer(x2.reshape(batch, seq, d), row(mix_norm[1]), pool_w[0].astype(BF16), row(pool_scale[0]))
    x2 = _ffn(x3.reshape(n_tok, d), row(ffn_norm[1]), *ffn1_weights, 0, row(final_norm), True, "ffn1_final_norm")
    return x2.reshape(batch, seq, d)
```

```python
import functools
import math

import jax
import jax.numpy as jnp
from jax import lax
from jax.experimental import pallas as pl
from jax.experimental.pallas import tpu as pltpu

F32 = jnp.float32
BF16 = jnp.bfloat16

RMS_EPS = 1e-6
HEAD_DIM = 128
HG_HEADS = 8
SB_HEADS = 8
POOL_WINDOWS = (2, 4, 8, 16)

V7X_VMEM_BYTES = 64 * 1024 * 1024
V7X_VMEM_USABLE_BYTES = V7X_VMEM_BYTES * 13 // 16
V7X_DEFAULT_SCOPED_VMEM_BYTES = 32 * 1024 * 1024
F32_SUBLANES = 8
LOG2_E = math.log2(math.e)

HG_CHUNK = 128
HG_STEP_TOKENS = 512
SB_BLOCK = 256
SB_QUERY_BLOCKS = 4
SB_UNDERFLOW_BITS = 160.0
PROJ_ROWS = 1024
PROJ_COLS = 1024
OUT_PROJ_ROWS = 512
FFN_ROWS = 512
FFN_HIDDEN_TILE = 512
POOL_ROWS = 512
CAST_SLAB = 128
POOL_HALO = 16


def _vmem_limit(pipelined_bytes, resident_bytes):
    need = 2 * pipelined_bytes + resident_bytes
    assert need <= V7X_VMEM_USABLE_BYTES, (need, V7X_VMEM_USABLE_BYTES)
    return int(max(need, V7X_DEFAULT_SCOPED_VMEM_BYTES))


def _nbytes(shape, dtype):
    return math.prod(shape) * jnp.dtype(dtype).itemsize


def _rms_norm(x, gain):
    ms = jnp.mean(x * x, axis=-1, keepdims=True)
    return x * lax.rsqrt(ms + RMS_EPS) * gain


def _dot(a, b):
    return jnp.dot(a, b, preferred_element_type=F32)


def _dot_nt(a, b):
    return lax.dot_general(a, b, (((1,), (1,)), ((), ())), preferred_element_type=F32)


def _sigmoid(x):
    return 0.5 * jnp.tanh(0.5 * x) + 0.5


def _silu(x):
    return x * _sigmoid(x)


def _in_proj_kernel(x_ref, g_ref, w_ref, oa_ref, oqk_ref, ovt_ref, h_ref, *, n_a_tiles, q_scale, key_block):
    j = pl.program_id(1)

    @pl.when(j == 0)
    def _():
        h_ref[...] = _rms_norm(x_ref[...], g_ref[...]).astype(BF16)

    def tile():
        return _dot(h_ref[...], w_ref[0])

    @pl.when(j < n_a_tiles)
    def _():
        oa_ref[...] = tile()

    @pl.when(j == n_a_tiles)
    def _():
        oqk_ref[...] = (tile() * q_scale).astype(oqk_ref.dtype)

    @pl.when(j == n_a_tiles + 1)
    def _():
        oqk_ref[...] = tile().astype(oqk_ref.dtype)

    @pl.when(j == n_a_tiles + 2)
    def _():
        vt = tile().T.astype(ovt_ref.dtype)
        for kb in range(ovt_ref.shape[1]):
            ovt_ref[0, kb] = vt[:, kb * key_block:(kb + 1) * key_block]


def _in_proj(x2, gain, w_in, layer, batch, seq, key_block):
    n_tok, d = x2.shape
    tm, tn = PROJ_ROWS, PROJ_COLS
    hg_width, sb_width = HG_HEADS * HEAD_DIM, SB_HEADS * HEAD_DIM
    n_a_tiles, n_qk_tiles, n_v_tiles = 4 * hg_width // tn, 2 * sb_width // tn, sb_width // tn
    assert tn == hg_width == sb_width and w_in.shape[2] == (n_a_tiles + n_qk_tiles + n_v_tiles) * tn
    assert seq % tm == 0 and tm % key_block == 0
    tiles_per_seq = seq // tm
    kb_per_tile = tm // key_block
    pipelined = (_nbytes((tm, d), F32) + _nbytes((d, tn), BF16) + _nbytes((tm, tn), F32)
                 + 2 * _nbytes((tm, tn), BF16) + _nbytes((1, d), F32))
    resident = _nbytes((tm, d), BF16) + _nbytes((tm, tn), F32)
    return pl.pallas_call(
        functools.partial(_in_proj_kernel, n_a_tiles=n_a_tiles, q_scale=LOG2_E / math.sqrt(HEAD_DIM),
                          key_block=key_block),
        out_shape=(jax.ShapeDtypeStruct((n_tok, n_a_tiles * tn), F32),
                   jax.ShapeDtypeStruct((n_tok, 2 * tn), BF16),
                   jax.ShapeDtypeStruct((batch, seq // key_block, tn, key_block), BF16)),
        grid=(n_tok // tm, n_a_tiles + n_qk_tiles + n_v_tiles),
        in_specs=[
            pl.BlockSpec((tm, d), lambda i, j: (i, 0)),
            pl.BlockSpec((1, d), lambda i, j: (0, 0)),
            pl.BlockSpec((1, d, tn), lambda i, j: (layer, 0, j)),
        ],
        out_specs=(
            pl.BlockSpec((tm, tn), lambda i, j: (i, jnp.minimum(j, n_a_tiles - 1))),
            pl.BlockSpec((tm, tn), lambda i, j: (i, jnp.clip(j - n_a_tiles, 0, n_qk_tiles - 1))),
            pl.BlockSpec((1, kb_per_tile, tn, key_block),
                         lambda i, j: (i // tiles_per_seq, i % tiles_per_seq, 0, 0)),
        ),
        scratch_shapes=[pltpu.VMEM((tm, d), BF16)],
        compiler_params=pltpu.CompilerParams(
            dimension_semantics=("arbitrary", "arbitrary"),
            vmem_limit_bytes=_vmem_limit(pipelined, resident)),
        name="in_proj",
    )(x2, gain, w_in)


def _cast_rider_specs(weights, layer, grid, step_of):
    w_gate, w_up, w_down = weights
    _, d, hidden = w_gate.shape
    assert w_up.shape == w_gate.shape and w_down.shape[1:] == (hidden, d) and hidden % CAST_SLAB == 0
    n_slabs = max(n for n in range(1, hidden // CAST_SLAB + 1)
                  if (hidden // CAST_SLAB) % n == 0 and n <= math.prod(grid))
    width = hidden // n_slabs

    def slab(*grid_idx):
        return jnp.minimum(step_of(*grid_idx), n_slabs - 1)

    col_in = pl.BlockSpec((1, d, width), lambda *g: (layer, 0, slab(*g)))
    col_out = pl.BlockSpec((1, d, width), lambda *g: (0, 0, slab(*g)))
    row_in = pl.BlockSpec((1, width, d), lambda *g: (layer, slab(*g), 0))
    row_out = pl.BlockSpec((1, width, d), lambda *g: (0, slab(*g), 0))
    out_shapes = [jax.ShapeDtypeStruct((1, d, hidden), BF16), jax.ShapeDtypeStruct((1, d, hidden), BF16),
                  jax.ShapeDtypeStruct((1, hidden, d), BF16)]
    slab_bytes = 3 * (_nbytes((d, width), F32) + _nbytes((d, width), BF16))
    return [col_in, col_in, row_in], [col_out, col_out, row_out], out_shapes, n_slabs, slab_bytes


def _cast_rider(step, n_slabs, src_refs, dst_refs):
    @pl.when(step < n_slabs)
    def _():
        for src, dst in zip(src_refs, dst_refs):
            dst[...] = src[...].astype(dst.dtype)


def _split2_bf16(x):
    hi = x.astype(BF16)
    return hi, (x - hi.astype(F32)).astype(BF16)


def _sublane_broadcast(x, group, row):
    rows, lanes = x.shape
    x3 = x.reshape(rows // group, group, lanes)
    return jnp.broadcast_to(x3[:, row:row + 1, :], x3.shape).reshape(rows, lanes)


def _hgrn_kernel(lbl_ref, gn_ref, tri_ref, lvl_ref, qa_ref, fa_ref, ia_ref, ga_ref, wg_ref, wu_ref, wd_ref,
                 o_ref, wg16_ref, wu16_ref, wd16_ref, st_ref, *, layer_row, n_slabs):
    c, ts = HG_CHUNK, HG_STEP_TOKENS
    n_levels = c.bit_length() - 1
    chunks = [slice(j * c, (j + 1) * c) for j in range(ts // c)]

    step = (pl.program_id(0) * pl.num_programs(1) + pl.program_id(1)) * pl.num_programs(2) + pl.program_id(2)
    _cast_rider(step, n_slabs, (wg_ref, wu_ref, wd_ref), (wg16_ref, wu16_ref, wd16_ref))

    @pl.when(pl.program_id(2) == 0)
    def _():
        st_ref[...] = jnp.zeros_like(st_ref)

    logits = lbl_ref[...]
    e = jnp.exp(logits - jnp.max(logits, axis=0, keepdims=True))
    lower = jnp.sum(e[:layer_row + 1], axis=0, keepdims=True) / jnp.sum(e, axis=0, keepdims=True)

    row = lax.broadcasted_iota(jnp.int32, (ts, HEAD_DIM), 0)
    q = _silu(qa_ref[0])
    f = lower + (1.0 - lower) * _sigmoid(fa_ref[0])
    k = 1.0 - f
    v = ia_ref[0]
    v16 = v.astype(BF16)

    tri = tri_ref[...]
    parts = _split2_bf16(jnp.log(f) * LOG2_E)
    g = jnp.concatenate([sum(_dot(tri, p[ch]) for p in parts) for ch in chunks], axis=0)

    level_of = lvl_ref[...]
    att = [jnp.zeros((c, c), F32) for _ in chunks]
    for lvl in range(n_levels):
        half = 1 << lvl
        if half >= F32_SUBLANES:
            shape4 = (ts // (2 * half), 2, half, HEAD_DIM)
            g4, q4, k4 = g.reshape(shape4), q.reshape(shape4), k.reshape(shape4)
            g_mid = g4[:, 0:1, half - 1:half, :]
            x_lo = k4[:, 0:1] * jnp.exp2(g_mid - g4[:, 0:1])
            x_hi = q4[:, 1:2] * jnp.exp2(g4[:, 1:2] - g_mid)
            x = jnp.concatenate([x_lo, x_hi], axis=1).reshape(ts, HEAD_DIM)
        else:
            upper = (row & half) != 0
            if lvl == 0:
                x = jnp.where(upper, q * f, k)
            else:
                g_mid = _sublane_broadcast(g, F32_SUBLANES, half - 1)
                if 2 * half < F32_SUBLANES:
                    second = _sublane_broadcast(g, F32_SUBLANES, 2 * half + half - 1)
                    g_mid = jnp.where((row & (F32_SUBLANES - 1)) < 2 * half, g_mid, second)
                x = jnp.where(upper, q, k) * jnp.exp2(-jnp.abs(g - g_mid))
        x16 = x.astype(BF16)
        for j, ch in enumerate(chunks):
            att[j] = jnp.where(level_of == lvl, _dot_nt(x16[ch], x16[ch]), att[j])
    qk_diag = jnp.sum(q * k, axis=-1, keepdims=True)

    g_last = _sublane_broadcast(g, c, c - 1)
    q_dec = (q * jnp.exp2(g)).astype(BF16)
    k_dec = (k * jnp.exp2(g_last - g)).astype(BF16)
    st = st_ref[...]
    outs = []
    for j, ch in enumerate(chunks):
        a = jnp.where(level_of == n_levels, qk_diag[ch], att[j])
        outs.append(_dot_nt(q_dec[ch], st.astype(BF16)) + _dot(a.astype(BF16), v16[ch]))
        st = st * jnp.exp2(g_last[ch][:1]) + _dot(v[ch].T.astype(BF16), k_dec[ch])
    st_ref[...] = st

    y = _rms_norm(jnp.concatenate(outs, axis=0), gn_ref[...]) * _silu(ga_ref[0])
    o_ref[0] = y.astype(o_ref.dtype)


def _hgrn(proj, lb_logits, out_gain, batch, seq, layer_row, cast_weights, cast_layer):
    ts, c, hd = HG_STEP_TOKENS, HG_CHUNK, HEAD_DIM
    assert seq % ts == 0 and ts % c == 0
    n_rows = lb_logits.shape[0]
    grid = (batch, HG_HEADS, seq // ts)
    cast_in, cast_out, cast_shapes, n_slabs, slab_bytes = _cast_rider_specs(
        cast_weights, cast_layer, grid, lambda b, h, s: (b * grid[1] + h) * grid[2] + s)
    t_idx = lax.broadcasted_iota(jnp.int32, (c, c), 0)
    s_idx = lax.broadcasted_iota(jnp.int32, (c, c), 1)
    tri = (t_idx >= s_idx).astype(BF16)
    level_of = jnp.where(t_idx > s_idx, 31 - lax.clz(t_idx ^ s_idx),
                         jnp.where(t_idx == s_idx, c.bit_length() - 1, -1)).astype(jnp.int32)

    def head_block(part):
        return pl.BlockSpec((1, ts, hd), lambda b, h, s: (b, s, part * HG_HEADS + h))

    pipelined = (4 * _nbytes((ts, hd), F32) + _nbytes((ts, hd), BF16) + _nbytes((c, c), BF16)
                 + _nbytes((c, c), jnp.int32) + slab_bytes)
    resident = 24 * _nbytes((ts, hd), F32)
    return pl.pallas_call(
        functools.partial(_hgrn_kernel, layer_row=layer_row, n_slabs=n_slabs),
        out_shape=[jax.ShapeDtypeStruct((batch, seq, HG_HEADS * hd), BF16)] + cast_shapes,
        grid=grid,
        in_specs=[
            pl.BlockSpec((n_rows, hd), lambda b, h, s: (0, h)),
            pl.BlockSpec((1, hd), lambda b, h, s: (0, 0)),
            pl.BlockSpec((c, c), lambda b, h, s: (0, 0)),
            pl.BlockSpec((c, c), lambda b, h, s: (0, 0)),
            head_block(0), head_block(1), head_block(2), head_block(3),
        ] + cast_in,
        out_specs=[pl.BlockSpec((1, ts, hd), lambda b, h, s: (b, s, h))] + cast_out,
        scratch_shapes=[pltpu.VMEM((hd, hd), F32)],
        compiler_params=pltpu.CompilerParams(
            dimension_semantics=("arbitrary", "arbitrary", "arbitrary"),
            vmem_limit_bytes=_vmem_limit(pipelined, resident)),
        name="hgrn2",
    )(lb_logits, out_gain, tri, level_of, proj, proj, proj, proj, *cast_weights)


def _sb_kernel(sfx_ref, q_ref, k_ref, vt_ref, wg_ref, wu_ref, wd_ref, o_ref, wg16_ref, wu16_ref, wd16_ref,
               acc_ref, carry_ref, sp_ref, lbc_ref, w_ref, *, n_slabs):
    blk, nq = SB_BLOCK, SB_QUERY_BLOCKS
    step = (pl.program_id(0) * pl.num_programs(1) + pl.program_id(1)) * pl.num_programs(2) + pl.program_id(2)
    _cast_rider(step, n_slabs, (wg_ref, wu_ref, wd_ref), (wg16_ref, wu16_ref, wd16_ref))

    first = nq * pl.program_id(2)
    n_depths = first + nq
    causal = (lax.broadcasted_iota(jnp.int32, (blk, blk), 0)
              < lax.broadcasted_iota(jnp.int32, (blk, blk), 1))
    sfx = sfx_ref[...]

    def key_block(h, depth, always_live):
        kb = first + h - depth
        return (kb, None) if always_live else (jnp.maximum(kb, 0), kb >= 0)

    def scores(depth, slot, on_diagonal=False):
        for h in range(nq):
            j, live = key_block(h, depth, on_diagonal)
            ks = k_ref[0, pl.ds(pl.multiple_of(j * blk, blk), blk), :]
            z = _dot_nt(ks, q_ref[0, h * blk:(h + 1) * blk, :])
            sp = jnp.maximum(z, 0.0) + jnp.log2(1.0 + jnp.exp2(-jnp.abs(z)))
            lbc_ref[slot, h] = (z - sp) - carry_ref[h]
            if on_diagonal:
                sp = jnp.where(causal, sp, 0.0)
            sp_ref[slot, h] = sp.astype(BF16)
            total = jnp.sum(sp, axis=0, keepdims=True)
            carry_ref[h] += total if live is None else jnp.where(live, total, 0.0)

    def weights(slot, on_diagonal=False):
        for h in range(nq):
            w = jnp.exp2(lbc_ref[slot, h] - _dot(sfx, sp_ref[slot, h]))
            if on_diagonal:
                w = jnp.where(causal, w, 0.0)
            w_ref[slot, h] = w.astype(BF16)

    def values(depth, slot, on_diagonal=False):
        for h in range(nq):
            j, live = key_block(h, depth, on_diagonal)
            pv = _dot(vt_ref[0, j], w_ref[slot, h])
            acc_ref[h] += pv if live is None else jnp.where(live, pv, 0.0)

    def min_open_carry(next_depth):
        h_idx = lax.broadcasted_iota(jnp.int32, carry_ref.shape, 0)
        return jnp.min(jnp.where(first + h_idx - next_depth >= 0, carry_ref[...], SB_UNDERFLOW_BITS))

    acc_ref[...] = jnp.zeros_like(acc_ref)
    carry_ref[...] = jnp.zeros_like(carry_ref)

    scores(0, 0, on_diagonal=True)
    scores(1, 1)
    weights(0, on_diagonal=True)

    def keep_going(state):
        pairs, open_carry = state
        return jnp.logical_and(pairs < (n_depths - 2) // 2, open_carry < SB_UNDERFLOW_BITS)

    def body(state):
        pairs, _ = state
        d = 2 * pairs + 2
        scores(d, 0)
        weights(1)
        values(d - 2, 0)
        scores(d + 1, 1)
        weights(0)
        values(d - 1, 1)
        return pairs + 1, min_open_carry(d + 2)

    pairs_done, _ = lax.while_loop(keep_going, body, (jnp.int32(0), min_open_carry(2)))
    scored = 2 * pairs_done + 2
    weights(1)
    values(scored - 2, 0)
    values(scored - 1, 1)

    for h in range(nq):
        o_ref[0, h * blk:(h + 1) * blk, :] = acc_ref[h].T.astype(o_ref.dtype)


def _stick_breaking(qk, vt, batch, seq, cast_weights, cast_layer):
    blk, nq, hd = SB_BLOCK, SB_QUERY_BLOCKS, HEAD_DIM
    tq = nq * blk
    assert seq % tq == 0 and nq % 2 == 0
    n_blk = seq // blk
    grid = (batch, SB_HEADS, seq // tq)
    cast_in, cast_out, cast_shapes, n_slabs, slab_bytes = _cast_rider_specs(
        cast_weights, cast_layer, grid, lambda b, h, i: (b * grid[1] + h) * grid[2] + i)
    sfx = (lax.broadcasted_iota(jnp.int32, (blk, blk), 1) > lax.broadcasted_iota(jnp.int32, (blk, blk), 0)).astype(BF16)
    pipelined = (_nbytes((blk, blk), BF16) + 2 * _nbytes((tq, hd), BF16) + 2 * _nbytes((seq, hd), BF16)
                 + slab_bytes)
    slots = 2 * nq * (2 * _nbytes((blk, blk), BF16) + _nbytes((blk, blk), F32))
    resident = nq * _nbytes((hd, blk), F32) + slots + nq * 4 * _nbytes((blk, blk), F32)
    return pl.pallas_call(
        functools.partial(_sb_kernel, n_slabs=n_slabs),
        out_shape=[jax.ShapeDtypeStruct((batch, seq, SB_HEADS * hd), BF16)] + cast_shapes,
        grid=grid,
        in_specs=[
            pl.BlockSpec((blk, blk), lambda b, h, i: (0, 0)),
            pl.BlockSpec((1, tq, hd), lambda b, h, i: (b, i, h)),
            pl.BlockSpec((1, seq, hd), lambda b, h, i: (b, 0, SB_HEADS + h)),
            pl.BlockSpec((1, n_blk, hd, blk), lambda b, h, i: (b, 0, h, 0)),
        ] + cast_in,
        out_specs=[pl.BlockSpec((1, tq, hd), lambda b, h, i: (b, i, h))] + cast_out,
        scratch_shapes=[
            pltpu.VMEM((nq, hd, blk), F32),
            pltpu.VMEM((nq, 1, blk), F32),
            pltpu.VMEM((2, nq, blk, blk), BF16),
            pltpu.VMEM((2, nq, blk, blk), F32),
            pltpu.VMEM((2, nq, blk, blk), BF16),
        ],
        compiler_params=pltpu.CompilerParams(
            dimension_semantics=("arbitrary", "arbitrary", "arbitrary"),
            vmem_limit_bytes=_vmem_limit(pipelined, resident)),
        name="stick_breaking",
    )(sfx, qk, qk, vt, *cast_weights)


def _out_proj_kernel(oa_ref, ob_ref, wa_ref, wb_ref, x_ref, o_ref):
    o_ref[...] = x_ref[...] + _dot(oa_ref[...], wa_ref[0]) + _dot(ob_ref[...], wb_ref[0])


def _out_proj(o_a, o_b, w_out, layer, x2):
    n_tok, d = x2.shape
    ka, kb = o_a.shape[1], o_b.shape[1]
    tm = OUT_PROJ_ROWS
    assert n_tok % tm == 0 and ka == kb and w_out.shape[1] == ka + kb
    pipelined = (_nbytes((tm, ka), BF16) + _nbytes((tm, kb), BF16) + _nbytes((ka, d), BF16)
                 + _nbytes((kb, d), BF16) + 2 * _nbytes((tm, d), F32))
    resident = 2 * _nbytes((tm, d), F32)
    return pl.pallas_call(
        _out_proj_kernel,
        out_shape=jax.ShapeDtypeStruct((n_tok, d), F32),
        grid=(n_tok // tm,),
        in_specs=[
            pl.BlockSpec((tm, ka), lambda i: (i, 0)),
            pl.BlockSpec((tm, kb), lambda i: (i, 0)),
            pl.BlockSpec((1, ka, d), lambda i: (layer, 0, 0)),
            pl.BlockSpec((1, kb, d), lambda i: (layer, 1, 0)),
            pl.BlockSpec((tm, d), lambda i: (i, 0)),
        ],
        out_specs=pl.BlockSpec((tm, d), lambda i: (i, 0)),
        compiler_params=pltpu.CompilerParams(
            dimension_semantics=("arbitrary",),
            vmem_limit_bytes=_vmem_limit(pipelined, resident)),
        name="out_proj",
    )(o_a, o_b, w_out, w_out, x2)


def _ffn_kernel(x_ref, g_ref, wg_ref, wu_ref, wd_ref, fg_ref, o_ref, h_ref, *, final_norm):
    j = pl.program_id(1)

    @pl.when(j == 0)
    def _():
        x = x_ref[...]
        h_ref[...] = _rms_norm(x, g_ref[...]).astype(BF16)
        o_ref[...] = x

    h = h_ref[...]
    act = _silu(_dot(h, wg_ref[0])) * _dot(h, wu_ref[0])
    o_ref[...] += _dot(act.astype(BF16), wd_ref[0])

    if final_norm:
        @pl.when(j == pl.num_programs(1) - 1)
        def _():
            o_ref[...] = _rms_norm(o_ref[...], fg_ref[...])


def _ffn(x2, gain, w_gate, w_up, w_down, layer, final_gain, final_norm, name):
    n_tok, d = x2.shape
    hidden = w_gate.shape[2]
    tm, th = FFN_ROWS, FFN_HIDDEN_TILE
    assert n_tok % tm == 0 and hidden % th == 0
    pipelined = (2 * _nbytes((tm, d), F32) + 2 * _nbytes((d, th), BF16) + _nbytes((th, d), BF16)
                 + 2 * _nbytes((1, d), F32))
    resident = _nbytes((tm, d), BF16) + 4 * _nbytes((tm, th), F32) + _nbytes((tm, d), F32)
    return pl.pallas_call(
        functools.partial(_ffn_kernel, final_norm=final_norm),
        out_shape=jax.ShapeDtypeStruct((n_tok, d), F32),
        grid=(n_tok // tm, hidden // th),
        in_specs=[
            pl.BlockSpec((tm, d), lambda i, j: (i, 0)),
            pl.BlockSpec((1, d), lambda i, j: (0, 0)),
            pl.BlockSpec((1, d, th), lambda i, j: (layer, 0, j)),
            pl.BlockSpec((1, d, th), lambda i, j: (layer, 0, j)),
            pl.BlockSpec((1, th, d), lambda i, j: (layer, j, 0)),
            pl.BlockSpec((1, d), lambda i, j: (0, 0)),
        ],
        out_specs=pl.BlockSpec((tm, d), lambda i, j: (i, 0)),
        scratch_shapes=[pltpu.VMEM((tm, d), BF16)],
        compiler_params=pltpu.CompilerParams(
            dimension_semantics=("arbitrary", "arbitrary"),
            vmem_limit_bytes=_vmem_limit(pipelined, resident)),
        name=name,
    )(x2, gain, w_gate, w_up, w_down, final_gain)


def _pool_kernel(x_ref, halo_ref, g_ref, pw_ref, sc_ref, o_ref, hs_ref):
    ts = x_ref.shape[1]
    halo = POOL_HALO
    i = pl.program_id(1)
    gain = g_ref[...]
    x = x_ref[0]
    hs_ref[0:halo, :] = jnp.where(i == 0, 0.0, _rms_norm(halo_ref[0], gain))
    hs_ref[halo:halo + ts, :] = _rms_norm(x, gain)

    pos = i * ts + lax.broadcasted_iota(jnp.int32, (ts, 1), 0)
    group = pw_ref.shape[1]
    for gi, win in enumerate(POOL_WINDOWS):
        cols = slice(gi * group, (gi + 1) * group)
        wsum = hs_ref[:, cols]
        for s in range(win.bit_length() - 1):
            wsum = wsum + pltpu.roll(wsum, 1 << s, axis=0)
        h = hs_ref[halo:halo + ts, cols]
        inv_count = 1.0 / jnp.minimum(pos + 1, win).astype(F32)
        pooled = wsum[halo:] * inv_count - h
        mixed = _dot(pooled.astype(BF16), pw_ref[gi])
        o_ref[0, :, cols] = x[:, cols] + mixed * sc_ref[:, cols]


def _pool_mixer(x3, gain, pool_w, scale):
    batch, seq, d = x3.shape
    ts, halo = POOL_ROWS, POOL_HALO
    n_groups, group = pool_w.shape[0], pool_w.shape[1]
    assert seq % ts == 0 and ts % halo == 0 and max(POOL_WINDOWS) - 1 <= halo
    assert all(w & (w - 1) == 0 for w in POOL_WINDOWS)
    assert n_groups == len(POOL_WINDOWS) and n_groups * group == d
    halo_per_tile = ts // halo
    pipelined = (2 * _nbytes((ts, d), F32) + _nbytes((halo, d), F32) + _nbytes(pool_w.shape, BF16)
                 + 2 * _nbytes((1, d), F32))
    resident = _nbytes((ts + halo, d), F32) + 3 * _nbytes((ts, d), F32)
    return pl.pallas_call(
        _pool_kernel,
        out_shape=jax.ShapeDtypeStruct((batch, seq, d), F32),
        grid=(batch, seq // ts),
        in_specs=[
            pl.BlockSpec((1, ts, d), lambda b, i: (b, i, 0)),
            pl.BlockSpec((1, halo, d), lambda b, i: (b, jnp.maximum(i * halo_per_tile - 1, 0), 0)),
            pl.BlockSpec((1, d), lambda b, i: (0, 0)),
            pl.BlockSpec((n_groups, group, group), lambda b, i: (0, 0, 0)),
            pl.BlockSpec((1, d), lambda b, i: (0, 0)),
        ],
        out_specs=pl.BlockSpec((1, ts, d), lambda b, i: (b, i, 0)),
        scratch_shapes=[pltpu.VMEM((ts + halo, d), F32)],
        compiler_params=pltpu.CompilerParams(
            dimension_semantics=("arbitrary", "arbitrary"),
            vmem_limit_bytes=_vmem_limit(pipelined, resident)),
        name="pool_mixer",
    )(x3, x3, gain, pool_w, scale)


def kernel(x, mix_norm, ffn_norm, final_norm, ab_w_in, lb_logits, hg_out_norm, ab_w_out,
           pool_w, pool_scale, ffn_w_gate, ffn_w_up, ffn_w_down):
    batch, seq, d = x.shape
    n_tok = batch * seq
    hg_width = HG_HEADS * HEAD_DIM
    sb_width = SB_HEADS * HEAD_DIM
    row = lambda a: a.reshape(1, -1)

    x2 = x.reshape(n_tok, d)
    ffn_weights = (ffn_w_gate, ffn_w_up, ffn_w_down)
    proj_a, qk, vt = _in_proj(x2, row(mix_norm[0]), ab_w_in.astype(BF16), 0, batch, seq, SB_BLOCK)
    o_a, *ffn1_weights = _hgrn(proj_a.reshape(batch, seq, 4 * hg_width), lb_logits, row(hg_out_norm[0]),
                               batch, seq, 0, ffn_weights, 1)
    o_b, *ffn0_weights = _stick_breaking(qk.reshape(batch, seq, 2 * sb_width), vt, batch, seq, ffn_weights, 0)
    x2 = _out_proj(o_a.reshape(n_tok, hg_width), o_b.reshape(n_tok, sb_width), ab_w_out.astype(BF16), 0, x2)
    x2 = _ffn(x2, row(ffn_norm[0]), *ffn0_weights, 0, row(final_norm), False, "ffn0")

    x3 = _pool_mixer(x2.reshape(batch, seq, d), row(mix_norm[1]), pool_w[0].astype(BF16), row(pool_scale[0]))
    x2 = _ffn(x3.reshape(n_tok, d), row(ffn_norm[1]), *ffn1_weights, 0, row(final_norm), True, "ffn1_final_norm")
    return x2.reshape(batch, seq, d)
```

```python
import functools
import math

import jax
import jax.numpy as jnp
from jax import lax
from jax.experimental import pallas as pl
from jax.experimental.pallas import tpu as pltpu

F32 = jnp.float32
BF16 = jnp.bfloat16

RMS_EPS = 1e-6
HEAD_DIM = 128
HG_HEADS = 8
SB_HEADS = 8
POOL_WINDOWS = (2, 4, 8, 16)

V7X_VMEM_BYTES = 64 * 1024 * 1024
V7X_VMEM_USABLE_BYTES = V7X_VMEM_BYTES * 13 // 16
V7X_DEFAULT_SCOPED_VMEM_BYTES = 32 * 1024 * 1024
F32_SUBLANES = 8
LOG2_E = math.log2(math.e)

HG_CHUNK = 128
HG_STEP_TOKENS = 2048
SB_BLOCK = 256
SB_QUERY_BLOCKS = 8
SB_UNDERFLOW_BITS = 160.0
PROJ_ROWS = 1024
PROJ_COLS = 1024
OUT_PROJ_ROWS = 512
FFN_ROWS = 512
FFN_HIDDEN_TILE = 512
POOL_ROWS = 512
CAST_SLAB = 128
POOL_HALO = 16


def _vmem_limit(pipelined_bytes, resident_bytes):
    need = 2 * pipelined_bytes + resident_bytes
    assert need <= V7X_VMEM_USABLE_BYTES, (need, V7X_VMEM_USABLE_BYTES)
    return int(max(need, V7X_DEFAULT_SCOPED_VMEM_BYTES))


def _nbytes(shape, dtype):
    return math.prod(shape) * jnp.dtype(dtype).itemsize


def _rms_norm(x, gain):
    ms = jnp.mean(x * x, axis=-1, keepdims=True)
    return x * lax.rsqrt(ms + RMS_EPS) * gain


def _dot(a, b):
    return jnp.dot(a, b, preferred_element_type=F32)


def _dot_nt(a, b):
    return lax.dot_general(a, b, (((1,), (1,)), ((), ())), preferred_element_type=F32)


def _sigmoid(x):
    return 0.5 * jnp.tanh(0.5 * x) + 0.5


def _silu(x):
    return x * _sigmoid(x)


def _in_proj_kernel(x_ref, g_ref, w_ref, oa_ref, oqk_ref, ovt_ref, h_ref, *, n_a_tiles, q_scale, key_block):
    j = pl.program_id(1)

    @pl.when(j == 0)
    def _():
        h_ref[...] = _rms_norm(x_ref[...], g_ref[...]).astype(BF16)

    def tile():
        return _dot(h_ref[...], w_ref[0])

    @pl.when(j < n_a_tiles)
    def _():
        oa_ref[...] = tile()

    @pl.when(j == n_a_tiles)
    def _():
        oqk_ref[...] = (tile() * q_scale).astype(oqk_ref.dtype)

    @pl.when(j == n_a_tiles + 1)
    def _():
        oqk_ref[...] = tile().astype(oqk_ref.dtype)

    @pl.when(j == n_a_tiles + 2)
    def _():
        vt = tile().T.astype(ovt_ref.dtype)
        for kb in range(ovt_ref.shape[1]):
            ovt_ref[0, kb] = vt[:, kb * key_block:(kb + 1) * key_block]


def _in_proj(x2, gain, w_in, layer, batch, seq, key_block):
    n_tok, d = x2.shape
    tm, tn = PROJ_ROWS, PROJ_COLS
    hg_width, sb_width = HG_HEADS * HEAD_DIM, SB_HEADS * HEAD_DIM
    n_a_tiles, n_qk_tiles, n_v_tiles = 4 * hg_width // tn, 2 * sb_width // tn, sb_width // tn
    assert tn == hg_width == sb_width and w_in.shape[2] == (n_a_tiles + n_qk_tiles + n_v_tiles) * tn
    assert seq % tm == 0 and tm % key_block == 0
    tiles_per_seq = seq // tm
    kb_per_tile = tm // key_block
    pipelined = (_nbytes((tm, d), F32) + _nbytes((d, tn), BF16) + _nbytes((tm, tn), F32)
                 + 2 * _nbytes((tm, tn), BF16) + _nbytes((1, d), F32))
    resident = _nbytes((tm, d), BF16) + _nbytes((tm, tn), F32)
    return pl.pallas_call(
        functools.partial(_in_proj_kernel, n_a_tiles=n_a_tiles, q_scale=LOG2_E / math.sqrt(HEAD_DIM),
                          key_block=key_block),
        out_shape=(jax.ShapeDtypeStruct((n_tok, n_a_tiles * tn), F32),
                   jax.ShapeDtypeStruct((n_tok, 2 * tn), BF16),
                   jax.ShapeDtypeStruct((batch, seq // key_block, tn, key_block), BF16)),
        grid=(n_tok // tm, n_a_tiles + n_qk_tiles + n_v_tiles),
        in_specs=[
            pl.BlockSpec((tm, d), lambda i, j: (i, 0)),
            pl.BlockSpec((1, d), lambda i, j: (0, 0)),
            pl.BlockSpec((1, d, tn), lambda i, j: (layer, 0, j)),
        ],
        out_specs=(
            pl.BlockSpec((tm, tn), lambda i, j: (i, jnp.minimum(j, n_a_tiles - 1))),
            pl.BlockSpec((tm, tn), lambda i, j: (i, jnp.clip(j - n_a_tiles, 0, n_qk_tiles - 1))),
            pl.BlockSpec((1, kb_per_tile, tn, key_block),
                         lambda i, j: (i // tiles_per_seq, i % tiles_per_seq, 0, 0)),
        ),
        scratch_shapes=[pltpu.VMEM((tm, d), BF16)],
        compiler_params=pltpu.CompilerParams(
            dimension_semantics=("arbitrary", "arbitrary"),
            vmem_limit_bytes=_vmem_limit(pipelined, resident)),
        name="in_proj",
    )(x2, gain, w_in)


def _cast_rider_specs(weights, layer, grid, step_of):
    w_gate, w_up, w_down = weights
    _, d, hidden = w_gate.shape
    assert w_up.shape == w_gate.shape and w_down.shape[1:] == (hidden, d) and hidden % CAST_SLAB == 0
    n_slabs = max(n for n in range(1, hidden // CAST_SLAB + 1)
                  if (hidden // CAST_SLAB) % n == 0 and n <= math.prod(grid))
    width = hidden // n_slabs

    def slab(*grid_idx):
        return jnp.minimum(step_of(*grid_idx), n_slabs - 1)

    col_in = pl.BlockSpec((1, d, width), lambda *g: (layer, 0, slab(*g)))
    col_out = pl.BlockSpec((1, d, width), lambda *g: (0, 0, slab(*g)))
    row_in = pl.BlockSpec((1, width, d), lambda *g: (layer, slab(*g), 0))
    row_out = pl.BlockSpec((1, width, d), lambda *g: (0, slab(*g), 0))
    out_shapes = [jax.ShapeDtypeStruct((1, d, hidden), BF16), jax.ShapeDtypeStruct((1, d, hidden), BF16),
                  jax.ShapeDtypeStruct((1, hidden, d), BF16)]
    slab_bytes = 3 * (_nbytes((d, width), F32) + _nbytes((d, width), BF16))
    return [col_in, col_in, row_in], [col_out, col_out, row_out], out_shapes, n_slabs, slab_bytes


def _cast_rider(step, n_slabs, src_refs, dst_refs):
    @pl.when(step < n_slabs)
    def _():
        for src, dst in zip(src_refs, dst_refs):
            dst[...] = src[...].astype(dst.dtype)


def _split2_bf16(x):
    hi = x.astype(BF16)
    return hi, (x - hi.astype(F32)).astype(BF16)


def _sublane_broadcast(x, group, row):
    rows, lanes = x.shape
    x3 = x.reshape(rows // group, group, lanes)
    return jnp.broadcast_to(x3[:, row:row + 1, :], x3.shape).reshape(rows, lanes)


def _hgrn_kernel(lbl_ref, gn_ref, tri_ref, lvl_ref, qa_ref, fa_ref, ia_ref, ga_ref, wg_ref, wu_ref, wd_ref,
                 o_ref, wg16_ref, wu16_ref, wd16_ref, st_ref, *, layer_row, n_slabs):
    c, ts = HG_CHUNK, HG_STEP_TOKENS
    n_levels = c.bit_length() - 1
    chunks = [slice(j * c, (j + 1) * c) for j in range(ts // c)]

    step = (pl.program_id(0) * pl.num_programs(1) + pl.program_id(1)) * pl.num_programs(2) + pl.program_id(2)
    _cast_rider(step, n_slabs, (wg_ref, wu_ref, wd_ref), (wg16_ref, wu16_ref, wd16_ref))

    @pl.when(pl.program_id(2) == 0)
    def _():
        st_ref[...] = jnp.zeros_like(st_ref)

    logits = lbl_ref[...]
    e = jnp.exp(logits - jnp.max(logits, axis=0, keepdims=True))
    lower = jnp.sum(e[:layer_row + 1], axis=0, keepdims=True) / jnp.sum(e, axis=0, keepdims=True)

    row = lax.broadcasted_iota(jnp.int32, (ts, HEAD_DIM), 0)
    q = _silu(qa_ref[0])
    f = lower + (1.0 - lower) * _sigmoid(fa_ref[0])
    k = 1.0 - f
    v = ia_ref[0]
    v16 = v.astype(BF16)

    tri = tri_ref[...]
    parts = _split2_bf16(jnp.log(f) * LOG2_E)
    g = jnp.concatenate([sum(_dot(tri, p[ch]) for p in parts) for ch in chunks], axis=0)

    level_of = lvl_ref[...]
    att = [jnp.zeros((c, c), F32) for _ in chunks]
    for lvl in range(n_levels):
        half = 1 << lvl
        if half >= F32_SUBLANES:
            shape4 = (ts // (2 * half), 2, half, HEAD_DIM)
            g4, q4, k4 = g.reshape(shape4), q.reshape(shape4), k.reshape(shape4)
            g_mid = g4[:, 0:1, half - 1:half, :]
            x_lo = k4[:, 0:1] * jnp.exp2(g_mid - g4[:, 0:1])
            x_hi = q4[:, 1:2] * jnp.exp2(g4[:, 1:2] - g_mid)
            x = jnp.concatenate([x_lo, x_hi], axis=1).reshape(ts, HEAD_DIM)
        else:
            upper = (row & half) != 0
            if lvl == 0:
                x = jnp.where(upper, q * f, k)
            else:
                g_mid = _sublane_broadcast(g, F32_SUBLANES, half - 1)
                if 2 * half < F32_SUBLANES:
                    second = _sublane_broadcast(g, F32_SUBLANES, 2 * half + half - 1)
                    g_mid = jnp.where((row & (F32_SUBLANES - 1)) < 2 * half, g_mid, second)
                x = jnp.where(upper, q, k) * jnp.exp2(-jnp.abs(g - g_mid))
        x16 = x.astype(BF16)
        for j, ch in enumerate(chunks):
            att[j] = jnp.where(level_of == lvl, _dot_nt(x16[ch], x16[ch]), att[j])
    qk_diag = jnp.sum(q * k, axis=-1, keepdims=True)

    g_last = _sublane_broadcast(g, c, c - 1)
    q_dec = (q * jnp.exp2(g)).astype(BF16)
    k_dec = (k * jnp.exp2(g_last - g)).astype(BF16)
    st = st_ref[...]
    outs = []
    for j, ch in enumerate(chunks):
        a = jnp.where(level_of == n_levels, qk_diag[ch], att[j])
        outs.append(_dot_nt(q_dec[ch], st.astype(BF16)) + _dot(a.astype(BF16), v16[ch]))
        st = st * jnp.exp2(g_last[ch][:1]) + _dot(v[ch].T.astype(BF16), k_dec[ch])
    st_ref[...] = st

    y = _rms_norm(jnp.concatenate(outs, axis=0), gn_ref[...]) * _silu(ga_ref[0])
    o_ref[0] = y.astype(o_ref.dtype)


def _hgrn(proj, lb_logits, out_gain, batch, seq, layer_row, cast_weights, cast_layer):
    ts, c, hd = HG_STEP_TOKENS, HG_CHUNK, HEAD_DIM
    assert seq % ts == 0 and ts % c == 0
    n_rows = lb_logits.shape[0]
    grid = (batch, HG_HEADS, seq // ts)
    cast_in, cast_out, cast_shapes, n_slabs, slab_bytes = _cast_rider_specs(
        cast_weights, cast_layer, grid, lambda b, h, s: (b * grid[1] + h) * grid[2] + s)
    t_idx = lax.broadcasted_iota(jnp.int32, (c, c), 0)
    s_idx = lax.broadcasted_iota(jnp.int32, (c, c), 1)
    tri = (t_idx >= s_idx).astype(BF16)
    level_of = jnp.where(t_idx > s_idx, 31 - lax.clz(t_idx ^ s_idx),
                         jnp.where(t_idx == s_idx, c.bit_length() - 1, -1)).astype(jnp.int32)

    def head_block(part):
        return pl.BlockSpec((1, ts, hd), lambda b, h, s: (b, s, part * HG_HEADS + h))

    pipelined = (4 * _nbytes((ts, hd), F32) + _nbytes((ts, hd), BF16) + _nbytes((c, c), BF16)
                 + _nbytes((c, c), jnp.int32) + slab_bytes)
    resident = 24 * _nbytes((ts, hd), F32)
    return pl.pallas_call(
        functools.partial(_hgrn_kernel, layer_row=layer_row, n_slabs=n_slabs),
        out_shape=[jax.ShapeDtypeStruct((batch, seq, HG_HEADS * hd), BF16)] + cast_shapes,
        grid=grid,
        in_specs=[
            pl.BlockSpec((n_rows, hd), lambda b, h, s: (0, h)),
            pl.BlockSpec((1, hd), lambda b, h, s: (0, 0)),
            pl.BlockSpec((c, c), lambda b, h, s: (0, 0)),
            pl.BlockSpec((c, c), lambda b, h, s: (0, 0)),
            head_block(0), head_block(1), head_block(2), head_block(3),
        ] + cast_in,
        out_specs=[pl.BlockSpec((1, ts, hd), lambda b, h, s: (b, s, h))] + cast_out,
        scratch_shapes=[pltpu.VMEM((hd, hd), F32)],
        compiler_params=pltpu.CompilerParams(
            dimension_semantics=("arbitrary", "arbitrary", "arbitrary"),
            vmem_limit_bytes=_vmem_limit(pipelined, resident)),
        name="hgrn2",
    )(lb_logits, out_gain, tri, level_of, proj, proj, proj, proj, *cast_weights)


def _sb_kernel(sfx_ref, q_ref, k_ref, vt_ref, wg_ref, wu_ref, wd_ref, o_ref, wg16_ref, wu16_ref, wd16_ref,
               acc_ref, carry_ref, sp_ref, lbc_ref, w_ref, *, n_slabs):
    blk, nq = SB_BLOCK, SB_QUERY_BLOCKS
    step = (pl.program_id(0) * pl.num_programs(1) + pl.program_id(1)) * pl.num_programs(2) + pl.program_id(2)
    _cast_rider(step, n_slabs, (wg_ref, wu_ref, wd_ref), (wg16_ref, wu16_ref, wd16_ref))

    first = nq * pl.program_id(2)
    n_depths = first + nq
    causal = (lax.broadcasted_iota(jnp.int32, (blk, blk), 0)
              < lax.broadcasted_iota(jnp.int32, (blk, blk), 1))
    sfx = sfx_ref[...]

    def key_block(h, depth, always_live):
        kb = first + h - depth
        return (kb, None) if always_live else (jnp.maximum(kb, 0), kb >= 0)

    def scores(depth, slot, on_diagonal=False):
        for h in range(nq):
            j, live = key_block(h, depth, on_diagonal)
            ks = k_ref[0, pl.ds(pl.multiple_of(j * blk, blk), blk), :]
            z = _dot_nt(ks, q_ref[0, h * blk:(h + 1) * blk, :])
            sp = jnp.maximum(z, 0.0) + jnp.log2(1.0 + jnp.exp2(-jnp.abs(z)))
            lbc_ref[slot, h] = (z - sp) - carry_ref[h]
            if on_diagonal:
                sp = jnp.where(causal, sp, 0.0)
            sp_ref[slot, h] = sp.astype(BF16)
            total = jnp.sum(sp, axis=0, keepdims=True)
            carry_ref[h] += total if live is None else jnp.where(live, total, 0.0)

    def weights(slot, on_diagonal=False):
        for h in range(nq):
            w = jnp.exp2(lbc_ref[slot, h] - _dot(sfx, sp_ref[slot, h]))
            if on_diagonal:
                w = jnp.where(causal, w, 0.0)
            w_ref[slot, h] = w.astype(BF16)

    def values(depth, slot, on_diagonal=False):
        for h in range(nq):
            j, live = key_block(h, depth, on_diagonal)
            pv = _dot(vt_ref[0, j], w_ref[slot, h])
            acc_ref[h] += pv if live is None else jnp.where(live, pv, 0.0)

    def min_open_carry(next_depth):
        h_idx = lax.broadcasted_iota(jnp.int32, carry_ref.shape, 0)
        return jnp.min(jnp.where(first + h_idx - next_depth >= 0, carry_ref[...], SB_UNDERFLOW_BITS))

    acc_ref[...] = jnp.zeros_like(acc_ref)
    carry_ref[...] = jnp.zeros_like(carry_ref)

    scores(0, 0, on_diagonal=True)
    scores(1, 1)
    weights(0, on_diagonal=True)

    def keep_going(state):
        pairs, open_carry = state
        return jnp.logical_and(pairs < (n_depths - 2) // 2, open_carry < SB_UNDERFLOW_BITS)

    def body(state):
        pairs, _ = state
        d = 2 * pairs + 2
        scores(d, 0)
        weights(1)
        values(d - 2, 0)
        scores(d + 1, 1)
        weights(0)
        values(d - 1, 1)
        return pairs + 1, min_open_carry(d + 2)

    pairs_done, _ = lax.while_loop(keep_going, body, (jnp.int32(0), min_open_carry(2)))
    scored = 2 * pairs_done + 2
    weights(1)
    values(scored - 2, 0)
    values(scored - 1, 1)

    for h in range(nq):
        o_ref[0, h * blk:(h + 1) * blk, :] = acc_ref[h].T.astype(o_ref.dtype)


def _stick_breaking(qk, vt, batch, seq, cast_weights, cast_layer):
    blk, nq, hd = SB_BLOCK, SB_QUERY_BLOCKS, HEAD_DIM
    tq = nq * blk
    assert seq % tq == 0 and nq % 2 == 0
    n_blk = seq // blk
    grid = (batch, SB_HEADS, seq // tq)
    cast_in, cast_out, cast_shapes, n_slabs, slab_bytes = _cast_rider_specs(
        cast_weights, cast_layer, grid, lambda b, h, i: (b * grid[1] + h) * grid[2] + i)
    sfx = (lax.broadcasted_iota(jnp.int32, (blk, blk), 1) > lax.broadcasted_iota(jnp.int32, (blk, blk), 0)).astype(BF16)
    pipelined = (_nbytes((blk, blk), BF16) + 2 * _nbytes((tq, hd), BF16) + 2 * _nbytes((seq, hd), BF16)
                 + slab_bytes)
    slots = 2 * nq * (2 * _nbytes((blk, blk), BF16) + _nbytes((blk, blk), F32))
    resident = nq * _nbytes((hd, blk), F32) + slots + nq * 4 * _nbytes((blk, blk), F32)
    return pl.pallas_call(
        functools.partial(_sb_kernel, n_slabs=n_slabs),
        out_shape=[jax.ShapeDtypeStruct((batch, seq, SB_HEADS * hd), BF16)] + cast_shapes,
        grid=grid,
        in_specs=[
            pl.BlockSpec((blk, blk), lambda b, h, i: (0, 0)),
            pl.BlockSpec((1, tq, hd), lambda b, h, i: (b, i, h)),
            pl.BlockSpec((1, seq, hd), lambda b, h, i: (b, 0, SB_HEADS + h)),
            pl.BlockSpec((1, n_blk, hd, blk), lambda b, h, i: (b, 0, h, 0)),
        ] + cast_in,
        out_specs=[pl.BlockSpec((1, tq, hd), lambda b, h, i: (b, i, h))] + cast_out,
        scratch_shapes=[
            pltpu.VMEM((nq, hd, blk), F32),
            pltpu.VMEM((nq, 1, blk), F32),
            pltpu.VMEM((2, nq, blk, blk), BF16),
            pltpu.VMEM((2, nq, blk, blk), F32),
            pltpu.VMEM((2, nq, blk, blk), BF16),
        ],
        compiler_params=pltpu.CompilerParams(
            dimension_semantics=("arbitrary", "arbitrary", "arbitrary"),
            vmem_limit_bytes=_vmem_limit(pipelined, resident)),
        name="stick_breaking",
    )(sfx, qk, qk, vt, *cast_weights)


def _out_proj_kernel(oa_ref, ob_ref, wa_ref, wb_ref, x_ref, o_ref):
    o_ref[...] = x_ref[...] + _dot(oa_ref[...], wa_ref[0]) + _dot(ob_ref[...], wb_ref[0])


def _out_proj(o_a, o_b, w_out, layer, x2):
    n_tok, d = x2.shape
    ka, kb = o_a.shape[1], o_b.shape[1]
    tm = OUT_PROJ_ROWS
    assert n_tok % tm == 0 and ka == kb and w_out.shape[1] == ka + kb
    pipelined = (_nbytes((tm, ka), BF16) + _nbytes((tm, kb), BF16) + _nbytes((ka, d), BF16)
                 + _nbytes((kb, d), BF16) + 2 * _nbytes((tm, d), F32))
    resident = 2 * _nbytes((tm, d), F32)
    return pl.pallas_call(
        _out_proj_kernel,
        out_shape=jax.ShapeDtypeStruct((n_tok, d), F32),
        grid=(n_tok // tm,),
        in_specs=[
            pl.BlockSpec((tm, ka), lambda i: (i, 0)),
            pl.BlockSpec((tm, kb), lambda i: (i, 0)),
            pl.BlockSpec((1, ka, d), lambda i: (layer, 0, 0)),
            pl.BlockSpec((1, kb, d), lambda i: (layer, 1, 0)),
            pl.BlockSpec((tm, d), lambda i: (i, 0)),
        ],
        out_specs=pl.BlockSpec((tm, d), lambda i: (i, 0)),
        compiler_params=pltpu.CompilerParams(
            dimension_semantics=("arbitrary",),
            vmem_limit_bytes=_vmem_limit(pipelined, resident)),
        name="out_proj",
    )(o_a, o_b, w_out, w_out, x2)


def _ffn_kernel(x_ref, g_ref, wg_ref, wu_ref, wd_ref, fg_ref, o_ref, h_ref, *, final_norm):
    j = pl.program_id(1)

    @pl.when(j == 0)
    def _():
        x = x_ref[...]
        h_ref[...] = _rms_norm(x, g_ref[...]).astype(BF16)
        o_ref[...] = x

    h = h_ref[...]
    act = _silu(_dot(h, wg_ref[0])) * _dot(h, wu_ref[0])
    o_ref[...] += _dot(act.astype(BF16), wd_ref[0])

    if final_norm:
        @pl.when(j == pl.num_programs(1) - 1)
        def _():
            o_ref[...] = _rms_norm(o_ref[...], fg_ref[...])


def _ffn(x2, gain, w_gate, w_up, w_down, layer, final_gain, final_norm, name):
    n_tok, d = x2.shape
    hidden = w_gate.shape[2]
    tm, th = FFN_ROWS, FFN_HIDDEN_TILE
    assert n_tok % tm == 0 and hidden % th == 0
    pipelined = (2 * _nbytes((tm, d), F32) + 2 * _nbytes((d, th), BF16) + _nbytes((th, d), BF16)
                 + 2 * _nbytes((1, d), F32))
    resident = _nbytes((tm, d), BF16) + 4 * _nbytes((tm, th), F32) + _nbytes((tm, d), F32)
    return pl.pallas_call(
        functools.partial(_ffn_kernel, final_norm=final_norm),
        out_shape=jax.ShapeDtypeStruct((n_tok, d), F32),
        grid=(n_tok // tm, hidden // th),
        in_specs=[
            pl.BlockSpec((tm, d), lambda i, j: (i, 0)),
            pl.BlockSpec((1, d), lambda i, j: (0, 0)),
            pl.BlockSpec((1, d, th), lambda i, j: (layer, 0, j)),
            pl.BlockSpec((1, d, th), lambda i, j: (layer, 0, j)),
            pl.BlockSpec((1, th, d), lambda i, j: (layer, j, 0)),
            pl.BlockSpec((1, d), lambda i, j: (0, 0)),
        ],
        out_specs=pl.BlockSpec((tm, d), lambda i, j: (i, 0)),
        scratch_shapes=[pltpu.VMEM((tm, d), BF16)],
        compiler_params=pltpu.CompilerParams(
            dimension_semantics=("arbitrary", "arbitrary"),
            vmem_limit_bytes=_vmem_limit(pipelined, resident)),
        name=name,
    )(x2, gain, w_gate, w_up, w_down, final_gain)


def _pool_kernel(x_ref, halo_ref, g_ref, pw_ref, sc_ref, o_ref, hs_ref):
    ts = x_ref.shape[1]
    halo = POOL_HALO
    i = pl.program_id(1)
    gain = g_ref[...]
    x = x_ref[0]
    hs_ref[0:halo, :] = jnp.where(i == 0, 0.0, _rms_norm(halo_ref[0], gain))
    hs_ref[halo:halo + ts, :] = _rms_norm(x, gain)

    pos = i * ts + lax.broadcasted_iota(jnp.int32, (ts, 1), 0)
    group = pw_ref.shape[1]
    for gi, win in enumerate(POOL_WINDOWS):
        cols = slice(gi * group, (gi + 1) * group)
        wsum = hs_ref[:, cols]
        for s in range(win.bit_length() - 1):
            wsum = wsum + pltpu.roll(wsum, 1 << s, axis=0)
        h = hs_ref[halo:halo + ts, cols]
        inv_count = 1.0 / jnp.minimum(pos + 1, win).astype(F32)
        pooled = wsum[halo:] * inv_count - h
        mixed = _dot(pooled.astype(BF16), pw_ref[gi])
        o_ref[0, :, cols] = x[:, cols] + mixed * sc_ref[:, cols]


def _pool_mixer(x3, gain, pool_w, scale):
    batch, seq, d = x3.shape
    ts, halo = POOL_ROWS, POOL_HALO
    n_groups, group = pool_w.shape[0], pool_w.shape[1]
    assert seq % ts == 0 and ts % halo == 0 and max(POOL_WINDOWS) - 1 <= halo
    assert all(w & (w - 1) == 0 for w in POOL_WINDOWS)
    assert n_groups == len(POOL_WINDOWS) and n_groups * group == d
    halo_per_tile = ts // halo
    pipelined = (2 * _nbytes((ts, d), F32) + _nbytes((halo, d), F32) + _nbytes(pool_w.shape, BF16)
                 + 2 * _nbytes((1, d), F32))
    resident = _nbytes((ts + halo, d), F32) + 3 * _nbytes((ts, d), F32)
    return pl.pallas_call(
        _pool_kernel,
        out_shape=jax.ShapeDtypeStruct((batch, seq, d), F32),
        grid=(batch, seq // ts),
        in_specs=[
            pl.BlockSpec((1, ts, d), lambda b, i: (b, i, 0)),
            pl.BlockSpec((1, halo, d), lambda b, i: (b, jnp.maximum(i * halo_per_tile - 1, 0), 0)),
            pl.BlockSpec((1, d), lambda b, i: (0, 0)),
            pl.BlockSpec((n_groups, group, group), lambda b, i: (0, 0, 0)),
            pl.BlockSpec((1, d), lambda b, i: (0, 0)),
        ],
        out_specs=pl.BlockSpec((1, ts, d), lambda b, i: (b, i, 0)),
        scratch_shapes=[pltpu.VMEM((ts + halo, d), F32)],
        compiler_params=pltpu.CompilerParams(
            dimension_semantics=("arbitrary", "arbitrary"),
            vmem_limit_bytes=_vmem_limit(pipelined, resident)),
        name="pool_mixer",
    )(x3, x3, gain, pool_w, scale)


def kernel(x, mix_norm, ffn_norm, final_norm, ab_w_in, lb_logits, hg_out_norm, ab_w_out,
           pool_w, pool_scale, ffn_w_gate, ffn_w_up, ffn_w_down):
    batch, seq, d = x.shape
    n_tok = batch * seq
    hg_width = HG_HEADS * HEAD_DIM
    sb_width = SB_HEADS * HEAD_DIM
    row = lambda a: a.reshape(1, -1)

    x2 = x.reshape(n_tok, d)
    ffn_weights = (ffn_w_gate, ffn_w_up, ffn_w_down)
    proj_a, qk, vt = _in_proj(x2, row(mix_norm[0]), ab_w_in.astype(BF16), 0, batch, seq, SB_BLOCK)
    o_a, *ffn1_weights = _hgrn(proj_a.reshape(batch, seq, 4 * hg_width), lb_logits, row(hg_out_norm[0]),
                               batch, seq, 0, ffn_weights, 1)
    o_b, *ffn0_weights = _stick_breaking(qk.reshape(batch, seq, 2 * sb_width), vt, batch, seq, ffn_weights, 0)
    x2 = _out_proj(o_a.reshape(n_tok, hg_width), o_b.reshape(n_tok, sb_width), ab_w_out.astype(BF16), 0, x2)
    x2 = _ffn(x2, row(ffn_norm[0]), *ffn0_weights, 0, row(final_norm), False, "ffn0")

    x3 = _pool_mixer(x2.reshape(batch, seq, d), row(mix_norm[1]), pool_w[0].astype(BF16), row(pool_scale[0]))
    x2 = _ffn(x3.reshape(n_tok, d), row(ffn_norm[1]), *ffn1_weights, 0, row(final_norm), True, "ffn1_final_norm")
    return x2.reshape(batch, seq, d)
```

```python
import functools
import math

import jax
import jax.numpy as jnp
from jax import lax
from jax.experimental import pallas as pl
from jax.experimental.pallas import tpu as pltpu

F32 = jnp.float32
BF16 = jnp.bfloat16

RMS_EPS = 1e-6
HEAD_DIM = 128
HG_HEADS = 8
SB_HEADS = 8
POOL_WINDOWS = (2, 4, 8, 16)

V7X_VMEM_BYTES = 64 * 1024 * 1024
V7X_VMEM_USABLE_BYTES = V7X_VMEM_BYTES * 13 // 16
V7X_DEFAULT_SCOPED_VMEM_BYTES = 32 * 1024 * 1024
F32_SUBLANES = 8
LOG2_E = math.log2(math.e)

HG_CHUNK = 128
HG_STEP_TOKENS = 2048
SB_BLOCK = 256
SB_QUERY_BLOCKS = 16
SB_UNDERFLOW_BITS = 160.0
PROJ_ROWS = 1024
PROJ_COLS = 1024
OUT_PROJ_ROWS = 512
FFN_ROWS = 512
FFN_HIDDEN_TILE = 512
POOL_ROWS = 512
CAST_SLAB = 128
POOL_HALO = 16


def _vmem_limit(pipelined_bytes, resident_bytes):
    need = 2 * pipelined_bytes + resident_bytes
    assert need <= V7X_VMEM_USABLE_BYTES, (need, V7X_VMEM_USABLE_BYTES)
    return int(max(need, V7X_DEFAULT_SCOPED_VMEM_BYTES))


def _nbytes(shape, dtype):
    return math.prod(shape) * jnp.dtype(dtype).itemsize


def _rms_norm(x, gain):
    ms = jnp.mean(x * x, axis=-1, keepdims=True)
    return x * lax.rsqrt(ms + RMS_EPS) * gain


def _dot(a, b):
    return jnp.dot(a, b, preferred_element_type=F32)


def _dot_nt(a, b):
    return lax.dot_general(a, b, (((1,), (1,)), ((), ())), preferred_element_type=F32)


def _sigmoid(x):
    return 0.5 * jnp.tanh(0.5 * x) + 0.5


def _silu(x):
    return x * _sigmoid(x)


def _in_proj_kernel(x_ref, g_ref, w_ref, oa_ref, oqk_ref, ovt_ref, h_ref, *, n_a_tiles, q_scale, key_block):
    j = pl.program_id(1)

    @pl.when(j == 0)
    def _():
        h_ref[...] = _rms_norm(x_ref[...], g_ref[...]).astype(BF16)

    def tile():
        return _dot(h_ref[...], w_ref[0])

    @pl.when(j < n_a_tiles)
    def _():
        oa_ref[...] = tile()

    @pl.when(j == n_a_tiles)
    def _():
        oqk_ref[...] = (tile() * q_scale).astype(oqk_ref.dtype)

    @pl.when(j == n_a_tiles + 1)
    def _():
        oqk_ref[...] = tile().astype(oqk_ref.dtype)

    @pl.when(j == n_a_tiles + 2)
    def _():
        vt = tile().T.astype(ovt_ref.dtype)
        for kb in range(ovt_ref.shape[1]):
            ovt_ref[0, kb] = vt[:, kb * key_block:(kb + 1) * key_block]


def _in_proj(x2, gain, w_in, layer, batch, seq, key_block):
    n_tok, d = x2.shape
    tm, tn = PROJ_ROWS, PROJ_COLS
    hg_width, sb_width = HG_HEADS * HEAD_DIM, SB_HEADS * HEAD_DIM
    n_a_tiles, n_qk_tiles, n_v_tiles = 4 * hg_width // tn, 2 * sb_width // tn, sb_width // tn
    assert tn == hg_width == sb_width and w_in.shape[2] == (n_a_tiles + n_qk_tiles + n_v_tiles) * tn
    assert seq % tm == 0 and tm % key_block == 0
    tiles_per_seq = seq // tm
    kb_per_tile = tm // key_block
    pipelined = (_nbytes((tm, d), F32) + _nbytes((d, tn), BF16) + _nbytes((tm, tn), F32)
                 + 2 * _nbytes((tm, tn), BF16) + _nbytes((1, d), F32))
    resident = _nbytes((tm, d), BF16) + _nbytes((tm, tn), F32)
    return pl.pallas_call(
        functools.partial(_in_proj_kernel, n_a_tiles=n_a_tiles, q_scale=LOG2_E / math.sqrt(HEAD_DIM),
                          key_block=key_block),
        out_shape=(jax.ShapeDtypeStruct((n_tok, n_a_tiles * tn), F32),
                   jax.ShapeDtypeStruct((n_tok, 2 * tn), BF16),
                   jax.ShapeDtypeStruct((batch, seq // key_block, tn, key_block), BF16)),
        grid=(n_tok // tm, n_a_tiles + n_qk_tiles + n_v_tiles),
        in_specs=[
            pl.BlockSpec((tm, d), lambda i, j: (i, 0)),
            pl.BlockSpec((1, d), lambda i, j: (0, 0)),
            pl.BlockSpec((1, d, tn), lambda i, j: (layer, 0, j)),
        ],
        out_specs=(
            pl.BlockSpec((tm, tn), lambda i, j: (i, jnp.minimum(j, n_a_tiles - 1))),
            pl.BlockSpec((tm, tn), lambda i, j: (i, jnp.clip(j - n_a_tiles, 0, n_qk_tiles - 1))),
            pl.BlockSpec((1, kb_per_tile, tn, key_block),
                         lambda i, j: (i // tiles_per_seq, i % tiles_per_seq, 0, 0)),
        ),
        scratch_shapes=[pltpu.VMEM((tm, d), BF16)],
        compiler_params=pltpu.CompilerParams(
            dimension_semantics=("arbitrary", "arbitrary"),
            vmem_limit_bytes=_vmem_limit(pipelined, resident)),
        name="in_proj",
    )(x2, gain, w_in)


def _cast_rider_specs(weights, layers, grid, step_of):
    w_gate, w_up, w_down = weights
    _, d, hidden = w_gate.shape
    assert w_up.shape == w_gate.shape and w_down.shape[1:] == (hidden, d) and hidden % CAST_SLAB == 0
    n_slabs = max(n for n in range(1, hidden // CAST_SLAB + 1)
                  if (hidden // CAST_SLAB) % n == 0 and n <= math.prod(grid))
    width = hidden // n_slabs

    def slab(*grid_idx):
        return jnp.minimum(step_of(*grid_idx), n_slabs - 1)

    col_out = pl.BlockSpec((1, d, width), lambda *g: (0, 0, slab(*g)))
    row_out = pl.BlockSpec((1, width, d), lambda *g: (0, slab(*g), 0))
    in_specs, out_specs, out_shapes = [], [], []
    for layer in layers:
        col_in = pl.BlockSpec((1, d, width), lambda *g, layer=layer: (layer, 0, slab(*g)))
        row_in = pl.BlockSpec((1, width, d), lambda *g, layer=layer: (layer, slab(*g), 0))
        in_specs += [col_in, col_in, row_in]
        out_specs += [col_out, col_out, row_out]
        out_shapes += [jax.ShapeDtypeStruct((1, d, hidden), BF16), jax.ShapeDtypeStruct((1, d, hidden), BF16),
                       jax.ShapeDtypeStruct((1, hidden, d), BF16)]
    slab_bytes = 3 * len(layers) * (_nbytes((d, width), F32) + _nbytes((d, width), BF16))
    return in_specs, out_specs, out_shapes, n_slabs, slab_bytes


def _cast_rider(step, n_slabs, src_refs, dst_refs):
    @pl.when(step < n_slabs)
    def _():
        for src, dst in zip(src_refs, dst_refs):
            dst[...] = src[...].astype(dst.dtype)


def _split2_bf16(x):
    hi = x.astype(BF16)
    return hi, (x - hi.astype(F32)).astype(BF16)


def _sublane_broadcast(x, group, row):
    rows, lanes = x.shape
    x3 = x.reshape(rows // group, group, lanes)
    return jnp.broadcast_to(x3[:, row:row + 1, :], x3.shape).reshape(rows, lanes)


def _hgrn_kernel(lbl_ref, gn_ref, tri_ref, lvl_ref, qa_ref, fa_ref, ia_ref, ga_ref, *refs, layer_row, n_slabs):
    n_cast = (len(refs) - 2) // 2
    cast_src, o_ref, cast_dst, st_ref = refs[:n_cast], refs[n_cast], refs[n_cast + 1:-1], refs[-1]
    c, ts = HG_CHUNK, HG_STEP_TOKENS
    n_levels = c.bit_length() - 1
    chunks = [slice(j * c, (j + 1) * c) for j in range(ts // c)]

    step = (pl.program_id(0) * pl.num_programs(1) + pl.program_id(1)) * pl.num_programs(2) + pl.program_id(2)
    _cast_rider(step, n_slabs, cast_src, cast_dst)

    @pl.when(pl.program_id(2) == 0)
    def _():
        st_ref[...] = jnp.zeros_like(st_ref)

    logits = lbl_ref[...]
    e = jnp.exp(logits - jnp.max(logits, axis=0, keepdims=True))
    lower = jnp.sum(e[:layer_row + 1], axis=0, keepdims=True) / jnp.sum(e, axis=0, keepdims=True)

    row = lax.broadcasted_iota(jnp.int32, (ts, HEAD_DIM), 0)
    q = _silu(qa_ref[0])
    f = lower + (1.0 - lower) * _sigmoid(fa_ref[0])
    k = 1.0 - f
    v = ia_ref[0]
    v16 = v.astype(BF16)

    tri = tri_ref[...]
    parts = _split2_bf16(jnp.log(f) * LOG2_E)
    g = jnp.concatenate([sum(_dot(tri, p[ch]) for p in parts) for ch in chunks], axis=0)

    level_of = lvl_ref[...]
    att = [jnp.zeros((c, c), F32) for _ in chunks]
    for lvl in range(n_levels):
        half = 1 << lvl
        if half >= F32_SUBLANES:
            shape4 = (ts // (2 * half), 2, half, HEAD_DIM)
            g4, q4, k4 = g.reshape(shape4), q.reshape(shape4), k.reshape(shape4)
            g_mid = g4[:, 0:1, half - 1:half, :]
            x_lo = k4[:, 0:1] * jnp.exp2(g_mid - g4[:, 0:1])
            x_hi = q4[:, 1:2] * jnp.exp2(g4[:, 1:2] - g_mid)
            x = jnp.concatenate([x_lo, x_hi], axis=1).reshape(ts, HEAD_DIM)
        else:
            upper = (row & half) != 0
            if lvl == 0:
                x = jnp.where(upper, q * f, k)
            else:
                g_mid = _sublane_broadcast(g, F32_SUBLANES, half - 1)
                if 2 * half < F32_SUBLANES:
                    second = _sublane_broadcast(g, F32_SUBLANES, 2 * half + half - 1)
                    g_mid = jnp.where((row & (F32_SUBLANES - 1)) < 2 * half, g_mid, second)
                x = jnp.where(upper, q, k) * jnp.exp2(-jnp.abs(g - g_mid))
        x16 = x.astype(BF16)
        for j, ch in enumerate(chunks):
            att[j] = jnp.where(level_of == lvl, _dot_nt(x16[ch], x16[ch]), att[j])
    qk_diag = jnp.sum(q * k, axis=-1, keepdims=True)

    g_last = _sublane_broadcast(g, c, c - 1)
    q_dec = (q * jnp.exp2(g)).astype(BF16)
    k_dec = (k * jnp.exp2(g_last - g)).astype(BF16)
    st = st_ref[...]
    outs = []
    for j, ch in enumerate(chunks):
        a = jnp.where(level_of == n_levels, qk_diag[ch], att[j])
        outs.append(_dot_nt(q_dec[ch], st.astype(BF16)) + _dot(a.astype(BF16), v16[ch]))
        st = st * jnp.exp2(g_last[ch][:1]) + _dot(v[ch].T.astype(BF16), k_dec[ch])
    st_ref[...] = st

    y = _rms_norm(jnp.concatenate(outs, axis=0), gn_ref[...]) * _silu(ga_ref[0])
    o_ref[0] = y.astype(o_ref.dtype)


def _hgrn(proj, lb_logits, out_gain, batch, seq, layer_row, cast_weights, cast_layers):
    ts, c, hd = HG_STEP_TOKENS, HG_CHUNK, HEAD_DIM
    assert seq % ts == 0 and ts % c == 0
    n_rows = lb_logits.shape[0]
    grid = (batch, HG_HEADS, seq // ts)
    cast_in, cast_out, cast_shapes, n_slabs, slab_bytes = _cast_rider_specs(
        cast_weights, cast_layers, grid, lambda b, h, s: (b * grid[1] + h) * grid[2] + s)
    t_idx = lax.broadcasted_iota(jnp.int32, (c, c), 0)
    s_idx = lax.broadcasted_iota(jnp.int32, (c, c), 1)
    tri = (t_idx >= s_idx).astype(BF16)
    level_of = jnp.where(t_idx > s_idx, 31 - lax.clz(t_idx ^ s_idx),
                         jnp.where(t_idx == s_idx, c.bit_length() - 1, -1)).astype(jnp.int32)

    def head_block(part):
        return pl.BlockSpec((1, ts, hd), lambda b, h, s: (b, s, part * HG_HEADS + h))

    pipelined = (4 * _nbytes((ts, hd), F32) + _nbytes((ts, hd), BF16) + _nbytes((c, c), BF16)
                 + _nbytes((c, c), jnp.int32) + slab_bytes)
    resident = 24 * _nbytes((ts, hd), F32)
    return pl.pallas_call(
        functools.partial(_hgrn_kernel, layer_row=layer_row, n_slabs=n_slabs),
        out_shape=[jax.ShapeDtypeStruct((batch, seq, HG_HEADS * hd), BF16)] + cast_shapes,
        grid=grid,
        in_specs=[
            pl.BlockSpec((n_rows, hd), lambda b, h, s: (0, h)),
            pl.BlockSpec((1, hd), lambda b, h, s: (0, 0)),
            pl.BlockSpec((c, c), lambda b, h, s: (0, 0)),
            pl.BlockSpec((c, c), lambda b, h, s: (0, 0)),
            head_block(0), head_block(1), head_block(2), head_block(3),
        ] + cast_in,
        out_specs=[pl.BlockSpec((1, ts, hd), lambda b, h, s: (b, s, h))] + cast_out,
        scratch_shapes=[pltpu.VMEM((hd, hd), F32)],
        compiler_params=pltpu.CompilerParams(
            dimension_semantics=("arbitrary", "arbitrary", "arbitrary"),
            vmem_limit_bytes=_vmem_limit(pipelined, resident)),
        name="hgrn2",
    )(lb_logits, out_gain, tri, level_of, proj, proj, proj, proj, *(cast_weights * len(cast_layers)))


def _sb_kernel(sfx_ref, q_ref, k_ref, vt_ref, o_ref, acc_ref, carry_ref, sp_ref, lbc_ref, w_ref):
    blk, nq = SB_BLOCK, SB_QUERY_BLOCKS
    first = nq * pl.program_id(2)
    n_depths = first + nq
    causal = (lax.broadcasted_iota(jnp.int32, (blk, blk), 0)
              < lax.broadcasted_iota(jnp.int32, (blk, blk), 1))
    sfx = sfx_ref[...]

    def key_block(h, depth, always_live):
        kb = first + h - depth
        return (kb, None) if always_live else (jnp.maximum(kb, 0), kb >= 0)

    def scores(depth, slot, on_diagonal=False):
        for h in range(nq):
            j, live = key_block(h, depth, on_diagonal)
            ks = k_ref[0, pl.ds(pl.multiple_of(j * blk, blk), blk), :]
            z = _dot_nt(ks, q_ref[0, h * blk:(h + 1) * blk, :])
            sp = jnp.maximum(z, 0.0) + jnp.log2(1.0 + jnp.exp2(-jnp.abs(z)))
            lbc_ref[slot, h] = (z - sp) - carry_ref[h]
            if on_diagonal:
                sp = jnp.where(causal, sp, 0.0)
            sp_ref[slot, h] = sp.astype(BF16)
            total = jnp.sum(sp, axis=0, keepdims=True)
            carry_ref[h] += total if live is None else jnp.where(live, total, 0.0)

    def weights(slot, on_diagonal=False):
        for h in range(nq):
            w = jnp.exp2(lbc_ref[slot, h] - _dot(sfx, sp_ref[slot, h]))
            if on_diagonal:
                w = jnp.where(causal, w, 0.0)
            w_ref[slot, h] = w.astype(BF16)

    def values(depth, slot, on_diagonal=False):
        for h in range(nq):
            j, live = key_block(h, depth, on_diagonal)
            pv = _dot(vt_ref[0, j], w_ref[slot, h])
            acc_ref[h] += pv if live is None else jnp.where(live, pv, 0.0)

    def min_open_carry(next_depth):
        h_idx = lax.broadcasted_iota(jnp.int32, carry_ref.shape, 0)
        return jnp.min(jnp.where(first + h_idx - next_depth >= 0, carry_ref[...], SB_UNDERFLOW_BITS))

    acc_ref[...] = jnp.zeros_like(acc_ref)
    carry_ref[...] = jnp.zeros_like(carry_ref)

    scores(0, 0, on_diagonal=True)
    scores(1, 1)
    weights(0, on_diagonal=True)

    def keep_going(state):
        pairs, open_carry = state
        return jnp.logical_and(pairs < (n_depths - 2) // 2, open_carry < SB_UNDERFLOW_BITS)

    def body(state):
        pairs, _ = state
        d = 2 * pairs + 2
        scores(d, 0)
        weights(1)
        values(d - 2, 0)
        scores(d + 1, 1)
        weights(0)
        values(d - 1, 1)
        return pairs + 1, min_open_carry(d + 2)

    pairs_done, _ = lax.while_loop(keep_going, body, (jnp.int32(0), min_open_carry(2)))
    scored = 2 * pairs_done + 2
    weights(1)
    values(scored - 2, 0)
    values(scored - 1, 1)

    for h in range(nq):
        o_ref[0, h * blk:(h + 1) * blk, :] = acc_ref[h].T.astype(o_ref.dtype)


def _stick_breaking(qk, vt, batch, seq):
    blk, nq, hd = SB_BLOCK, SB_QUERY_BLOCKS, HEAD_DIM
    tq = nq * blk
    assert seq % tq == 0 and nq % 2 == 0
    n_blk = seq // blk
    sfx = (lax.broadcasted_iota(jnp.int32, (blk, blk), 1) > lax.broadcasted_iota(jnp.int32, (blk, blk), 0)).astype(BF16)
    pipelined = (_nbytes((blk, blk), BF16) + 2 * _nbytes((tq, hd), BF16) + 2 * _nbytes((seq, hd), BF16))
    slots = 2 * nq * (2 * _nbytes((blk, blk), BF16) + _nbytes((blk, blk), F32))
    resident = nq * _nbytes((hd, blk), F32) + slots + nq * 4 * _nbytes((blk, blk), F32)
    return pl.pallas_call(
        _sb_kernel,
        out_shape=jax.ShapeDtypeStruct((batch, seq, SB_HEADS * hd), BF16),
        grid=(batch, SB_HEADS, seq // tq),
        in_specs=[
            pl.BlockSpec((blk, blk), lambda b, h, i: (0, 0)),
            pl.BlockSpec((1, tq, hd), lambda b, h, i: (b, i, h)),
            pl.BlockSpec((1, seq, hd), lambda b, h, i: (b, 0, SB_HEADS + h)),
            pl.BlockSpec((1, n_blk, hd, blk), lambda b, h, i: (b, 0, h, 0)),
        ],
        out_specs=pl.BlockSpec((1, tq, hd), lambda b, h, i: (b, i, h)),
        scratch_shapes=[
            pltpu.VMEM((nq, hd, blk), F32),
            pltpu.VMEM((nq, 1, blk), F32),
            pltpu.VMEM((2, nq, blk, blk), BF16),
            pltpu.VMEM((2, nq, blk, blk), F32),
            pltpu.VMEM((2, nq, blk, blk), BF16),
        ],
        compiler_params=pltpu.CompilerParams(
            dimension_semantics=("arbitrary", "arbitrary", "arbitrary"),
            vmem_limit_bytes=_vmem_limit(pipelined, resident)),
        name="stick_breaking",
    )(sfx, qk, qk, vt)


def _out_proj_kernel(oa_ref, ob_ref, wa_ref, wb_ref, x_ref, o_ref):
    o_ref[...] = x_ref[...] + _dot(oa_ref[...], wa_ref[0]) + _dot(ob_ref[...], wb_ref[0])


def _out_proj(o_a, o_b, w_out, layer, x2):
    n_tok, d = x2.shape
    ka, kb = o_a.shape[1], o_b.shape[1]
    tm = OUT_PROJ_ROWS
    assert n_tok % tm == 0 and ka == kb and w_out.shape[1] == ka + kb
    pipelined = (_nbytes((tm, ka), BF16) + _nbytes((tm, kb), BF16) + _nbytes((ka, d), BF16)
                 + _nbytes((kb, d), BF16) + 2 * _nbytes((tm, d), F32))
    resident = 2 * _nbytes((tm, d), F32)
    return pl.pallas_call(
        _out_proj_kernel,
        out_shape=jax.ShapeDtypeStruct((n_tok, d), F32),
        grid=(n_tok // tm,),
        in_specs=[
            pl.BlockSpec((tm, ka), lambda i: (i, 0)),
            pl.BlockSpec((tm, kb), lambda i: (i, 0)),
            pl.BlockSpec((1, ka, d), lambda i: (layer, 0, 0)),
            pl.BlockSpec((1, kb, d), lambda i: (layer, 1, 0)),
            pl.BlockSpec((tm, d), lambda i: (i, 0)),
        ],
        out_specs=pl.BlockSpec((tm, d), lambda i: (i, 0)),
        compiler_params=pltpu.CompilerParams(
            dimension_semantics=("arbitrary",),
            vmem_limit_bytes=_vmem_limit(pipelined, resident)),
        name="out_proj",
    )(o_a, o_b, w_out, w_out, x2)


def _ffn_kernel(x_ref, g_ref, wg_ref, wu_ref, wd_ref, fg_ref, o_ref, h_ref, *, final_norm):
    j = pl.program_id(1)

    @pl.when(j == 0)
    def _():
        x = x_ref[...]
        h_ref[...] = _rms_norm(x, g_ref[...]).astype(BF16)
        o_ref[...] = x

    h = h_ref[...]
    act = _silu(_dot(h, wg_ref[0])) * _dot(h, wu_ref[0])
    o_ref[...] += _dot(act.astype(BF16), wd_ref[0])

    if final_norm:
        @pl.when(j == pl.num_programs(1) - 1)
        def _():
            o_ref[...] = _rms_norm(o_ref[...], fg_ref[...])


def _ffn(x2, gain, w_gate, w_up, w_down, layer, final_gain, final_norm, name):
    n_tok, d = x2.shape
    hidden = w_gate.shape[2]
    tm, th = FFN_ROWS, FFN_HIDDEN_TILE
    assert n_tok % tm == 0 and hidden % th == 0
    pipelined = (2 * _nbytes((tm, d), F32) + 2 * _nbytes((d, th), BF16) + _nbytes((th, d), BF16)
                 + 2 * _nbytes((1, d), F32))
    resident = _nbytes((tm, d), BF16) + 4 * _nbytes((tm, th), F32) + _nbytes((tm, d), F32)
    return pl.pallas_call(
        functools.partial(_ffn_kernel, final_norm=final_norm),
        out_shape=jax.ShapeDtypeStruct((n_tok, d), F32),
        grid=(n_tok // tm, hidden // th),
        in_specs=[
            pl.BlockSpec((tm, d), lambda i, j: (i, 0)),
            pl.BlockSpec((1, d), lambda i, j: (0, 0)),
            pl.BlockSpec((1, d, th), lambda i, j: (layer, 0, j)),
            pl.BlockSpec((1, d, th), lambda i, j: (layer, 0, j)),
            pl.BlockSpec((1, th, d), lambda i, j: (layer, j, 0)),
            pl.BlockSpec((1, d), lambda i, j: (0, 0)),
        ],
        out_specs=pl.BlockSpec((tm, d), lambda i, j: (i, 0)),
        scratch_shapes=[pltpu.VMEM((tm, d), BF16)],
        compiler_params=pltpu.CompilerParams(
            dimension_semantics=("arbitrary", "arbitrary"),
            vmem_limit_bytes=_vmem_limit(pipelined, resident)),
        name=name,
    )(x2, gain, w_gate, w_up, w_down, final_gain)


def _pool_kernel(x_ref, halo_ref, g_ref, pw_ref, sc_ref, o_ref, hs_ref):
    ts = x_ref.shape[1]
    halo = POOL_HALO
    i = pl.program_id(1)
    gain = g_ref[...]
    x = x_ref[0]
    hs_ref[0:halo, :] = jnp.where(i == 0, 0.0, _rms_norm(halo_ref[0], gain))
    hs_ref[halo:halo + ts, :] = _rms_norm(x, gain)

    pos = i * ts + lax.broadcasted_iota(jnp.int32, (ts, 1), 0)
    group = pw_ref.shape[1]
    for gi, win in enumerate(POOL_WINDOWS):
        cols = slice(gi * group, (gi + 1) * group)
        wsum = hs_ref[:, cols]
        for s in range(win.bit_length() - 1):
            wsum = wsum + pltpu.roll(wsum, 1 << s, axis=0)
        h = hs_ref[halo:halo + ts, cols]
        inv_count = 1.0 / jnp.minimum(pos + 1, win).astype(F32)
        pooled = wsum[halo:] * inv_count - h
        mixed = _dot(pooled.astype(BF16), pw_ref[gi])
        o_ref[0, :, cols] = x[:, cols] + mixed * sc_ref[:, cols]


def _pool_mixer(x3, gain, pool_w, scale):
    batch, seq, d = x3.shape
    ts, halo = POOL_ROWS, POOL_HALO
    n_groups, group = pool_w.shape[0], pool_w.shape[1]
    assert seq % ts == 0 and ts % halo == 0 and max(POOL_WINDOWS) - 1 <= halo
    assert all(w & (w - 1) == 0 for w in POOL_WINDOWS)
    assert n_groups == len(POOL_WINDOWS) and n_groups * group == d
    halo_per_tile = ts // halo
    pipelined = (2 * _nbytes((ts, d), F32) + _nbytes((halo, d), F32) + _nbytes(pool_w.shape, BF16)
                 + 2 * _nbytes((1, d), F32))
    resident = _nbytes((ts + halo, d), F32) + 3 * _nbytes((ts, d), F32)
    return pl.pallas_call(
        _pool_kernel,
        out_shape=jax.ShapeDtypeStruct((batch, seq, d), F32),
        grid=(batch, seq // ts),
        in_specs=[
            pl.BlockSpec((1, ts, d), lambda b, i: (b, i, 0)),
            pl.BlockSpec((1, halo, d), lambda b, i: (b, jnp.maximum(i * halo_per_tile - 1, 0), 0)),
            pl.BlockSpec((1, d), lambda b, i: (0, 0)),
            pl.BlockSpec((n_groups, group, group), lambda b, i: (0, 0, 0)),
            pl.BlockSpec((1, d), lambda b, i: (0, 0)),
        ],
        out_specs=pl.BlockSpec((1, ts, d), lambda b, i: (b, i, 0)),
        scratch_shapes=[pltpu.VMEM((ts + halo, d), F32)],
        compiler_params=pltpu.CompilerParams(
            dimension_semantics=("arbitrary", "arbitrary"),
            vmem_limit_bytes=_vmem_limit(pipelined, resident)),
        name="pool_mixer",
    )(x3, x3, gain, pool_w, scale)


def kernel(x, mix_norm, ffn_norm, final_norm, ab_w_in, lb_logits, hg_out_norm, ab_w_out,
           pool_w, pool_scale, ffn_w_gate, ffn_w_up, ffn_w_down):
    batch, seq, d = x.shape
    n_tok = batch * seq
    hg_width = HG_HEADS * HEAD_DIM
    sb_width = SB_HEADS * HEAD_DIM
    row = lambda a: a.reshape(1, -1)

    x2 = x.reshape(n_tok, d)
    ffn_weights = (ffn_w_gate, ffn_w_up, ffn_w_down)
    proj_a, qk, vt = _in_proj(x2, row(mix_norm[0]), ab_w_in.astype(BF16), 0, batch, seq, SB_BLOCK)
    o_a, *ffn_bf16 = _hgrn(proj_a.reshape(batch, seq, 4 * hg_width), lb_logits, row(hg_out_norm[0]),
                           batch, seq, 0, ffn_weights, (0, 1))
    ffn0_weights, ffn1_weights = ffn_bf16[:3], ffn_bf16[3:]
    o_b = _stick_breaking(qk.reshape(batch, seq, 2 * sb_width), vt, batch, seq)
    x2 = _out_proj(o_a.reshape(n_tok, hg_width), o_b.reshape(n_tok, sb_width), ab_w_out.astype(BF16), 0, x2)
    x2 = _ffn(x2, row(ffn_norm[0]), *ffn0_weights, 0, row(final_norm), False, "ffn0")

    x3 = _pool_mixer(x2.reshape(batch, seq, d), row(mix_norm[1]), pool_w[0].astype(BF16), row(pool_scale[0]))
    x2 = _ffn(x3.reshape(n_tok, d), row(ffn_norm[1]), *ffn1_weights, 0, row(final_norm), True, "ffn1_final_norm")
    return x2.reshape(batch, seq, d)
```

```python
import functools
import math

import jax
import jax.numpy as jnp
from jax import lax
from jax.experimental import pallas as pl
from jax.experimental.pallas import tpu as pltpu

F32 = jnp.float32
BF16 = jnp.bfloat16

RMS_EPS = 1e-6
HEAD_DIM = 128
HG_HEADS = 8
SB_HEADS = 8
POOL_WINDOWS = (2, 4, 8, 16)

V7X_VMEM_BYTES = 64 * 1024 * 1024
V7X_VMEM_USABLE_BYTES = V7X_VMEM_BYTES * 13 // 16
V7X_DEFAULT_SCOPED_VMEM_BYTES = 32 * 1024 * 1024
F32_SUBLANES = 8
LOG2_E = math.log2(math.e)

HG_CHUNK = 128
HG_STEP_TOKENS = 2048
SB_BLOCK = 256
SB_QUERY_BLOCKS = 16
SB_UNDERFLOW_BITS = 160.0
PROJ_ROWS = 1024
PROJ_COLS = 1024
OUT_PROJ_ROWS = 512
FFN_ROWS = 512
FFN_HIDDEN_TILE = 512
POOL_ROWS = 512
CAST_SLAB = 128
POOL_HALO = 16


def _vmem_limit(pipelined_bytes, resident_bytes):
    need = 2 * pipelined_bytes + resident_bytes
    assert need <= V7X_VMEM_USABLE_BYTES, (need, V7X_VMEM_USABLE_BYTES)
    return int(max(need, V7X_DEFAULT_SCOPED_VMEM_BYTES))


def _nbytes(shape, dtype):
    return math.prod(shape) * jnp.dtype(dtype).itemsize


def _rms_norm(x, gain):
    ms = jnp.mean(x * x, axis=-1, keepdims=True)
    return x * lax.rsqrt(ms + RMS_EPS) * gain


def _dot(a, b):
    return jnp.dot(a, b, preferred_element_type=F32)


def _dot_nt(a, b):
    return lax.dot_general(a, b, (((1,), (1,)), ((), ())), preferred_element_type=F32)


def _sigmoid(x):
    return 0.5 * jnp.tanh(0.5 * x) + 0.5


def _silu(x, scale=None):
    half = 0.5 * x
    t = jnp.tanh(half)
    if scale is not None:
        half = half * scale
    return half * t + half


def _in_proj_kernel(x_ref, g_ref, w_ref, oa_ref, oqk_ref, ovt_ref, h_ref, *, n_a_tiles, q_scale, key_block):
    j = pl.program_id(1)

    @pl.when(j == 0)
    def _():
        h_ref[...] = _rms_norm(x_ref[...], g_ref[...]).astype(BF16)

    def tile():
        return _dot(h_ref[...], w_ref[0])

    @pl.when(j < n_a_tiles)
    def _():
        oa_ref[...] = tile()

    @pl.when(j == n_a_tiles)
    def _():
        oqk_ref[...] = (tile() * q_scale).astype(oqk_ref.dtype)

    @pl.when(j == n_a_tiles + 1)
    def _():
        oqk_ref[...] = tile().astype(oqk_ref.dtype)

    @pl.when(j == n_a_tiles + 2)
    def _():
        vt = tile().T.astype(ovt_ref.dtype)
        for kb in range(ovt_ref.shape[1]):
            ovt_ref[0, kb] = vt[:, kb * key_block:(kb + 1) * key_block]


def _in_proj(x2, gain, w_in, layer, batch, seq, key_block):
    n_tok, d = x2.shape
    tm, tn = PROJ_ROWS, PROJ_COLS
    hg_width, sb_width = HG_HEADS * HEAD_DIM, SB_HEADS * HEAD_DIM
    n_a_tiles, n_qk_tiles, n_v_tiles = 4 * hg_width // tn, 2 * sb_width // tn, sb_width // tn
    assert tn == hg_width == sb_width and w_in.shape[2] == (n_a_tiles + n_qk_tiles + n_v_tiles) * tn
    assert seq % tm == 0 and tm % key_block == 0
    tiles_per_seq = seq // tm
    kb_per_tile = tm // key_block
    pipelined = (_nbytes((tm, d), F32) + _nbytes((d, tn), BF16) + _nbytes((tm, tn), F32)
                 + 2 * _nbytes((tm, tn), BF16) + _nbytes((1, d), F32))
    resident = _nbytes((tm, d), BF16) + _nbytes((tm, tn), F32)
    return pl.pallas_call(
        functools.partial(_in_proj_kernel, n_a_tiles=n_a_tiles, q_scale=LOG2_E / math.sqrt(HEAD_DIM),
                          key_block=key_block),
        out_shape=(jax.ShapeDtypeStruct((n_tok, n_a_tiles * tn), F32),
                   jax.ShapeDtypeStruct((n_tok, 2 * tn), BF16),
                   jax.ShapeDtypeStruct((batch, seq // key_block, tn, key_block), BF16)),
        grid=(n_tok // tm, n_a_tiles + n_qk_tiles + n_v_tiles),
        in_specs=[
            pl.BlockSpec((tm, d), lambda i, j: (i, 0)),
            pl.BlockSpec((1, d), lambda i, j: (0, 0)),
            pl.BlockSpec((1, d, tn), lambda i, j: (layer, 0, j)),
        ],
        out_specs=(
            pl.BlockSpec((tm, tn), lambda i, j: (i, jnp.minimum(j, n_a_tiles - 1))),
            pl.BlockSpec((tm, tn), lambda i, j: (i, jnp.clip(j - n_a_tiles, 0, n_qk_tiles - 1))),
            pl.BlockSpec((1, kb_per_tile, tn, key_block),
                         lambda i, j: (i // tiles_per_seq, i % tiles_per_seq, 0, 0)),
        ),
        scratch_shapes=[pltpu.VMEM((tm, d), BF16)],
        compiler_params=pltpu.CompilerParams(
            dimension_semantics=("arbitrary", "arbitrary"),
            vmem_limit_bytes=_vmem_limit(pipelined, resident)),
        name="in_proj",
    )(x2, gain, w_in)


def _cast_rider_specs(weights, layers, grid, step_of):
    w_gate, w_up, w_down = weights
    _, d, hidden = w_gate.shape
    assert w_up.shape == w_gate.shape and w_down.shape[1:] == (hidden, d) and hidden % CAST_SLAB == 0
    n_slabs = max(n for n in range(1, hidden // CAST_SLAB + 1)
                  if (hidden // CAST_SLAB) % n == 0 and n <= math.prod(grid))
    width = hidden // n_slabs

    def slab(*grid_idx):
        return jnp.minimum(step_of(*grid_idx), n_slabs - 1)

    col_out = pl.BlockSpec((1, d, width), lambda *g: (0, 0, slab(*g)))
    row_out = pl.BlockSpec((1, width, d), lambda *g: (0, slab(*g), 0))
    in_specs, out_specs, out_shapes = [], [], []
    for layer in layers:
        col_in = pl.BlockSpec((1, d, width), lambda *g, layer=layer: (layer, 0, slab(*g)))
        row_in = pl.BlockSpec((1, width, d), lambda *g, layer=layer: (layer, slab(*g), 0))
        in_specs += [col_in, col_in, row_in]
        out_specs += [col_out, col_out, row_out]
        out_shapes += [jax.ShapeDtypeStruct((1, d, hidden), BF16), jax.ShapeDtypeStruct((1, d, hidden), BF16),
                       jax.ShapeDtypeStruct((1, hidden, d), BF16)]
    slab_bytes = 3 * len(layers) * (_nbytes((d, width), F32) + _nbytes((d, width), BF16))
    return in_specs, out_specs, out_shapes, n_slabs, slab_bytes


def _cast_rider(step, n_slabs, src_refs, dst_refs):
    @pl.when(step < n_slabs)
    def _():
        for src, dst in zip(src_refs, dst_refs):
            dst[...] = src[...].astype(dst.dtype)


def _split2_bf16(x):
    hi = x.astype(BF16)
    return hi, (x - hi.astype(F32)).astype(BF16)


def _sublane_broadcast(x, group, row):
    rows, lanes = x.shape
    x3 = x.reshape(rows // group, group, lanes)
    return jnp.broadcast_to(x3[:, row:row + 1, :], x3.shape).reshape(rows, lanes)


def _hgrn_kernel(lbl_ref, gn_ref, tri_ref, lvl_ref, qa_ref, fa_ref, ia_ref, ga_ref, *refs, layer_row, n_slabs):
    n_cast = (len(refs) - 2) // 2
    cast_src, o_ref, cast_dst, st_ref = refs[:n_cast], refs[n_cast], refs[n_cast + 1:-1], refs[-1]
    c, ts = HG_CHUNK, HG_STEP_TOKENS
    n_levels = c.bit_length() - 1
    chunks = [slice(j * c, (j + 1) * c) for j in range(ts // c)]

    step = (pl.program_id(0) * pl.num_programs(1) + pl.program_id(1)) * pl.num_programs(2) + pl.program_id(2)
    _cast_rider(step, n_slabs, cast_src, cast_dst)

    @pl.when(pl.program_id(2) == 0)
    def _():
        st_ref[...] = jnp.zeros_like(st_ref)

    logits = lbl_ref[...]
    e = jnp.exp(logits - jnp.max(logits, axis=0, keepdims=True))
    lower = jnp.sum(e[:layer_row + 1], axis=0, keepdims=True) / jnp.sum(e, axis=0, keepdims=True)

    row = lax.broadcasted_iota(jnp.int32, (ts, HEAD_DIM), 0)
    q = _silu(qa_ref[0])
    f = lower + (1.0 - lower) * _sigmoid(fa_ref[0])
    k = 1.0 - f
    v = ia_ref[0]
    v16 = v.astype(BF16)

    tri = tri_ref[...]
    parts = _split2_bf16(jnp.log(f) * LOG2_E)
    g = jnp.concatenate([sum(_dot(tri, p[ch]) for p in parts) for ch in chunks], axis=0)

    level_of = lvl_ref[...]
    att = [jnp.zeros((c, c), F32) for _ in chunks]
    for lvl in range(n_levels):
        half = 1 << lvl
        if half >= F32_SUBLANES:
            shape4 = (ts // (2 * half), 2, half, HEAD_DIM)
            g4, q4, k4 = g.reshape(shape4), q.reshape(shape4), k.reshape(shape4)
            g_mid = g4[:, 0:1, half - 1:half, :]
            x_lo = k4[:, 0:1] * jnp.exp2(g_mid - g4[:, 0:1])
            x_hi = q4[:, 1:2] * jnp.exp2(g4[:, 1:2] - g_mid)
            x = jnp.concatenate([x_lo, x_hi], axis=1).reshape(ts, HEAD_DIM)
        else:
            upper = (row & half) != 0
            if lvl == 0:
                x = jnp.where(upper, q * f, k)
            else:
                g_mid = _sublane_broadcast(g, F32_SUBLANES, half - 1)
                if 2 * half < F32_SUBLANES:
                    second = _sublane_broadcast(g, F32_SUBLANES, 2 * half + half - 1)
                    g_mid = jnp.where((row & (F32_SUBLANES - 1)) < 2 * half, g_mid, second)
                x = jnp.where(upper, q, k) * jnp.exp2(-jnp.abs(g - g_mid))
        x16 = x.astype(BF16)
        for j, ch in enumerate(chunks):
            att[j] = jnp.where(level_of == lvl, _dot_nt(x16[ch], x16[ch]), att[j])
    qk_diag = jnp.sum(q * k, axis=-1, keepdims=True)

    g_last = _sublane_broadcast(g, c, c - 1)
    q_dec = (q * jnp.exp2(g)).astype(BF16)
    k_dec = (k * jnp.exp2(g_last - g)).astype(BF16)
    st = st_ref[...]
    outs = []
    for j, ch in enumerate(chunks):
        a = jnp.where(level_of == n_levels, qk_diag[ch], att[j])
        outs.append(_dot_nt(q_dec[ch], st.astype(BF16)) + _dot(a.astype(BF16), v16[ch]))
        st = st * jnp.exp2(g_last[ch][:1]) + _dot(v[ch].T.astype(BF16), k_dec[ch])
    st_ref[...] = st

    y = _rms_norm(jnp.concatenate(outs, axis=0), gn_ref[...]) * _silu(ga_ref[0])
    o_ref[0] = y.astype(o_ref.dtype)


def _hgrn(proj, lb_logits, out_gain, batch, seq, layer_row, cast_weights, cast_layers):
    ts, c, hd = HG_STEP_TOKENS, HG_CHUNK, HEAD_DIM
    assert seq % ts == 0 and ts % c == 0
    n_rows = lb_logits.shape[0]
    grid = (batch, HG_HEADS, seq // ts)
    cast_in, cast_out, cast_shapes, n_slabs, slab_bytes = _cast_rider_specs(
        cast_weights, cast_layers, grid, lambda b, h, s: (b * grid[1] + h) * grid[2] + s)
    t_idx = lax.broadcasted_iota(jnp.int32, (c, c), 0)
    s_idx = lax.broadcasted_iota(jnp.int32, (c, c), 1)
    tri = (t_idx >= s_idx).astype(BF16)
    level_of = jnp.where(t_idx > s_idx, jnp.iinfo(jnp.int32).bits - 1 - lax.clz(t_idx ^ s_idx),
                         jnp.where(t_idx == s_idx, c.bit_length() - 1, -1)).astype(jnp.int32)

    def head_block(part):
        return pl.BlockSpec((1, ts, hd), lambda b, h, s: (b, s, part * HG_HEADS + h))

    pipelined = (4 * _nbytes((ts, hd), F32) + _nbytes((ts, hd), BF16) + _nbytes((c, c), BF16)
                 + _nbytes((c, c), jnp.int32) + slab_bytes)
    resident = 24 * _nbytes((ts, hd), F32)
    return pl.pallas_call(
        functools.partial(_hgrn_kernel, layer_row=layer_row, n_slabs=n_slabs),
        out_shape=[jax.ShapeDtypeStruct((batch, seq, HG_HEADS * hd), BF16)] + cast_shapes,
        grid=grid,
        in_specs=[
            pl.BlockSpec((n_rows, hd), lambda b, h, s: (0, h)),
            pl.BlockSpec((1, hd), lambda b, h, s: (0, 0)),
            pl.BlockSpec((c, c), lambda b, h, s: (0, 0)),
            pl.BlockSpec((c, c), lambda b, h, s: (0, 0)),
            head_block(0), head_block(1), head_block(2), head_block(3),
        ] + cast_in,
        out_specs=[pl.BlockSpec((1, ts, hd), lambda b, h, s: (b, s, h))] + cast_out,
        scratch_shapes=[pltpu.VMEM((hd, hd), F32)],
        compiler_params=pltpu.CompilerParams(
            dimension_semantics=("arbitrary", "arbitrary", "arbitrary"),
            vmem_limit_bytes=_vmem_limit(pipelined, resident)),
        name="hgrn2",
    )(lb_logits, out_gain, tri, level_of, proj, proj, proj, proj, *(cast_weights * len(cast_layers)))


def _sb_kernel(sfx_ref, q_ref, k_ref, vt_ref, o_ref, acc_ref, carry_ref, sp_ref, lbc_ref, w_ref):
    blk, nq = SB_BLOCK, SB_QUERY_BLOCKS
    first = nq * pl.program_id(2)
    n_depths = first + nq
    causal = (lax.broadcasted_iota(jnp.int32, (blk, blk), 0)
              < lax.broadcasted_iota(jnp.int32, (blk, blk), 1))
    sfx = sfx_ref[...]

    def key_block(h, depth, always_live):
        kb = first + h - depth
        return (kb, None) if always_live else (jnp.maximum(kb, 0), kb >= 0)

    def scores(depth, slot, on_diagonal=False):
        for h in range(nq):
            j, live = key_block(h, depth, on_diagonal)
            ks = k_ref[0, pl.ds(pl.multiple_of(j * blk, blk), blk), :]
            z = _dot_nt(ks, q_ref[0, h * blk:(h + 1) * blk, :])
            sp = jnp.maximum(z, 0.0) + jnp.log2(1.0 + jnp.exp2(-jnp.abs(z)))
            lbc_ref[slot, h] = (z - sp) - carry_ref[h]
            if on_diagonal:
                sp = jnp.where(causal, sp, 0.0)
            sp_ref[slot, h] = sp.astype(BF16)
            total = jnp.sum(sp, axis=0, keepdims=True)
            carry_ref[h] += total if live is None else jnp.where(live, total, 0.0)

    def weights(slot, on_diagonal=False):
        for h in range(nq):
            w = jnp.exp2(lbc_ref[slot, h] - _dot(sfx, sp_ref[slot, h]))
            if on_diagonal:
                w = jnp.where(causal, w, 0.0)
            w_ref[slot, h] = w.astype(BF16)

    def values(depth, slot, on_diagonal=False):
        for h in range(nq):
            j, live = key_block(h, depth, on_diagonal)
            pv = _dot(vt_ref[0, j], w_ref[slot, h])
            acc_ref[h] += pv if live is None else jnp.where(live, pv, 0.0)

    def min_open_carry(next_depth):
        h_idx = lax.broadcasted_iota(jnp.int32, carry_ref.shape, 0)
        return jnp.min(jnp.where(first + h_idx - next_depth >= 0, carry_ref[...], SB_UNDERFLOW_BITS))

    acc_ref[...] = jnp.zeros_like(acc_ref)
    carry_ref[...] = jnp.zeros_like(carry_ref)

    scores(0, 0, on_diagonal=True)
    scores(1, 1)
    weights(0, on_diagonal=True)

    def keep_going(state):
        pairs, open_carry = state
        return jnp.logical_and(pairs < (n_depths - 2) // 2, open_carry < SB_UNDERFLOW_BITS)

    def body(state):
        pairs, _ = state
        d = 2 * pairs + 2
        scores(d, 0)
        weights(1)
        values(d - 2, 0)
        scores(d + 1, 1)
        weights(0)
        values(d - 1, 1)
        return pairs + 1, min_open_carry(d + 2)

    pairs_done, _ = lax.while_loop(keep_going, body, (jnp.int32(0), min_open_carry(2)))
    scored = 2 * pairs_done + 2
    weights(1)
    values(scored - 2, 0)
    values(scored - 1, 1)

    for h in range(nq):
        o_ref[0, h * blk:(h + 1) * blk, :] = acc_ref[h].T.astype(o_ref.dtype)


def _stick_breaking(qk, vt, batch, seq):
    blk, nq, hd = SB_BLOCK, SB_QUERY_BLOCKS, HEAD_DIM
    tq = nq * blk
    assert seq % tq == 0 and nq % 2 == 0
    n_blk = seq // blk
    sfx = (lax.broadcasted_iota(jnp.int32, (blk, blk), 1) > lax.broadcasted_iota(jnp.int32, (blk, blk), 0)).astype(BF16)
    pipelined = (_nbytes((blk, blk), BF16) + 2 * _nbytes((tq, hd), BF16) + 2 * _nbytes((seq, hd), BF16))
    slots = 2 * nq * (2 * _nbytes((blk, blk), BF16) + _nbytes((blk, blk), F32))
    resident = nq * _nbytes((hd, blk), F32) + slots + nq * 4 * _nbytes((blk, blk), F32)
    return pl.pallas_call(
        _sb_kernel,
        out_shape=jax.ShapeDtypeStruct((batch, seq, SB_HEADS * hd), BF16),
        grid=(batch, SB_HEADS, seq // tq),
        in_specs=[
            pl.BlockSpec((blk, blk), lambda b, h, i: (0, 0)),
            pl.BlockSpec((1, tq, hd), lambda b, h, i: (b, i, h)),
            pl.BlockSpec((1, seq, hd), lambda b, h, i: (b, 0, SB_HEADS + h)),
            pl.BlockSpec((1, n_blk, hd, blk), lambda b, h, i: (b, 0, h, 0)),
        ],
        out_specs=pl.BlockSpec((1, tq, hd), lambda b, h, i: (b, i, h)),
        scratch_shapes=[
            pltpu.VMEM((nq, hd, blk), F32),
            pltpu.VMEM((nq, 1, blk), F32),
            pltpu.VMEM((2, nq, blk, blk), BF16),
            pltpu.VMEM((2, nq, blk, blk), F32),
            pltpu.VMEM((2, nq, blk, blk), BF16),
        ],
        compiler_params=pltpu.CompilerParams(
            dimension_semantics=("arbitrary", "arbitrary", "arbitrary"),
            vmem_limit_bytes=_vmem_limit(pipelined, resident)),
        name="stick_breaking",
    )(sfx, qk, qk, vt)


def _out_proj_kernel(oa_ref, ob_ref, wa_ref, wb_ref, x_ref, o_ref):
    o_ref[...] = x_ref[...] + _dot(oa_ref[...], wa_ref[0]) + _dot(ob_ref[...], wb_ref[0])


def _out_proj(o_a, o_b, w_out, layer, x2):
    n_tok, d = x2.shape
    ka, kb = o_a.shape[1], o_b.shape[1]
    tm = OUT_PROJ_ROWS
    assert n_tok % tm == 0 and ka == kb and w_out.shape[1] == ka + kb
    pipelined = (_nbytes((tm, ka), BF16) + _nbytes((tm, kb), BF16) + _nbytes((ka, d), BF16)
                 + _nbytes((kb, d), BF16) + 2 * _nbytes((tm, d), F32))
    resident = 2 * _nbytes((tm, d), F32)
    return pl.pallas_call(
        _out_proj_kernel,
        out_shape=jax.ShapeDtypeStruct((n_tok, d), F32),
        grid=(n_tok // tm,),
        in_specs=[
            pl.BlockSpec((tm, ka), lambda i: (i, 0)),
            pl.BlockSpec((tm, kb), lambda i: (i, 0)),
            pl.BlockSpec((1, ka, d), lambda i: (layer, 0, 0)),
            pl.BlockSpec((1, kb, d), lambda i: (layer, 1, 0)),
            pl.BlockSpec((tm, d), lambda i: (i, 0)),
        ],
        out_specs=pl.BlockSpec((tm, d), lambda i: (i, 0)),
        compiler_params=pltpu.CompilerParams(
            dimension_semantics=("arbitrary",),
            vmem_limit_bytes=_vmem_limit(pipelined, resident)),
        name="out_proj",
    )(o_a, o_b, w_out, w_out, x2)


def _ffn_kernel(x_ref, g_ref, wg_ref, wu_ref, wd_ref, fg_ref, o_ref, h_ref, *, final_norm):
    j = pl.program_id(1)

    @pl.when(j == 0)
    def _():
        x = x_ref[...]
        h_ref[...] = _rms_norm(x, g_ref[...]).astype(BF16)
        o_ref[...] = x

    h = h_ref[...]
    act = _silu(_dot(h, wg_ref[0]), scale=_dot(h, wu_ref[0]))
    o_ref[...] += _dot(act.astype(BF16), wd_ref[0])

    if final_norm:
        @pl.when(j == pl.num_programs(1) - 1)
        def _():
            o_ref[...] = _rms_norm(o_ref[...], fg_ref[...])


def _ffn(x2, gain, w_gate, w_up, w_down, layer, final_gain, final_norm, name):
    n_tok, d = x2.shape
    hidden = w_gate.shape[2]
    tm, th = FFN_ROWS, FFN_HIDDEN_TILE
    assert n_tok % tm == 0 and hidden % th == 0
    pipelined = (2 * _nbytes((tm, d), F32) + 2 * _nbytes((d, th), BF16) + _nbytes((th, d), BF16)
                 + 2 * _nbytes((1, d), F32))
    resident = _nbytes((tm, d), BF16) + 4 * _nbytes((tm, th), F32) + _nbytes((tm, d), F32)
    return pl.pallas_call(
        functools.partial(_ffn_kernel, final_norm=final_norm),
        out_shape=jax.ShapeDtypeStruct((n_tok, d), F32),
        grid=(n_tok // tm, hidden // th),
        in_specs=[
            pl.BlockSpec((tm, d), lambda i, j: (i, 0)),
            pl.BlockSpec((1, d), lambda i, j: (0, 0)),
            pl.BlockSpec((1, d, th), lambda i, j: (layer, 0, j)),
            pl.BlockSpec((1, d, th), lambda i, j: (layer, 0, j)),
            pl.BlockSpec((1, th, d), lambda i, j: (layer, j, 0)),
            pl.BlockSpec((1, d), lambda i, j: (0, 0)),
        ],
        out_specs=pl.BlockSpec((tm, d), lambda i, j: (i, 0)),
        scratch_shapes=[pltpu.VMEM((tm, d), BF16)],
        compiler_params=pltpu.CompilerParams(
            dimension_semantics=("arbitrary", "arbitrary"),
            vmem_limit_bytes=_vmem_limit(pipelined, resident)),
        name=name,
    )(x2, gain, w_gate, w_up, w_down, final_gain)


def _pool_kernel(x_ref, halo_ref, g_ref, pw_ref, sc_ref, o_ref, hs_ref):
    ts = x_ref.shape[1]
    halo = POOL_HALO
    i = pl.program_id(1)
    gain = g_ref[...]
    x = x_ref[0]
    hs_ref[0:halo, :] = jnp.where(i == 0, 0.0, _rms_norm(halo_ref[0], gain))
    hs_ref[halo:halo + ts, :] = _rms_norm(x, gain)

    pos = i * ts + lax.broadcasted_iota(jnp.int32, (ts, 1), 0)
    group = pw_ref.shape[1]
    for gi, win in enumerate(POOL_WINDOWS):
        cols = slice(gi * group, (gi + 1) * group)
        wsum = hs_ref[:, cols]
        for s in range(win.bit_length() - 1):
            wsum = wsum + pltpu.roll(wsum, 1 << s, axis=0)
        h = hs_ref[halo:halo + ts, cols]
        inv_count = 1.0 / jnp.minimum(pos + 1, win).astype(F32)
        pooled = wsum[halo:] * inv_count - h
        mixed = _dot(pooled.astype(BF16), pw_ref[gi])
        o_ref[0, :, cols] = x[:, cols] + mixed * sc_ref[:, cols]


def _pool_mixer(x3, gain, pool_w, scale):
    batch, seq, d = x3.shape
    ts, halo = POOL_ROWS, POOL_HALO
    n_groups, group = pool_w.shape[0], pool_w.shape[1]
    assert seq % ts == 0 and ts % halo == 0 and max(POOL_WINDOWS) - 1 <= halo
    assert all(w & (w - 1) == 0 for w in POOL_WINDOWS)
    assert n_groups == len(POOL_WINDOWS) and n_groups * group == d
    halo_per_tile = ts // halo
    pipelined = (2 * _nbytes((ts, d), F32) + _nbytes((halo, d), F32) + _nbytes(pool_w.shape, BF16)
                 + 2 * _nbytes((1, d), F32))
    resident = _nbytes((ts + halo, d), F32) + 3 * _nbytes((ts, d), F32)
    return pl.pallas_call(
        _pool_kernel,
        out_shape=jax.ShapeDtypeStruct((batch, seq, d), F32),
        grid=(batch, seq // ts),
        in_specs=[
            pl.BlockSpec((1, ts, d), lambda b, i: (b, i, 0)),
            pl.BlockSpec((1, halo, d), lambda b, i: (b, jnp.maximum(i * halo_per_tile - 1, 0), 0)),
            pl.BlockSpec((1, d), lambda b, i: (0, 0)),
            pl.BlockSpec((n_groups, group, group), lambda b, i: (0, 0, 0)),
            pl.BlockSpec((1, d), lambda b, i: (0, 0)),
        ],
        out_specs=pl.BlockSpec((1, ts, d), lambda b, i: (b, i, 0)),
        scratch_shapes=[pltpu.VMEM((ts + halo, d), F32)],
        compiler_params=pltpu.CompilerParams(
            dimension_semantics=("arbitrary", "arbitrary"),
            vmem_limit_bytes=_vmem_limit(pipelined, resident)),
        name="pool_mixer",
    )(x3, x3, gain, pool_w, scale)


def kernel(x, mix_norm, ffn_norm, final_norm, ab_w_in, lb_logits, hg_out_norm, ab_w_out,
           pool_w, pool_scale, ffn_w_gate, ffn_w_up, ffn_w_down):
    batch, seq, d = x.shape
    n_tok = batch * seq
    hg_width = HG_HEADS * HEAD_DIM
    sb_width = SB_HEADS * HEAD_DIM
    row = lambda a: a.reshape(1, -1)

    x2 = x.reshape(n_tok, d)
    ffn_weights = (ffn_w_gate, ffn_w_up, ffn_w_down)
    proj_a, qk, vt = _in_proj(x2, row(mix_norm[0]), ab_w_in.astype(BF16), 0, batch, seq, SB_BLOCK)
    o_a, *ffn_bf16 = _hgrn(proj_a.reshape(batch, seq, 4 * hg_width), lb_logits, row(hg_out_norm[0]),
                           batch, seq, 0, ffn_weights, (0, 1))
    ffn0_weights, ffn1_weights = ffn_bf16[:3], ffn_bf16[3:]
    o_b = _stick_breaking(qk.reshape(batch, seq, 2 * sb_width), vt, batch, seq)
    x2 = _out_proj(o_a.reshape(n_tok, hg_width), o_b.reshape(n_tok, sb_width), ab_w_out.astype(BF16), 0, x2)
    x2 = _ffn(x2, row(ffn_norm[0]), *ffn0_weights, 0, row(final_norm), False, "ffn0")

    x3 = _pool_mixer(x2.reshape(batch, seq, d), row(mix_norm[1]), pool_w[0].astype(BF16), row(pool_scale[0]))
    x2 = _ffn(x3.reshape(n_tok, d), row(ffn_norm[1]), *ffn1_weights, 0, row(final_norm), True, "ffn1_final_norm")
    return x2.reshape(batch, seq, d)
```

```python
import functools
import math

import jax
import jax.numpy as jnp
from jax import lax
from jax.experimental import pallas as pl
from jax.experimental.pallas import tpu as pltpu

F32 = jnp.float32
BF16 = jnp.bfloat16

RMS_EPS = 1e-6
HEAD_DIM = 128
HG_HEADS = 8
SB_HEADS = 8
POOL_WINDOWS = (2, 4, 8, 16)

V7X_VMEM_BYTES = 64 * 1024 * 1024
V7X_VMEM_USABLE_BYTES = V7X_VMEM_BYTES * 13 // 16
V7X_DEFAULT_SCOPED_VMEM_BYTES = 32 * 1024 * 1024
F32_SUBLANES = 8
LOG2_E = math.log2(math.e)

HG_CHUNK = 128
HG_STEP_TOKENS = 2048
SB_BLOCK = 256
SB_QUERY_BLOCKS = 16
SB_UNDERFLOW_BITS = 160.0
PROJ_ROWS = 1024
PROJ_COLS = 1024
OUT_PROJ_ROWS = 512
FFN_ROWS = 512
FFN_HIDDEN_TILE = 512
POOL_ROWS = 512
CAST_SLAB = 128
POOL_HALO = 16


def _vmem_limit(pipelined_bytes, resident_bytes):
    need = 2 * pipelined_bytes + resident_bytes
    assert need <= V7X_VMEM_USABLE_BYTES, (need, V7X_VMEM_USABLE_BYTES)
    return int(max(need, V7X_DEFAULT_SCOPED_VMEM_BYTES))


def _nbytes(shape, dtype):
    return math.prod(shape) * jnp.dtype(dtype).itemsize


def _rms_norm(x, gain):
    ms = jnp.mean(x * x, axis=-1, keepdims=True)
    return x * lax.rsqrt(ms + RMS_EPS) * gain


def _dot(a, b):
    return jnp.dot(a, b, preferred_element_type=F32)


def _dot_nt(a, b):
    return lax.dot_general(a, b, (((1,), (1,)), ((), ())), preferred_element_type=F32)


def _sigmoid(x):
    return 0.5 * jnp.tanh(0.5 * x) + 0.5


def _silu(x, scale=None):
    half = 0.5 * x
    t = jnp.tanh(half)
    if scale is not None:
        half = half * scale
    return half * t + half


def _in_proj_kernel(x_ref, g_ref, w_ref, oa_ref, oqk_ref, ovt_ref, h_ref, *, n_a_tiles, q_scale, key_block):
    j = pl.program_id(1)

    @pl.when(j == 0)
    def _():
        h_ref[...] = _rms_norm(x_ref[...], g_ref[...]).astype(BF16)

    def tile():
        return _dot(h_ref[...], w_ref[0])

    @pl.when(j < n_a_tiles)
    def _():
        oa_ref[...] = tile()

    @pl.when(j == n_a_tiles)
    def _():
        oqk_ref[...] = (tile() * q_scale).astype(oqk_ref.dtype)

    @pl.when(j == n_a_tiles + 1)
    def _():
        oqk_ref[...] = tile().astype(oqk_ref.dtype)

    @pl.when(j == n_a_tiles + 2)
    def _():
        vt = tile().T.astype(ovt_ref.dtype)
        for kb in range(ovt_ref.shape[1]):
            ovt_ref[0, kb] = vt[:, kb * key_block:(kb + 1) * key_block]


def _in_proj(x2, gain, w_in, layer, batch, seq, key_block):
    n_tok, d = x2.shape
    tm, tn = PROJ_ROWS, PROJ_COLS
    hg_width, sb_width = HG_HEADS * HEAD_DIM, SB_HEADS * HEAD_DIM
    n_a_tiles, n_qk_tiles, n_v_tiles = 4 * hg_width // tn, 2 * sb_width // tn, sb_width // tn
    assert tn == hg_width == sb_width and w_in.shape[2] == (n_a_tiles + n_qk_tiles + n_v_tiles) * tn
    assert seq % tm == 0 and tm % key_block == 0
    tiles_per_seq = seq // tm
    kb_per_tile = tm // key_block
    pipelined = (_nbytes((tm, d), F32) + _nbytes((d, tn), BF16) + _nbytes((tm, tn), F32)
                 + 2 * _nbytes((tm, tn), BF16) + _nbytes((1, d), F32))
    resident = _nbytes((tm, d), BF16) + _nbytes((tm, tn), F32)
    return pl.pallas_call(
        functools.partial(_in_proj_kernel, n_a_tiles=n_a_tiles, q_scale=LOG2_E / math.sqrt(HEAD_DIM),
                          key_block=key_block),
        out_shape=(jax.ShapeDtypeStruct((n_tok, n_a_tiles * tn), F32),
                   jax.ShapeDtypeStruct((n_tok, 2 * tn), BF16),
                   jax.ShapeDtypeStruct((batch, seq // key_block, tn, key_block), BF16)),
        grid=(n_tok // tm, n_a_tiles + n_qk_tiles + n_v_tiles),
        in_specs=[
            pl.BlockSpec((tm, d), lambda i, j: (i, 0)),
            pl.BlockSpec((1, d), lambda i, j: (0, 0)),
            pl.BlockSpec((1, d, tn), lambda i, j: (layer, 0, j)),
        ],
        out_specs=(
            pl.BlockSpec((tm, tn), lambda i, j: (i, jnp.minimum(j, n_a_tiles - 1))),
            pl.BlockSpec((tm, tn), lambda i, j: (i, jnp.clip(j - n_a_tiles, 0, n_qk_tiles - 1))),
            pl.BlockSpec((1, kb_per_tile, tn, key_block),
                         lambda i, j: (i // tiles_per_seq, i % tiles_per_seq, 0, 0)),
        ),
        scratch_shapes=[pltpu.VMEM((tm, d), BF16)],
        compiler_params=pltpu.CompilerParams(
            dimension_semantics=("arbitrary", "arbitrary"),
            vmem_limit_bytes=_vmem_limit(pipelined, resident)),
        name="in_proj",
    )(x2, gain, w_in)


def _cast_rider_specs(weights, layers, grid, step_of):
    w_gate, w_up, w_down = weights
    _, d, hidden = w_gate.shape
    assert w_up.shape == w_gate.shape and w_down.shape[1:] == (hidden, d) and hidden % CAST_SLAB == 0
    n_slabs = max(n for n in range(1, hidden // CAST_SLAB + 1)
                  if (hidden // CAST_SLAB) % n == 0 and n <= math.prod(grid))
    width = hidden // n_slabs

    def slab(*grid_idx):
        return jnp.minimum(step_of(*grid_idx), n_slabs - 1)

    col_out = pl.BlockSpec((1, d, width), lambda *g: (0, 0, slab(*g)))
    row_out = pl.BlockSpec((1, width, d), lambda *g: (0, slab(*g), 0))
    in_specs, out_specs, out_shapes = [], [], []
    for layer in layers:
        col_in = pl.BlockSpec((1, d, width), lambda *g, layer=layer: (layer, 0, slab(*g)))
        row_in = pl.BlockSpec((1, width, d), lambda *g, layer=layer: (layer, slab(*g), 0))
        in_specs += [col_in, col_in, row_in]
        out_specs += [col_out, col_out, row_out]
        out_shapes += [jax.ShapeDtypeStruct((1, d, hidden), BF16), jax.ShapeDtypeStruct((1, d, hidden), BF16),
                       jax.ShapeDtypeStruct((1, hidden, d), BF16)]
    slab_bytes = 3 * len(layers) * (_nbytes((d, width), F32) + _nbytes((d, width), BF16))
    return in_specs, out_specs, out_shapes, n_slabs, slab_bytes


def _cast_rider(step, n_slabs, src_refs, dst_refs):
    @pl.when(step < n_slabs)
    def _():
        for src, dst in zip(src_refs, dst_refs):
            dst[...] = src[...].astype(dst.dtype)


def _split2_bf16(x):
    hi = x.astype(BF16)
    return hi, (x - hi.astype(F32)).astype(BF16)


def _sublane_broadcast(x, group, row):
    rows, lanes = x.shape
    x3 = x.reshape(rows // group, group, lanes)
    return jnp.broadcast_to(x3[:, row:row + 1, :], x3.shape).reshape(rows, lanes)


def _hgrn_kernel(lbl_ref, gn_ref, tri_ref, lvl_ref, qa_ref, fa_ref, ia_ref, ga_ref, *refs, layer_row, n_slabs):
    n_cast = (len(refs) - 2) // 2
    cast_src, o_ref, cast_dst, st_ref = refs[:n_cast], refs[n_cast], refs[n_cast + 1:-1], refs[-1]
    c, ts = HG_CHUNK, HG_STEP_TOKENS
    n_levels = c.bit_length() - 1
    chunks = [slice(j * c, (j + 1) * c) for j in range(ts // c)]

    step = (pl.program_id(0) * pl.num_programs(1) + pl.program_id(1)) * pl.num_programs(2) + pl.program_id(2)
    _cast_rider(step, n_slabs, cast_src, cast_dst)

    @pl.when(pl.program_id(2) == 0)
    def _():
        st_ref[...] = jnp.zeros_like(st_ref)

    logits = lbl_ref[...]
    e = jnp.exp(logits - jnp.max(logits, axis=0, keepdims=True))
    lower = jnp.sum(e[:layer_row + 1], axis=0, keepdims=True) / jnp.sum(e, axis=0, keepdims=True)

    row = lax.broadcasted_iota(jnp.int32, (ts, HEAD_DIM), 0)
    q = _silu(qa_ref[0])
    f = lower + (1.0 - lower) * _sigmoid(fa_ref[0])
    k = 1.0 - f
    v = ia_ref[0]
    v16 = v.astype(BF16)

    tri = tri_ref[...]
    parts = _split2_bf16(jnp.log(f) * LOG2_E)
    g = jnp.concatenate([sum(_dot(tri, p[ch]) for p in parts) for ch in chunks], axis=0)

    level_of = lvl_ref[...]
    att = [jnp.zeros((c, c), F32) for _ in chunks]
    for lvl in range(n_levels):
        half = 1 << lvl
        if half >= F32_SUBLANES:
            shape4 = (ts // (2 * half), 2, half, HEAD_DIM)
            g4, q4, k4 = g.reshape(shape4), q.reshape(shape4), k.reshape(shape4)
            g_mid = g4[:, 0:1, half - 1:half, :]
            x_lo = k4[:, 0:1] * jnp.exp2(g_mid - g4[:, 0:1])
            x_hi = q4[:, 1:2] * jnp.exp2(g4[:, 1:2] - g_mid)
            x = jnp.concatenate([x_lo, x_hi], axis=1).reshape(ts, HEAD_DIM)
        else:
            upper = (row & half) != 0
            if lvl == 0:
                x = jnp.where(upper, q * f, k)
            else:
                g_mid = _sublane_broadcast(g, F32_SUBLANES, half - 1)
                if 2 * half < F32_SUBLANES:
                    second = _sublane_broadcast(g, F32_SUBLANES, 2 * half + half - 1)
                    g_mid = jnp.where((row & (F32_SUBLANES - 1)) < 2 * half, g_mid, second)
                x = jnp.where(upper, q, k) * jnp.exp2(-jnp.abs(g - g_mid))
        x16 = x.astype(BF16)
        for j, ch in enumerate(chunks):
            att[j] = jnp.where(level_of == lvl, _dot_nt(x16[ch], x16[ch]), att[j])
    qk_diag = jnp.sum(q * k, axis=-1, keepdims=True)

    g_last = _sublane_broadcast(g, c, c - 1)
    q_dec = (q * jnp.exp2(g)).astype(BF16)
    k_dec = (k * jnp.exp2(g_last - g)).astype(BF16)
    st = st_ref[...]
    outs = []
    for j, ch in enumerate(chunks):
        a = jnp.where(level_of == n_levels, qk_diag[ch], att[j])
        outs.append(_dot_nt(q_dec[ch], st.astype(BF16)) + _dot(a.astype(BF16), v16[ch]))
        st = st * jnp.exp2(g_last[ch][:1]) + _dot(v[ch].T.astype(BF16), k_dec[ch])
    st_ref[...] = st

    y = _rms_norm(jnp.concatenate(outs, axis=0), gn_ref[...]) * _silu(ga_ref[0])
    o_ref[0] = y.astype(o_ref.dtype)


def _hgrn(proj, lb_logits, out_gain, batch, seq, layer_row, cast_weights, cast_layers):
    ts, c, hd = HG_STEP_TOKENS, HG_CHUNK, HEAD_DIM
    assert seq % ts == 0 and ts % c == 0
    n_rows = lb_logits.shape[0]
    grid = (batch, HG_HEADS, seq // ts)
    cast_in, cast_out, cast_shapes, n_slabs, slab_bytes = _cast_rider_specs(
        cast_weights, cast_layers, grid, lambda b, h, s: (b * grid[1] + h) * grid[2] + s)
    t_idx = lax.broadcasted_iota(jnp.int32, (c, c), 0)
    s_idx = lax.broadcasted_iota(jnp.int32, (c, c), 1)
    tri = (t_idx >= s_idx).astype(BF16)
    level_of = jnp.where(t_idx > s_idx, jnp.iinfo(jnp.int32).bits - 1 - lax.clz(t_idx ^ s_idx),
                         jnp.where(t_idx == s_idx, c.bit_length() - 1, -1)).astype(jnp.int32)

    def head_block(part):
        return pl.BlockSpec((1, ts, hd), lambda b, h, s: (b, s, part * HG_HEADS + h))

    pipelined = (4 * _nbytes((ts, hd), F32) + _nbytes((ts, hd), BF16) + _nbytes((c, c), BF16)
                 + _nbytes((c, c), jnp.int32) + slab_bytes)
    resident = 24 * _nbytes((ts, hd), F32)
    return pl.pallas_call(
        functools.partial(_hgrn_kernel, layer_row=layer_row, n_slabs=n_slabs),
        out_shape=[jax.ShapeDtypeStruct((batch, seq, HG_HEADS * hd), BF16)] + cast_shapes,
        grid=grid,
        in_specs=[
            pl.BlockSpec((n_rows, hd), lambda b, h, s: (0, h)),
            pl.BlockSpec((1, hd), lambda b, h, s: (0, 0)),
            pl.BlockSpec((c, c), lambda b, h, s: (0, 0)),
            pl.BlockSpec((c, c), lambda b, h, s: (0, 0)),
            head_block(0), head_block(1), head_block(2), head_block(3),
        ] + cast_in,
        out_specs=[pl.BlockSpec((1, ts, hd), lambda b, h, s: (b, s, h))] + cast_out,
        scratch_shapes=[pltpu.VMEM((hd, hd), F32)],
        compiler_params=pltpu.CompilerParams(
            dimension_semantics=("arbitrary", "arbitrary", "arbitrary"),
            vmem_limit_bytes=_vmem_limit(pipelined, resident)),
        name="hgrn2",
    )(lb_logits, out_gain, tri, level_of, proj, proj, proj, proj, *(cast_weights * len(cast_layers)))


def _sb_kernel(sfx_ref, q_ref, k_ref, vt_ref, o_ref, acc_ref, carry_ref, sp_ref, lbc_ref, w_ref):
    blk, nq = SB_BLOCK, SB_QUERY_BLOCKS
    first = nq * pl.program_id(2)
    n_depths = first + nq
    causal = (lax.broadcasted_iota(jnp.int32, (blk, blk), 0)
              < lax.broadcasted_iota(jnp.int32, (blk, blk), 1))
    sfx = sfx_ref[...]

    def key_block(h, depth, always_live):
        kb = first + h - depth
        return (kb, None) if always_live else (jnp.maximum(kb, 0), kb >= 0)

    def scores(depth, slot, on_diagonal=False):
        for h in range(nq):
            j, live = key_block(h, depth, on_diagonal)
            ks = k_ref[0, pl.ds(pl.multiple_of(j * blk, blk), blk), :]
            z = _dot_nt(ks, q_ref[0, h * blk:(h + 1) * blk, :])
            sp = jnp.maximum(z, 0.0) + jnp.log2(1.0 + jnp.exp2(-jnp.abs(z)))
            lbc_ref[slot, h] = (z - sp) - carry_ref[h]
            if on_diagonal:
                sp = jnp.where(causal, sp, 0.0)
            sp_ref[slot, h] = sp.astype(BF16)
            total = jnp.sum(sp, axis=0, keepdims=True)
            carry_ref[h] += total if live is None else jnp.where(live, total, 0.0)

    def weights(slot, on_diagonal=False):
        for h in range(nq):
            w = jnp.exp2(lbc_ref[slot, h] - _dot(sfx, sp_ref[slot, h]))
            if on_diagonal:
                w = jnp.where(causal, w, 0.0)
            w_ref[slot, h] = w.astype(BF16)

    def values(depth, slot, on_diagonal=False):
        for h in range(nq):
            j, live = key_block(h, depth, on_diagonal)
            pv = _dot(vt_ref[0, j], w_ref[slot, h])
            acc_ref[h] += pv if live is None else jnp.where(live, pv, 0.0)

    def min_open_carry(next_depth):
        h_idx = lax.broadcasted_iota(jnp.int32, carry_ref.shape, 0)
        return jnp.min(jnp.where(first + h_idx - next_depth >= 0, carry_ref[...], SB_UNDERFLOW_BITS))

    acc_ref[...] = jnp.zeros_like(acc_ref)
    carry_ref[...] = jnp.zeros_like(carry_ref)

    scores(0, 0, on_diagonal=True)
    scores(1, 1)
    weights(0, on_diagonal=True)

    def keep_going(state):
        pairs, open_carry = state
        return jnp.logical_and(pairs < (n_depths - 2) // 2, open_carry < SB_UNDERFLOW_BITS)

    def body(state):
        pairs, _ = state
        d = 2 * pairs + 2
        scores(d, 0)
        weights(1)
        values(d - 2, 0)
        scores(d + 1, 1)
        weights(0)
        values(d - 1, 1)
        return pairs + 1, min_open_carry(d + 2)

    pairs_done, _ = lax.while_loop(keep_going, body, (jnp.int32(0), min_open_carry(2)))
    scored = 2 * pairs_done + 2
    weights(1)
    values(scored - 2, 0)
    values(scored - 1, 1)

    for h in range(nq):
        o_ref[0, h * blk:(h + 1) * blk, :] = acc_ref[h].T.astype(o_ref.dtype)


def _stick_breaking(qk, vt, batch, seq):
    blk, nq, hd = SB_BLOCK, SB_QUERY_BLOCKS, HEAD_DIM
    tq = nq * blk
    assert seq % tq == 0 and nq % 2 == 0
    n_blk = seq // blk
    sfx = (lax.broadcasted_iota(jnp.int32, (blk, blk), 1) > lax.broadcasted_iota(jnp.int32, (blk, blk), 0)).astype(BF16)
    pipelined = (_nbytes((blk, blk), BF16) + 2 * _nbytes((tq, hd), BF16) + 2 * _nbytes((seq, hd), BF16))
    slots = 2 * nq * (2 * _nbytes((blk, blk), BF16) + _nbytes((blk, blk), F32))
    resident = nq * _nbytes((hd, blk), F32) + slots + nq * 4 * _nbytes((blk, blk), F32)
    return pl.pallas_call(
        _sb_kernel,
        out_shape=jax.ShapeDtypeStruct((batch, seq, SB_HEADS * hd), BF16),
        grid=(batch, SB_HEADS, seq // tq),
        in_specs=[
            pl.BlockSpec((blk, blk), lambda b, h, i: (0, 0)),
            pl.BlockSpec((1, tq, hd), lambda b, h, i: (b, i, h)),
            pl.BlockSpec((1, seq, hd), lambda b, h, i: (b, 0, SB_HEADS + h)),
            pl.BlockSpec((1, n_blk, hd, blk), lambda b, h, i: (b, 0, h, 0)),
        ],
        out_specs=pl.BlockSpec((1, tq, hd), lambda b, h, i: (b, i, h)),
        scratch_shapes=[
            pltpu.VMEM((nq, hd, blk), F32),
            pltpu.VMEM((nq, 1, blk), F32),
            pltpu.VMEM((2, nq, blk, blk), BF16),
            pltpu.VMEM((2, nq, blk, blk), F32),
            pltpu.VMEM((2, nq, blk, blk), BF16),
        ],
        compiler_params=pltpu.CompilerParams(
            dimension_semantics=("arbitrary", "arbitrary", "arbitrary"),
            vmem_limit_bytes=_vmem_limit(pipelined, resident)),
        name="stick_breaking",
    )(sfx, qk, qk, vt)


def _out_proj_kernel(oa_ref, ob_ref, wa_ref, wb_ref, x_ref, o_ref):
    o_ref[...] = x_ref[...] + _dot(oa_ref[...], wa_ref[0]) + _dot(ob_ref[...], wb_ref[0])


def _out_proj(o_a, o_b, w_out, layer, x2):
    n_tok, d = x2.shape
    ka, kb = o_a.shape[1], o_b.shape[1]
    tm = OUT_PROJ_ROWS
    assert n_tok % tm == 0 and ka == kb and w_out.shape[1] == ka + kb
    pipelined = (_nbytes((tm, ka), BF16) + _nbytes((tm, kb), BF16) + _nbytes((ka, d), BF16)
                 + _nbytes((kb, d), BF16) + 2 * _nbytes((tm, d), F32))
    resident = 2 * _nbytes((tm, d), F32)
    return pl.pallas_call(
        _out_proj_kernel,
        out_shape=jax.ShapeDtypeStruct((n_tok, d), F32),
        grid=(n_tok // tm,),
        in_specs=[
            pl.BlockSpec((tm, ka), lambda i: (i, 0)),
            pl.BlockSpec((tm, kb), lambda i: (i, 0)),
            pl.BlockSpec((1, ka, d), lambda i: (layer, 0, 0)),
            pl.BlockSpec((1, kb, d), lambda i: (layer, 1, 0)),
            pl.BlockSpec((tm, d), lambda i: (i, 0)),
        ],
        out_specs=pl.BlockSpec((tm, d), lambda i: (i, 0)),
        compiler_params=pltpu.CompilerParams(
            dimension_semantics=("arbitrary",),
            vmem_limit_bytes=_vmem_limit(pipelined, resident)),
        name="out_proj",
    )(o_a, o_b, w_out, w_out, x2)


def _ffn_kernel(x_ref, xn_ref, g_ref, wg_ref, wu_ref, wd_ref, fg_ref, o_ref, h_ref, *, final_norm):
    i, j = pl.program_id(0), pl.program_id(1)
    last = pl.num_programs(1) - 1
    cur = i % 2

    @pl.when(j == 0)
    def _():
        o_ref[...] = x_ref[...]

    @pl.when(jnp.logical_and(i == 0, j == 0))
    def _():
        h_ref[0] = _rms_norm(x_ref[...], g_ref[...]).astype(BF16)

    def hidden_tile():
        h = h_ref[cur]
        act = _silu(_dot(h, wg_ref[0]), scale=_dot(h, wu_ref[0]))
        o_ref[...] += _dot(act.astype(BF16), wd_ref[0])

    @pl.when(j < last)
    def _():
        hidden_tile()

    @pl.when(j == last)
    def _():
        hidden_tile()
        h_ref[1 - cur] = _rms_norm(xn_ref[...], g_ref[...]).astype(BF16)
        if final_norm:
            o_ref[...] = _rms_norm(o_ref[...], fg_ref[...])


def _ffn(x2, gain, w_gate, w_up, w_down, layer, final_gain, final_norm, name):
    n_tok, d = x2.shape
    hidden = w_gate.shape[2]
    tm, th = FFN_ROWS, FFN_HIDDEN_TILE
    assert n_tok % tm == 0 and hidden % th == 0
    n_rows, n_hidden = n_tok // tm, hidden // th
    pipelined = (3 * _nbytes((tm, d), F32) + 2 * _nbytes((d, th), BF16) + _nbytes((th, d), BF16)
                 + 2 * _nbytes((1, d), F32))
    resident = 2 * _nbytes((tm, d), BF16) + 4 * _nbytes((tm, th), F32) + _nbytes((tm, d), F32)

    def next_rows(i, j):
        return jnp.minimum(jnp.where(j == n_hidden - 1, i + 1, i), n_rows - 1), 0

    return pl.pallas_call(
        functools.partial(_ffn_kernel, final_norm=final_norm),
        out_shape=jax.ShapeDtypeStruct((n_tok, d), F32),
        grid=(n_rows, n_hidden),
        in_specs=[
            pl.BlockSpec((tm, d), lambda i, j: (i, 0)),
            pl.BlockSpec((tm, d), next_rows),
            pl.BlockSpec((1, d), lambda i, j: (0, 0)),
            pl.BlockSpec((1, d, th), lambda i, j: (layer, 0, j)),
            pl.BlockSpec((1, d, th), lambda i, j: (layer, 0, j)),
            pl.BlockSpec((1, th, d), lambda i, j: (layer, j, 0)),
            pl.BlockSpec((1, d), lambda i, j: (0, 0)),
        ],
        out_specs=pl.BlockSpec((tm, d), lambda i, j: (i, 0)),
        scratch_shapes=[pltpu.VMEM((2, tm, d), BF16)],
        compiler_params=pltpu.CompilerParams(
            dimension_semantics=("arbitrary", "arbitrary"),
            vmem_limit_bytes=_vmem_limit(pipelined, resident)),
        name=name,
    )(x2, x2, gain, w_gate, w_up, w_down, final_gain)


def _pool_kernel(x_ref, halo_ref, g_ref, pw_ref, sc_ref, o_ref, hs_ref):
    ts = x_ref.shape[1]
    halo = POOL_HALO
    i = pl.program_id(1)
    gain = g_ref[...]
    x = x_ref[0]
    hs_ref[0:halo, :] = jnp.where(i == 0, 0.0, _rms_norm(halo_ref[0], gain))
    hs_ref[halo:halo + ts, :] = _rms_norm(x, gain)

    pos = i * ts + lax.broadcasted_iota(jnp.int32, (ts, 1), 0)
    group = pw_ref.shape[1]
    for gi, win in enumerate(POOL_WINDOWS):
        cols = slice(gi * group, (gi + 1) * group)
        wsum = hs_ref[:, cols]
        for s in range(win.bit_length() - 1):
            wsum = wsum + pltpu.roll(wsum, 1 << s, axis=0)
        h = hs_ref[halo:halo + ts, cols]
        inv_count = 1.0 / jnp.minimum(pos + 1, win).astype(F32)
        pooled = wsum[halo:] * inv_count - h
        mixed = _dot(pooled.astype(BF16), pw_ref[gi])
        o_ref[0, :, cols] = x[:, cols] + mixed * sc_ref[:, cols]


def _pool_mixer(x3, gain, pool_w, scale):
    batch, seq, d = x3.shape
    ts, halo = POOL_ROWS, POOL_HALO
    n_groups, group = pool_w.shape[0], pool_w.shape[1]
    assert seq % ts == 0 and ts % halo == 0 and max(POOL_WINDOWS) - 1 <= halo
    assert all(w & (w - 1) == 0 for w in POOL_WINDOWS)
    assert n_groups == len(POOL_WINDOWS) and n_groups * group == d
    halo_per_tile = ts // halo
    pipelined = (2 * _nbytes((ts, d), F32) + _nbytes((halo, d), F32) + _nbytes(pool_w.shape, BF16)
                 + 2 * _nbytes((1, d), F32))
    resident = _nbytes((ts + halo, d), F32) + 3 * _nbytes((ts, d), F32)
    return pl.pallas_call(
        _pool_kernel,
        out_shape=jax.ShapeDtypeStruct((batch, seq, d), F32),
        grid=(batch, seq // ts),
        in_specs=[
            pl.BlockSpec((1, ts, d), lambda b, i: (b, i, 0)),
            pl.BlockSpec((1, halo, d), lambda b, i: (b, jnp.maximum(i * halo_per_tile - 1, 0), 0)),
            pl.BlockSpec((1, d), lambda b, i: (0, 0)),
            pl.BlockSpec((n_groups, group, group), lambda b, i: (0, 0, 0)),
            pl.BlockSpec((1, d), lambda b, i: (0, 0)),
        ],
        out_specs=pl.BlockSpec((1, ts, d), lambda b, i: (b, i, 0)),
        scratch_shapes=[pltpu.VMEM((ts + halo, d), F32)],
        compiler_params=pltpu.CompilerParams(
            dimension_semantics=("arbitrary", "arbitrary"),
            vmem_limit_bytes=_vmem_limit(pipelined, resident)),
        name="pool_mixer",
    )(x3, x3, gain, pool_w, scale)


def kernel(x, mix_norm, ffn_norm, final_norm, ab_w_in, lb_logits, hg_out_norm, ab_w_out,
           pool_w, pool_scale, ffn_w_gate, ffn_w_up, ffn_w_down):
    batch, seq, d = x.shape
    n_tok = batch * seq
    hg_width = HG_HEADS * HEAD_DIM
    sb_width = SB_HEADS * HEAD_DIM
    row = lambda a: a.reshape(1, -1)

    x2 = x.reshape(n_tok, d)
    ffn_weights = (ffn_w_gate, ffn_w_up, ffn_w_down)
    proj_a, qk, vt = _in_proj(x2, row(mix_norm[0]), ab_w_in.astype(BF16), 0, batch, seq, SB_BLOCK)
    o_a, *ffn_bf16 = _hgrn(proj_a.reshape(batch, seq, 4 * hg_width), lb_logits, row(hg_out_norm[0]),
                           batch, seq, 0, ffn_weights, (0, 1))
    ffn0_weights, ffn1_weights = ffn_bf16[:3], ffn_bf16[3:]
    o_b = _stick_breaking(qk.reshape(batch, seq, 2 * sb_width), vt, batch, seq)
    x2 = _out_proj(o_a.reshape(n_tok, hg_width), o_b.reshape(n_tok, sb_width), ab_w_out.astype(BF16), 0, x2)
    x2 = _ffn(x2, row(ffn_norm[0]), *ffn0_weights, 0, row(final_norm), False, "ffn0")

    x3 = _pool_mixer(x2.reshape(batch, seq, d), row(mix_norm[1]), pool_w[0].astype(BF16), row(pool_scale[0]))
    x2 = _ffn(x3.reshape(n_tok, d), row(ffn_norm[1]), *ffn1_weights, 0, row(final_norm), True, "ffn1_final_norm")
    return x2.reshape(batch, seq, d)
```

```python
import functools
import math

import jax
import jax.numpy as jnp
from jax import lax
from jax.experimental import pallas as pl
from jax.experimental.pallas import tpu as pltpu

F32 = jnp.float32
BF16 = jnp.bfloat16

RMS_EPS = 1e-6
HEAD_DIM = 128
HG_HEADS = 8
SB_HEADS = 8
POOL_WINDOWS = (2, 4, 8, 16)

V7X_VMEM_BYTES = 64 * 1024 * 1024
V7X_VMEM_USABLE_BYTES = V7X_VMEM_BYTES * 13 // 16
V7X_DEFAULT_SCOPED_VMEM_BYTES = 32 * 1024 * 1024
F32_SUBLANES = 8
LOG2_E = math.log2(math.e)

HG_CHUNK = 128
HG_STEP_TOKENS = 2048
SB_BLOCK = 256
SB_QUERY_BLOCKS = 16
SB_UNDERFLOW_BITS = 160.0
PROJ_ROWS = 1024
PROJ_COLS = 1024
OUT_PROJ_ROWS = 512
FFN_ROWS = 512
FFN_HIDDEN_TILE = 256
POOL_ROWS = 512
CAST_SLAB = 128
POOL_HALO = 16


def _vmem_limit(pipelined_bytes, resident_bytes):
    need = 2 * pipelined_bytes + resident_bytes
    assert need <= V7X_VMEM_USABLE_BYTES, (need, V7X_VMEM_USABLE_BYTES)
    return int(max(need, V7X_DEFAULT_SCOPED_VMEM_BYTES))


def _nbytes(shape, dtype):
    return math.prod(shape) * jnp.dtype(dtype).itemsize


def _rms_norm(x, gain):
    ms = jnp.mean(x * x, axis=-1, keepdims=True)
    return x * lax.rsqrt(ms + RMS_EPS) * gain


def _dot(a, b):
    return jnp.dot(a, b, preferred_element_type=F32)


def _dot_nt(a, b):
    return lax.dot_general(a, b, (((1,), (1,)), ((), ())), preferred_element_type=F32)


def _sigmoid(x):
    return 0.5 * jnp.tanh(0.5 * x) + 0.5


def _silu(x, scale=None):
    half = 0.5 * x
    t = jnp.tanh(half)
    if scale is not None:
        half = half * scale
    return half * t + half


def _in_proj_kernel(x_ref, g_ref, w_ref, oa_ref, oqk_ref, ovt_ref, h_ref, *, n_a_tiles, q_scale, key_block):
    j = pl.program_id(1)

    @pl.when(j == 0)
    def _():
        h_ref[...] = _rms_norm(x_ref[...], g_ref[...]).astype(BF16)

    def tile():
        return _dot(h_ref[...], w_ref[0])

    @pl.when(j < n_a_tiles)
    def _():
        oa_ref[...] = tile()

    @pl.when(j == n_a_tiles)
    def _():
        oqk_ref[...] = (tile() * q_scale).astype(oqk_ref.dtype)

    @pl.when(j == n_a_tiles + 1)
    def _():
        oqk_ref[...] = tile().astype(oqk_ref.dtype)

    @pl.when(j == n_a_tiles + 2)
    def _():
        vt = tile().T.astype(ovt_ref.dtype)
        for kb in range(ovt_ref.shape[1]):
            ovt_ref[0, kb] = vt[:, kb * key_block:(kb + 1) * key_block]


def _in_proj(x2, gain, w_in, layer, batch, seq, key_block):
    n_tok, d = x2.shape
    tm, tn = PROJ_ROWS, PROJ_COLS
    hg_width, sb_width = HG_HEADS * HEAD_DIM, SB_HEADS * HEAD_DIM
    n_a_tiles, n_qk_tiles, n_v_tiles = 4 * hg_width // tn, 2 * sb_width // tn, sb_width // tn
    assert tn == hg_width == sb_width and w_in.shape[2] == (n_a_tiles + n_qk_tiles + n_v_tiles) * tn
    assert seq % tm == 0 and tm % key_block == 0
    tiles_per_seq = seq // tm
    kb_per_tile = tm // key_block
    pipelined = (_nbytes((tm, d), F32) + _nbytes((d, tn), BF16) + _nbytes((tm, tn), F32)
                 + 2 * _nbytes((tm, tn), BF16) + _nbytes((1, d), F32))
    resident = _nbytes((tm, d), BF16) + _nbytes((tm, tn), F32)
    return pl.pallas_call(
        functools.partial(_in_proj_kernel, n_a_tiles=n_a_tiles, q_scale=LOG2_E / math.sqrt(HEAD_DIM),
                          key_block=key_block),
        out_shape=(jax.ShapeDtypeStruct((n_tok, n_a_tiles * tn), F32),
                   jax.ShapeDtypeStruct((n_tok, 2 * tn), BF16),
                   jax.ShapeDtypeStruct((batch, seq // key_block, tn, key_block), BF16)),
        grid=(n_tok // tm, n_a_tiles + n_qk_tiles + n_v_tiles),
        in_specs=[
            pl.BlockSpec((tm, d), lambda i, j: (i, 0)),
            pl.BlockSpec((1, d), lambda i, j: (0, 0)),
            pl.BlockSpec((1, d, tn), lambda i, j: (layer, 0, j)),
        ],
        out_specs=(
            pl.BlockSpec((tm, tn), lambda i, j: (i, jnp.minimum(j, n_a_tiles - 1))),
            pl.BlockSpec((tm, tn), lambda i, j: (i, jnp.clip(j - n_a_tiles, 0, n_qk_tiles - 1))),
            pl.BlockSpec((1, kb_per_tile, tn, key_block),
                         lambda i, j: (i // tiles_per_seq, i % tiles_per_seq, 0, 0)),
        ),
        scratch_shapes=[pltpu.VMEM((tm, d), BF16)],
        compiler_params=pltpu.CompilerParams(
            dimension_semantics=("arbitrary", "arbitrary"),
            vmem_limit_bytes=_vmem_limit(pipelined, resident)),
        name="in_proj",
    )(x2, gain, w_in)


def _cast_rider_specs(weights, layers, grid, step_of):
    w_gate, w_up, w_down = weights
    _, d, hidden = w_gate.shape
    assert w_up.shape == w_gate.shape and w_down.shape[1:] == (hidden, d) and hidden % CAST_SLAB == 0
    n_slabs = max(n for n in range(1, hidden // CAST_SLAB + 1)
                  if (hidden // CAST_SLAB) % n == 0 and n <= math.prod(grid))
    width = hidden // n_slabs

    def slab(*grid_idx):
        return jnp.minimum(step_of(*grid_idx), n_slabs - 1)

    col_out = pl.BlockSpec((1, d, width), lambda *g: (0, 0, slab(*g)))
    row_out = pl.BlockSpec((1, width, d), lambda *g: (0, slab(*g), 0))
    in_specs, out_specs, out_shapes = [], [], []
    for layer in layers:
        col_in = pl.BlockSpec((1, d, width), lambda *g, layer=layer: (layer, 0, slab(*g)))
        row_in = pl.BlockSpec((1, width, d), lambda *g, layer=layer: (layer, slab(*g), 0))
        in_specs += [col_in, col_in, row_in]
        out_specs += [col_out, col_out, row_out]
        out_shapes += [jax.ShapeDtypeStruct((1, d, hidden), BF16), jax.ShapeDtypeStruct((1, d, hidden), BF16),
                       jax.ShapeDtypeStruct((1, hidden, d), BF16)]
    slab_bytes = 3 * len(layers) * (_nbytes((d, width), F32) + _nbytes((d, width), BF16))
    return in_specs, out_specs, out_shapes, n_slabs, slab_bytes


def _cast_rider(step, n_slabs, src_refs, dst_refs):
    @pl.when(step < n_slabs)
    def _():
        for src, dst in zip(src_refs, dst_refs):
            dst[...] = src[...].astype(dst.dtype)


def _split2_bf16(x):
    hi = x.astype(BF16)
    return hi, (x - hi.astype(F32)).astype(BF16)


def _sublane_broadcast(x, group, row):
    rows, lanes = x.shape
    x3 = x.reshape(rows // group, group, lanes)
    return jnp.broadcast_to(x3[:, row:row + 1, :], x3.shape).reshape(rows, lanes)


def _hgrn_kernel(lbl_ref, gn_ref, tri_ref, lvl_ref, qa_ref, fa_ref, ia_ref, ga_ref, *refs, layer_row, n_slabs):
    n_cast = (len(refs) - 2) // 2
    cast_src, o_ref, cast_dst, st_ref = refs[:n_cast], refs[n_cast], refs[n_cast + 1:-1], refs[-1]
    c, ts = HG_CHUNK, HG_STEP_TOKENS
    n_levels = c.bit_length() - 1
    chunks = [slice(j * c, (j + 1) * c) for j in range(ts // c)]

    step = (pl.program_id(0) * pl.num_programs(1) + pl.program_id(1)) * pl.num_programs(2) + pl.program_id(2)
    _cast_rider(step, n_slabs, cast_src, cast_dst)

    @pl.when(pl.program_id(2) == 0)
    def _():
        st_ref[...] = jnp.zeros_like(st_ref)

    logits = lbl_ref[...]
    e = jnp.exp(logits - jnp.max(logits, axis=0, keepdims=True))
    lower = jnp.sum(e[:layer_row + 1], axis=0, keepdims=True) / jnp.sum(e, axis=0, keepdims=True)

    row = lax.broadcasted_iota(jnp.int32, (ts, HEAD_DIM), 0)
    q = _silu(qa_ref[0])
    f = lower + (1.0 - lower) * _sigmoid(fa_ref[0])
    k = 1.0 - f
    v = ia_ref[0]
    v16 = v.astype(BF16)

    tri = tri_ref[...]
    parts = _split2_bf16(jnp.log(f) * LOG2_E)
    g = jnp.concatenate([sum(_dot(tri, p[ch]) for p in parts) for ch in chunks], axis=0)

    level_of = lvl_ref[...]
    att = [jnp.zeros((c, c), F32) for _ in chunks]
    for lvl in range(n_levels):
        half = 1 << lvl
        if half >= F32_SUBLANES:
            shape4 = (ts // (2 * half), 2, half, HEAD_DIM)
            g4, q4, k4 = g.reshape(shape4), q.reshape(shape4), k.reshape(shape4)
            g_mid = g4[:, 0:1, half - 1:half, :]
            x_lo = k4[:, 0:1] * jnp.exp2(g_mid - g4[:, 0:1])
            x_hi = q4[:, 1:2] * jnp.exp2(g4[:, 1:2] - g_mid)
            x = jnp.concatenate([x_lo, x_hi], axis=1).reshape(ts, HEAD_DIM)
        else:
            upper = (row & half) != 0
            if lvl == 0:
                x = jnp.where(upper, q * f, k)
            else:
                g_mid = _sublane_broadcast(g, F32_SUBLANES, half - 1)
                if 2 * half < F32_SUBLANES:
                    second = _sublane_broadcast(g, F32_SUBLANES, 2 * half + half - 1)
                    g_mid = jnp.where((row & (F32_SUBLANES - 1)) < 2 * half, g_mid, second)
                x = jnp.where(upper, q, k) * jnp.exp2(-jnp.abs(g - g_mid))
        x16 = x.astype(BF16)
        for j, ch in enumerate(chunks):
            att[j] = jnp.where(level_of == lvl, _dot_nt(x16[ch], x16[ch]), att[j])
    qk_diag = jnp.sum(q * k, axis=-1, keepdims=True)

    g_last = _sublane_broadcast(g, c, c - 1)
    q_dec = (q * jnp.exp2(g)).astype(BF16)
    k_dec = (k * jnp.exp2(g_last - g)).astype(BF16)
    st = st_ref[...]
    outs = []
    for j, ch in enumerate(chunks):
        a = jnp.where(level_of == n_levels, qk_diag[ch], att[j])
        outs.append(_dot_nt(q_dec[ch], st.astype(BF16)) + _dot(a.astype(BF16), v16[ch]))
        st = st * jnp.exp2(g_last[ch][:1]) + _dot(v[ch].T.astype(BF16), k_dec[ch])
    st_ref[...] = st

    y = _rms_norm(jnp.concatenate(outs, axis=0), gn_ref[...]) * _silu(ga_ref[0])
    o_ref[0] = y.astype(o_ref.dtype)


def _hgrn(proj, lb_logits, out_gain, batch, seq, layer_row, cast_weights, cast_layers):
    ts, c, hd = HG_STEP_TOKENS, HG_CHUNK, HEAD_DIM
    assert seq % ts == 0 and ts % c == 0
    n_rows = lb_logits.shape[0]
    grid = (batch, HG_HEADS, seq // ts)
    cast_in, cast_out, cast_shapes, n_slabs, slab_bytes = _cast_rider_specs(
        cast_weights, cast_layers, grid, lambda b, h, s: (b * grid[1] + h) * grid[2] + s)
    t_idx = lax.broadcasted_iota(jnp.int32, (c, c), 0)
    s_idx = lax.broadcasted_iota(jnp.int32, (c, c), 1)
    tri = (t_idx >= s_idx).astype(BF16)
    level_of = jnp.where(t_idx > s_idx, jnp.iinfo(jnp.int32).bits - 1 - lax.clz(t_idx ^ s_idx),
                         jnp.where(t_idx == s_idx, c.bit_length() - 1, -1)).astype(jnp.int32)

    def head_block(part):
        return pl.BlockSpec((1, ts, hd), lambda b, h, s: (b, s, part * HG_HEADS + h))

    pipelined = (4 * _nbytes((ts, hd), F32) + _nbytes((ts, hd), BF16) + _nbytes((c, c), BF16)
                 + _nbytes((c, c), jnp.int32) + slab_bytes)
    resident = 24 * _nbytes((ts, hd), F32)
    return pl.pallas_call(
        functools.partial(_hgrn_kernel, layer_row=layer_row, n_slabs=n_slabs),
        out_shape=[jax.ShapeDtypeStruct((batch, seq, HG_HEADS * hd), BF16)] + cast_shapes,
        grid=grid,
        in_specs=[
            pl.BlockSpec((n_rows, hd), lambda b, h, s: (0, h)),
            pl.BlockSpec((1, hd), lambda b, h, s: (0, 0)),
            pl.BlockSpec((c, c), lambda b, h, s: (0, 0)),
            pl.BlockSpec((c, c), lambda b, h, s: (0, 0)),
            head_block(0), head_block(1), head_block(2), head_block(3),
        ] + cast_in,
        out_specs=[pl.BlockSpec((1, ts, hd), lambda b, h, s: (b, s, h))] + cast_out,
        scratch_shapes=[pltpu.VMEM((hd, hd), F32)],
        compiler_params=pltpu.CompilerParams(
            dimension_semantics=("arbitrary", "arbitrary", "arbitrary"),
            vmem_limit_bytes=_vmem_limit(pipelined, resident)),
        name="hgrn2",
    )(lb_logits, out_gain, tri, level_of, proj, proj, proj, proj, *(cast_weights * len(cast_layers)))


def _sb_kernel(sfx_ref, q_ref, k_ref, vt_ref, o_ref, acc_ref, carry_ref, sp_ref, lbc_ref, w_ref):
    blk, nq = SB_BLOCK, SB_QUERY_BLOCKS
    first = nq * pl.program_id(2)
    n_depths = first + nq
    causal = (lax.broadcasted_iota(jnp.int32, (blk, blk), 0)
              < lax.broadcasted_iota(jnp.int32, (blk, blk), 1))
    sfx = sfx_ref[...]

    def key_block(h, depth, always_live):
        kb = first + h - depth
        return (kb, None) if always_live else (jnp.maximum(kb, 0), kb >= 0)

    def scores(depth, slot, on_diagonal=False):
        for h in range(nq):
            j, live = key_block(h, depth, on_diagonal)
            ks = k_ref[0, pl.ds(pl.multiple_of(j * blk, blk), blk), :]
            z = _dot_nt(ks, q_ref[0, h * blk:(h + 1) * blk, :])
            sp = jnp.maximum(z, 0.0) + jnp.log2(1.0 + jnp.exp2(-jnp.abs(z)))
            lbc_ref[slot, h] = (z - sp) - carry_ref[h]
            if on_diagonal:
                sp = jnp.where(causal, sp, 0.0)
            sp_ref[slot, h] = sp.astype(BF16)
            total = jnp.sum(sp, axis=0, keepdims=True)
            carry_ref[h] += total if live is None else jnp.where(live, total, 0.0)

    def weights(slot, on_diagonal=False):
        for h in range(nq):
            w = jnp.exp2(lbc_ref[slot, h] - _dot(sfx, sp_ref[slot, h]))
            if on_diagonal:
                w = jnp.where(causal, w, 0.0)
            w_ref[slot, h] = w.astype(BF16)

    def values(depth, slot, on_diagonal=False):
        for h in range(nq):
            j, live = key_block(h, depth, on_diagonal)
            pv = _dot(vt_ref[0, j], w_ref[slot, h])
            acc_ref[h] += pv if live is None else jnp.where(live, pv, 0.0)

    def min_open_carry(next_depth):
        h_idx = lax.broadcasted_iota(jnp.int32, carry_ref.shape, 0)
        return jnp.min(jnp.where(first + h_idx - next_depth >= 0, carry_ref[...], SB_UNDERFLOW_BITS))

    acc_ref[...] = jnp.zeros_like(acc_ref)
    carry_ref[...] = jnp.zeros_like(carry_ref)

    scores(0, 0, on_diagonal=True)
    scores(1, 1)
    weights(0, on_diagonal=True)

    def keep_going(state):
        pairs, open_carry = state
        return jnp.logical_and(pairs < (n_depths - 2) // 2, open_carry < SB_UNDERFLOW_BITS)

    def body(state):
        pairs, _ = state
        d = 2 * pairs + 2
        scores(d, 0)
        weights(1)
        values(d - 2, 0)
        scores(d + 1, 1)
        weights(0)
        values(d - 1, 1)
        return pairs + 1, min_open_carry(d + 2)

    pairs_done, _ = lax.while_loop(keep_going, body, (jnp.int32(0), min_open_carry(2)))
    scored = 2 * pairs_done + 2
    weights(1)
    values(scored - 2, 0)
    values(scored - 1, 1)

    for h in range(nq):
        o_ref[0, h * blk:(h + 1) * blk, :] = acc_ref[h].T.astype(o_ref.dtype)


def _stick_breaking(qk, vt, batch, seq):
    blk, nq, hd = SB_BLOCK, SB_QUERY_BLOCKS, HEAD_DIM
    tq = nq * blk
    assert seq % tq == 0 and nq % 2 == 0
    n_blk = seq // blk
    sfx = (lax.broadcasted_iota(jnp.int32, (blk, blk), 1) > lax.broadcasted_iota(jnp.int32, (blk, blk), 0)).astype(BF16)
    pipelined = (_nbytes((blk, blk), BF16) + 2 * _nbytes((tq, hd), BF16) + 2 * _nbytes((seq, hd), BF16))
    slots = 2 * nq * (2 * _nbytes((blk, blk), BF16) + _nbytes((blk, blk), F32))
    resident = nq * _nbytes((hd, blk), F32) + slots + nq * 4 * _nbytes((blk, blk), F32)
    return pl.pallas_call(
        _sb_kernel,
        out_shape=jax.ShapeDtypeStruct((batch, seq, SB_HEADS * hd), BF16),
        grid=(batch, SB_HEADS, seq // tq),
        in_specs=[
            pl.BlockSpec((blk, blk), lambda b, h, i: (0, 0)),
            pl.BlockSpec((1, tq, hd), lambda b, h, i: (b, i, h)),
            pl.BlockSpec((1, seq, hd), lambda b, h, i: (b, 0, SB_HEADS + h)),
            pl.BlockSpec((1, n_blk, hd, blk), lambda b, h, i: (b, 0, h, 0)),
        ],
        out_specs=pl.BlockSpec((1, tq, hd), lambda b, h, i: (b, i, h)),
        scratch_shapes=[
            pltpu.VMEM((nq, hd, blk), F32),
            pltpu.VMEM((nq, 1, blk), F32),
            pltpu.VMEM((2, nq, blk, blk), BF16),
            pltpu.VMEM((2, nq, blk, blk), F32),
            pltpu.VMEM((2, nq, blk, blk), BF16),
        ],
        compiler_params=pltpu.CompilerParams(
            dimension_semantics=("arbitrary", "arbitrary", "arbitrary"),
            vmem_limit_bytes=_vmem_limit(pipelined, resident)),
        name="stick_breaking",
    )(sfx, qk, qk, vt)


def _out_proj_kernel(oa_ref, ob_ref, wa_ref, wb_ref, x_ref, o_ref):
    o_ref[...] = x_ref[...] + _dot(oa_ref[...], wa_ref[0]) + _dot(ob_ref[...], wb_ref[0])


def _out_proj(o_a, o_b, w_out, layer, x2):
    n_tok, d = x2.shape
    ka, kb = o_a.shape[1], o_b.shape[1]
    tm = OUT_PROJ_ROWS
    assert n_tok % tm == 0 and ka == kb and w_out.shape[1] == ka + kb
    pipelined = (_nbytes((tm, ka), BF16) + _nbytes((tm, kb), BF16) + _nbytes((ka, d), BF16)
                 + _nbytes((kb, d), BF16) + 2 * _nbytes((tm, d), F32))
    resident = 2 * _nbytes((tm, d), F32)
    return pl.pallas_call(
        _out_proj_kernel,
        out_shape=jax.ShapeDtypeStruct((n_tok, d), F32),
        grid=(n_tok // tm,),
        in_specs=[
            pl.BlockSpec((tm, ka), lambda i: (i, 0)),
            pl.BlockSpec((tm, kb), lambda i: (i, 0)),
            pl.BlockSpec((1, ka, d), lambda i: (layer, 0, 0)),
            pl.BlockSpec((1, kb, d), lambda i: (layer, 1, 0)),
            pl.BlockSpec((tm, d), lambda i: (i, 0)),
        ],
        out_specs=pl.BlockSpec((tm, d), lambda i: (i, 0)),
        compiler_params=pltpu.CompilerParams(
            dimension_semantics=("arbitrary",),
            vmem_limit_bytes=_vmem_limit(pipelined, resident)),
        name="out_proj",
    )(o_a, o_b, w_out, w_out, x2)


def _ffn_kernel(x_ref, g_ref, wg_ref, wu_ref, wd_ref, fg_ref, o_ref, h_ref, *, final_norm):
    j = pl.program_id(1)

    @pl.when(j == 0)
    def _():
        x = x_ref[...]
        h_ref[...] = _rms_norm(x, g_ref[...]).astype(BF16)
        o_ref[...] = x

    h = h_ref[...]
    act = _silu(_dot(h, wg_ref[0]), scale=_dot(h, wu_ref[0]))
    o_ref[...] += _dot(act.astype(BF16), wd_ref[0])

    if final_norm:
        @pl.when(j == pl.num_programs(1) - 1)
        def _():
            o_ref[...] = _rms_norm(o_ref[...], fg_ref[...])


def _ffn(x2, gain, w_gate, w_up, w_down, layer, final_gain, final_norm, name):
    n_tok, d = x2.shape
    hidden = w_gate.shape[2]
    tm, th = FFN_ROWS, FFN_HIDDEN_TILE
    assert n_tok % tm == 0 and hidden % th == 0
    pipelined = (2 * _nbytes((tm, d), F32) + 2 * _nbytes((d, th), BF16) + _nbytes((th, d), BF16)
                 + 2 * _nbytes((1, d), F32))
    resident = _nbytes((tm, d), BF16) + 4 * _nbytes((tm, th), F32) + _nbytes((tm, d), F32)
    return pl.pallas_call(
        functools.partial(_ffn_kernel, final_norm=final_norm),
        out_shape=jax.ShapeDtypeStruct((n_tok, d), F32),
        grid=(n_tok // tm, hidden // th),
        in_specs=[
            pl.BlockSpec((tm, d), lambda i, j: (i, 0)),
            pl.BlockSpec((1, d), lambda i, j: (0, 0)),
            pl.BlockSpec((1, d, th), lambda i, j: (layer, 0, j)),
            pl.BlockSpec((1, d, th), lambda i, j: (layer, 0, j)),
            pl.BlockSpec((1, th, d), lambda i, j: (layer, j, 0)),
            pl.BlockSpec((1, d), lambda i, j: (0, 0)),
        ],
        out_specs=pl.BlockSpec((tm, d), lambda i, j: (i, 0)),
        scratch_shapes=[pltpu.VMEM((tm, d), BF16)],
        compiler_params=pltpu.CompilerParams(
            dimension_semantics=("arbitrary", "arbitrary"),
            vmem_limit_bytes=_vmem_limit(pipelined, resident)),
        name=name,
    )(x2, gain, w_gate, w_up, w_down, final_gain)


def _pool_kernel(x_ref, halo_ref, g_ref, pw_ref, sc_ref, o_ref, hs_ref):
    ts = x_ref.shape[1]
    halo = POOL_HALO
    i = pl.program_id(1)
    gain = g_ref[...]
    x = x_ref[0]
    hs_ref[0:halo, :] = jnp.where(i == 0, 0.0, _rms_norm(halo_ref[0], gain))
    hs_ref[halo:halo + ts, :] = _rms_norm(x, gain)

    pos = i * ts + lax.broadcasted_iota(jnp.int32, (ts, 1), 0)
    group = pw_ref.shape[1]
    for gi, win in enumerate(POOL_WINDOWS):
        cols = slice(gi * group, (gi + 1) * group)
        wsum = hs_ref[:, cols]
        for s in range(win.bit_length() - 1):
            wsum = wsum + pltpu.roll(wsum, 1 << s, axis=0)
        h = hs_ref[halo:halo + ts, cols]
        inv_count = 1.0 / jnp.minimum(pos + 1, win).astype(F32)
        pooled = wsum[halo:] * inv_count - h
        mixed = _dot(pooled.astype(BF16), pw_ref[gi])
        o_ref[0, :, cols] = x[:, cols] + mixed * sc_ref[:, cols]


def _pool_mixer(x3, gain, pool_w, scale):
    batch, seq, d = x3.shape
    ts, halo = POOL_ROWS, POOL_HALO
    n_groups, group = pool_w.shape[0], pool_w.shape[1]
    assert seq % ts == 0 and ts % halo == 0 and max(POOL_WINDOWS) - 1 <= halo
    assert all(w & (w - 1) == 0 for w in POOL_WINDOWS)
    assert n_groups == len(POOL_WINDOWS) and n_groups * group == d
    halo_per_tile = ts // halo
    pipelined = (2 * _nbytes((ts, d), F32) + _nbytes((halo, d), F32) + _nbytes(pool_w.shape, BF16)
                 + 2 * _nbytes((1, d), F32))
    resident = _nbytes((ts + halo, d), F32) + 3 * _nbytes((ts, d), F32)
    return pl.pallas_call(
        _pool_kernel,
        out_shape=jax.ShapeDtypeStruct((batch, seq, d), F32),
        grid=(batch, seq // ts),
        in_specs=[
            pl.BlockSpec((1, ts, d), lambda b, i: (b, i, 0)),
            pl.BlockSpec((1, halo, d), lambda b, i: (b, jnp.maximum(i * halo_per_tile - 1, 0), 0)),
            pl.BlockSpec((1, d), lambda b, i: (0, 0)),
            pl.BlockSpec((n_groups, group, group), lambda b, i: (0, 0, 0)),
            pl.BlockSpec((1, d), lambda b, i: (0, 0)),
        ],
        out_specs=pl.BlockSpec((1, ts, d), lambda b, i: (b, i, 0)),
        scratch_shapes=[pltpu.VMEM((ts + halo, d), F32)],
        compiler_params=pltpu.CompilerParams(
            dimension_semantics=("arbitrary", "arbitrary"),
            vmem_limit_bytes=_vmem_limit(pipelined, resident)),
        name="pool_mixer",
    )(x3, x3, gain, pool_w, scale)


def kernel(x, mix_norm, ffn_norm, final_norm, ab_w_in, lb_logits, hg_out_norm, ab_w_out,
           pool_w, pool_scale, ffn_w_gate, ffn_w_up, ffn_w_down):
    batch, seq, d = x.shape
    n_tok = batch * seq
    hg_width = HG_HEADS * HEAD_DIM
    sb_width = SB_HEADS * HEAD_DIM
    row = lambda a: a.reshape(1, -1)

    x2 = x.reshape(n_tok, d)
    ffn_weights = (ffn_w_gate, ffn_w_up, ffn_w_down)
    proj_a, qk, vt = _in_proj(x2, row(mix_norm[0]), ab_w_in.astype(BF16), 0, batch, seq, SB_BLOCK)
    o_a, *ffn_bf16 = _hgrn(proj_a.reshape(batch, seq, 4 * hg_width), lb_logits, row(hg_out_norm[0]),
                           batch, seq, 0, ffn_weights, (0, 1))
    ffn0_weights, ffn1_weights = ffn_bf16[:3], ffn_bf16[3:]
    o_b = _stick_breaking(qk.reshape(batch, seq, 2 * sb_width), vt, batch, seq)
    x2 = _out_proj(o_a.reshape(n_tok, hg_width), o_b.reshape(n_tok, sb_width), ab_w_out.astype(BF16), 0, x2)
    x2 = _ffn(x2, row(ffn_norm[0]), *ffn0_weights, 0, row(final_norm), False, "ffn0")

    x3 = _pool_mixer(x2.reshape(batch, seq, d), row(mix_norm[1]), pool_w[0].astype(BF16), row(pool_scale[0]))
    x2 = _ffn(x3.reshape(n_tok, d), row(ffn_norm[1]), *ffn1_weights, 0, row(final_norm), True, "ffn1_final_norm")
    return x2.reshape(batch, seq, d)
```

```python
import functools
import math

import jax
import jax.numpy as jnp
from jax import lax
from jax.experimental import pallas as pl
from jax.experimental.pallas import tpu as pltpu

F32 = jnp.float32
BF16 = jnp.bfloat16

RMS_EPS = 1e-6
HEAD_DIM = 128
HG_HEADS = 8
SB_HEADS = 8
POOL_WINDOWS = (2, 4, 8, 16)

V7X_VMEM_BYTES = 64 * 1024 * 1024
V7X_VMEM_USABLE_BYTES = V7X_VMEM_BYTES * 13 // 16
V7X_DEFAULT_SCOPED_VMEM_BYTES = 32 * 1024 * 1024
F32_SUBLANES = 8
LOG2_E = math.log2(math.e)

HG_CHUNK = 128
HG_STEP_TOKENS = 2048
SB_BLOCK = 256
SB_QUERY_BLOCKS = 16
SB_UNDERFLOW_BITS = 160.0
PROJ_ROWS = 512
PROJ_COLS = 1024
OUT_PROJ_ROWS = 512
FFN_ROWS = 512
FFN_HIDDEN_TILE = 512
POOL_ROWS = 512
CAST_SLAB = 128
POOL_HALO = 16


def _vmem_limit(pipelined_bytes, resident_bytes):
    need = 2 * pipelined_bytes + resident_bytes
    assert need <= V7X_VMEM_USABLE_BYTES, (need, V7X_VMEM_USABLE_BYTES)
    return int(max(need, V7X_DEFAULT_SCOPED_VMEM_BYTES))


def _nbytes(shape, dtype):
    return math.prod(shape) * jnp.dtype(dtype).itemsize


def _rms_norm(x, gain):
    ms = jnp.mean(x * x, axis=-1, keepdims=True)
    return x * lax.rsqrt(ms + RMS_EPS) * gain


def _dot(a, b):
    return jnp.dot(a, b, preferred_element_type=F32)


def _dot_nt(a, b):
    return lax.dot_general(a, b, (((1,), (1,)), ((), ())), preferred_element_type=F32)


def _sigmoid(x):
    return 0.5 * jnp.tanh(0.5 * x) + 0.5


def _silu(x, scale=None):
    half = 0.5 * x
    t = jnp.tanh(half)
    if scale is not None:
        half = half * scale
    return half * t + half


def _in_proj_kernel(x_ref, g_ref, w_ref, oa_ref, oqk_ref, ovt_ref, h_ref, *, n_a_tiles, q_scale, key_block):
    j = pl.program_id(1)

    @pl.when(j == 0)
    def _():
        h_ref[...] = _rms_norm(x_ref[...], g_ref[...]).astype(BF16)

    def tile():
        return _dot(h_ref[...], w_ref[0])

    @pl.when(j < n_a_tiles)
    def _():
        oa_ref[...] = tile()

    @pl.when(j == n_a_tiles)
    def _():
        oqk_ref[...] = (tile() * q_scale).astype(oqk_ref.dtype)

    @pl.when(j == n_a_tiles + 1)
    def _():
        oqk_ref[...] = tile().astype(oqk_ref.dtype)

    @pl.when(j == n_a_tiles + 2)
    def _():
        vt = tile().T.astype(ovt_ref.dtype)
        for kb in range(ovt_ref.shape[1]):
            ovt_ref[0, kb] = vt[:, kb * key_block:(kb + 1) * key_block]


def _in_proj(x2, gain, w_in, layer, batch, seq, key_block):
    n_tok, d = x2.shape
    tm, tn = PROJ_ROWS, PROJ_COLS
    hg_width, sb_width = HG_HEADS * HEAD_DIM, SB_HEADS * HEAD_DIM
    n_a_tiles, n_qk_tiles, n_v_tiles = 4 * hg_width // tn, 2 * sb_width // tn, sb_width // tn
    assert tn == hg_width == sb_width and w_in.shape[2] == (n_a_tiles + n_qk_tiles + n_v_tiles) * tn
    assert seq % tm == 0 and tm % key_block == 0
    tiles_per_seq = seq // tm
    kb_per_tile = tm // key_block
    pipelined = (_nbytes((tm, d), F32) + _nbytes((d, tn), BF16) + _nbytes((tm, tn), F32)
                 + 2 * _nbytes((tm, tn), BF16) + _nbytes((1, d), F32))
    resident = _nbytes((tm, d), BF16) + _nbytes((tm, tn), F32)
    return pl.pallas_call(
        functools.partial(_in_proj_kernel, n_a_tiles=n_a_tiles, q_scale=LOG2_E / math.sqrt(HEAD_DIM),
                          key_block=key_block),
        out_shape=(jax.ShapeDtypeStruct((n_tok, n_a_tiles * tn), F32),
                   jax.ShapeDtypeStruct((n_tok, 2 * tn), BF16),
                   jax.ShapeDtypeStruct((batch, seq // key_block, tn, key_block), BF16)),
        grid=(n_tok // tm, n_a_tiles + n_qk_tiles + n_v_tiles),
        in_specs=[
            pl.BlockSpec((tm, d), lambda i, j: (i, 0)),
            pl.BlockSpec((1, d), lambda i, j: (0, 0)),
            pl.BlockSpec((1, d, tn), lambda i, j: (layer, 0, j)),
        ],
        out_specs=(
            pl.BlockSpec((tm, tn), lambda i, j: (i, jnp.minimum(j, n_a_tiles - 1))),
            pl.BlockSpec((tm, tn), lambda i, j: (i, jnp.clip(j - n_a_tiles, 0, n_qk_tiles - 1))),
            pl.BlockSpec((1, kb_per_tile, tn, key_block),
                         lambda i, j: (i // tiles_per_seq, i % tiles_per_seq, 0, 0)),
        ),
        scratch_shapes=[pltpu.VMEM((tm, d), BF16)],
        compiler_params=pltpu.CompilerParams(
            dimension_semantics=("arbitrary", "arbitrary"),
            vmem_limit_bytes=_vmem_limit(pipelined, resident)),
        name="in_proj",
    )(x2, gain, w_in)


def _cast_rider_specs(weights, layers, grid, step_of):
    w_gate, w_up, w_down = weights
    _, d, hidden = w_gate.shape
    assert w_up.shape == w_gate.shape and w_down.shape[1:] == (hidden, d) and hidden % CAST_SLAB == 0
    n_slabs = max(n for n in range(1, hidden // CAST_SLAB + 1)
                  if (hidden // CAST_SLAB) % n == 0 and n <= math.prod(grid))
    width = hidden // n_slabs

    def slab(*grid_idx):
        return jnp.minimum(step_of(*grid_idx), n_slabs - 1)

    col_out = pl.BlockSpec((1, d, width), lambda *g: (0, 0, slab(*g)))
    row_out = pl.BlockSpec((1, width, d), lambda *g: (0, slab(*g), 0))
    in_specs, out_specs, out_shapes = [], [], []
    for layer in layers:
        col_in = pl.BlockSpec((1, d, width), lambda *g, layer=layer: (layer, 0, slab(*g)))
        row_in = pl.BlockSpec((1, width, d), lambda *g, layer=layer: (layer, slab(*g), 0))
        in_specs += [col_in, col_in, row_in]
        out_specs += [col_out, col_out, row_out]
        out_shapes += [jax.ShapeDtypeStruct((1, d, hidden), BF16), jax.ShapeDtypeStruct((1, d, hidden), BF16),
                       jax.ShapeDtypeStruct((1, hidden, d), BF16)]
    slab_bytes = 3 * len(layers) * (_nbytes((d, width), F32) + _nbytes((d, width), BF16))
    return in_specs, out_specs, out_shapes, n_slabs, slab_bytes


def _cast_rider(step, n_slabs, src_refs, dst_refs):
    @pl.when(step < n_slabs)
    def _():
        for src, dst in zip(src_refs, dst_refs):
            dst[...] = src[...].astype(dst.dtype)


def _split2_bf16(x):
    hi = x.astype(BF16)
    return hi, (x - hi.astype(F32)).astype(BF16)


def _sublane_broadcast(x, group, row):
    rows, lanes = x.shape
    x3 = x.reshape(rows // group, group, lanes)
    return jnp.broadcast_to(x3[:, row:row + 1, :], x3.shape).reshape(rows, lanes)


def _hgrn_kernel(lbl_ref, gn_ref, tri_ref, lvl_ref, qa_ref, fa_ref, ia_ref, ga_ref, *refs, layer_row, n_slabs):
    n_cast = (len(refs) - 2) // 2
    cast_src, o_ref, cast_dst, st_ref = refs[:n_cast], refs[n_cast], refs[n_cast + 1:-1], refs[-1]
    c, ts = HG_CHUNK, HG_STEP_TOKENS
    n_levels = c.bit_length() - 1
    chunks = [slice(j * c, (j + 1) * c) for j in range(ts // c)]

    step = (pl.program_id(0) * pl.num_programs(1) + pl.program_id(1)) * pl.num_programs(2) + pl.program_id(2)
    _cast_rider(step, n_slabs, cast_src, cast_dst)

    @pl.when(pl.program_id(2) == 0)
    def _():
        st_ref[...] = jnp.zeros_like(st_ref)

    logits = lbl_ref[...]
    e = jnp.exp(logits - jnp.max(logits, axis=0, keepdims=True))
    lower = jnp.sum(e[:layer_row + 1], axis=0, keepdims=True) / jnp.sum(e, axis=0, keepdims=True)

    row = lax.broadcasted_iota(jnp.int32, (ts, HEAD_DIM), 0)
    q = _silu(qa_ref[0])
    f = lower + (1.0 - lower) * _sigmoid(fa_ref[0])
    k = 1.0 - f
    v = ia_ref[0]
    v16 = v.astype(BF16)

    tri = tri_ref[...]
    parts = _split2_bf16(jnp.log(f) * LOG2_E)
    g = jnp.concatenate([sum(_dot(tri, p[ch]) for p in parts) for ch in chunks], axis=0)

    level_of = lvl_ref[...]
    att = [jnp.zeros((c, c), F32) for _ in chunks]
    for lvl in range(n_levels):
        half = 1 << lvl
        if half >= F32_SUBLANES:
            shape4 = (ts // (2 * half), 2, half, HEAD_DIM)
            g4, q4, k4 = g.reshape(shape4), q.reshape(shape4), k.reshape(shape4)
            g_mid = g4[:, 0:1, half - 1:half, :]
            x_lo = k4[:, 0:1] * jnp.exp2(g_mid - g4[:, 0:1])
            x_hi = q4[:, 1:2] * jnp.exp2(g4[:, 1:2] - g_mid)
            x = jnp.concatenate([x_lo, x_hi], axis=1).reshape(ts, HEAD_DIM)
        else:
            upper = (row & half) != 0
            if lvl == 0:
                x = jnp.where(upper, q * f, k)
            else:
                g_mid = _sublane_broadcast(g, F32_SUBLANES, half - 1)
                if 2 * half < F32_SUBLANES:
                    second = _sublane_broadcast(g, F32_SUBLANES, 2 * half + half - 1)
                    g_mid = jnp.where((row & (F32_SUBLANES - 1)) < 2 * half, g_mid, second)
                x = jnp.where(upper, q, k) * jnp.exp2(-jnp.abs(g - g_mid))
        x16 = x.astype(BF16)
        for j, ch in enumerate(chunks):
            att[j] = jnp.where(level_of == lvl, _dot_nt(x16[ch], x16[ch]), att[j])
    qk_diag = jnp.sum(q * k, axis=-1, keepdims=True)

    g_last = _sublane_broadcast(g, c, c - 1)
    q_dec = (q * jnp.exp2(g)).astype(BF16)
    k_dec = (k * jnp.exp2(g_last - g)).astype(BF16)
    st = st_ref[...]
    outs = []
    for j, ch in enumerate(chunks):
        a = jnp.where(level_of == n_levels, qk_diag[ch], att[j])
        outs.append(_dot_nt(q_dec[ch], st.astype(BF16)) + _dot(a.astype(BF16), v16[ch]))
        st = st * jnp.exp2(g_last[ch][:1]) + _dot(v[ch].T.astype(BF16), k_dec[ch])
    st_ref[...] = st

    y = _rms_norm(jnp.concatenate(outs, axis=0), gn_ref[...]) * _silu(ga_ref[0])
    o_ref[0] = y.astype(o_ref.dtype)


def _hgrn(proj, lb_logits, out_gain, batch, seq, layer_row, cast_weights, cast_layers):
    ts, c, hd = HG_STEP_TOKENS, HG_CHUNK, HEAD_DIM
    assert seq % ts == 0 and ts % c == 0
    n_rows = lb_logits.shape[0]
    grid = (batch, HG_HEADS, seq // ts)
    cast_in, cast_out, cast_shapes, n_slabs, slab_bytes = _cast_rider_specs(
        cast_weights, cast_layers, grid, lambda b, h, s: (b * grid[1] + h) * grid[2] + s)
    t_idx = lax.broadcasted_iota(jnp.int32, (c, c), 0)
    s_idx = lax.broadcasted_iota(jnp.int32, (c, c), 1)
    tri = (t_idx >= s_idx).astype(BF16)
    level_of = jnp.where(t_idx > s_idx, jnp.iinfo(jnp.int32).bits - 1 - lax.clz(t_idx ^ s_idx),
                         jnp.where(t_idx == s_idx, c.bit_length() - 1, -1)).astype(jnp.int32)

    def head_block(part):
        return pl.BlockSpec((1, ts, hd), lambda b, h, s: (b, s, part * HG_HEADS + h))

    pipelined = (4 * _nbytes((ts, hd), F32) + _nbytes((ts, hd), BF16) + _nbytes((c, c), BF16)
                 + _nbytes((c, c), jnp.int32) + slab_bytes)
    resident = 24 * _nbytes((ts, hd), F32)
    return pl.pallas_call(
        functools.partial(_hgrn_kernel, layer_row=layer_row, n_slabs=n_slabs),
        out_shape=[jax.ShapeDtypeStruct((batch, seq, HG_HEADS * hd), BF16)] + cast_shapes,
        grid=grid,
        in_specs=[
            pl.BlockSpec((n_rows, hd), lambda b, h, s: (0, h)),
            pl.BlockSpec((1, hd), lambda b, h, s: (0, 0)),
            pl.BlockSpec((c, c), lambda b, h, s: (0, 0)),
            pl.BlockSpec((c, c), lambda b, h, s: (0, 0)),
            head_block(0), head_block(1), head_block(2), head_block(3),
        ] + cast_in,
        out_specs=[pl.BlockSpec((1, ts, hd), lambda b, h, s: (b, s, h))] + cast_out,
        scratch_shapes=[pltpu.VMEM((hd, hd), F32)],
        compiler_params=pltpu.CompilerParams(
            dimension_semantics=("arbitrary", "arbitrary", "arbitrary"),
            vmem_limit_bytes=_vmem_limit(pipelined, resident)),
        name="hgrn2",
    )(lb_logits, out_gain, tri, level_of, proj, proj, proj, proj, *(cast_weights * len(cast_layers)))


def _sb_kernel(sfx_ref, q_ref, k_ref, vt_ref, o_ref, acc_ref, carry_ref, sp_ref, lbc_ref, w_ref):
    blk, nq = SB_BLOCK, SB_QUERY_BLOCKS
    first = nq * pl.program_id(2)
    n_depths = first + nq
    causal = (lax.broadcasted_iota(jnp.int32, (blk, blk), 0)
              < lax.broadcasted_iota(jnp.int32, (blk, blk), 1))
    sfx = sfx_ref[...]

    def key_block(h, depth, always_live):
        kb = first + h - depth
        return (kb, None) if always_live else (jnp.maximum(kb, 0), kb >= 0)

    def scores(depth, slot, on_diagonal=False):
        for h in range(nq):
            j, live = key_block(h, depth, on_diagonal)
            ks = k_ref[0, pl.ds(pl.multiple_of(j * blk, blk), blk), :]
            z = _dot_nt(ks, q_ref[0, h * blk:(h + 1) * blk, :])
            sp = jnp.maximum(z, 0.0) + jnp.log2(1.0 + jnp.exp2(-jnp.abs(z)))
            lbc_ref[slot, h] = (z - sp) - carry_ref[h]
            if on_diagonal:
                sp = jnp.where(causal, sp, 0.0)
            sp_ref[slot, h] = sp.astype(BF16)
            total = jnp.sum(sp, axis=0, keepdims=True)
            carry_ref[h] += total if live is None else jnp.where(live, total, 0.0)

    def weights(slot, on_diagonal=False):
        for h in range(nq):
            w = jnp.exp2(lbc_ref[slot, h] - _dot(sfx, sp_ref[slot, h]))
            if on_diagonal:
                w = jnp.where(causal, w, 0.0)
            w_ref[slot, h] = w.astype(BF16)

    def values(depth, slot, on_diagonal=False):
        for h in range(nq):
            j, live = key_block(h, depth, on_diagonal)
            pv = _dot(vt_ref[0, j], w_ref[slot, h])
            acc_ref[h] += pv if live is None else jnp.where(live, pv, 0.0)

    def min_open_carry(next_depth):
        h_idx = lax.broadcasted_iota(jnp.int32, carry_ref.shape, 0)
        return jnp.min(jnp.where(first + h_idx - next_depth >= 0, carry_ref[...], SB_UNDERFLOW_BITS))

    acc_ref[...] = jnp.zeros_like(acc_ref)
    carry_ref[...] = jnp.zeros_like(carry_ref)

    scores(0, 0, on_diagonal=True)
    scores(1, 1)
    weights(0, on_diagonal=True)

    def keep_going(state):
        pairs, open_carry = state
        return jnp.logical_and(pairs < (n_depths - 2) // 2, open_carry < SB_UNDERFLOW_BITS)

    def body(state):
        pairs, _ = state
        d = 2 * pairs + 2
        scores(d, 0)
        weights(1)
        values(d - 2, 0)
        scores(d + 1, 1)
        weights(0)
        values(d - 1, 1)
        return pairs + 1, min_open_carry(d + 2)

    pairs_done, _ = lax.while_loop(keep_going, body, (jnp.int32(0), min_open_carry(2)))
    scored = 2 * pairs_done + 2
    weights(1)
    values(scored - 2, 0)
    values(scored - 1, 1)

    for h in range(nq):
        o_ref[0, h * blk:(h + 1) * blk, :] = acc_ref[h].T.astype(o_ref.dtype)


def _stick_breaking(qk, vt, batch, seq):
    blk, nq, hd = SB_BLOCK, SB_QUERY_BLOCKS, HEAD_DIM
    tq = nq * blk
    assert seq % tq == 0 and nq % 2 == 0
    n_blk = seq // blk
    sfx = (lax.broadcasted_iota(jnp.int32, (blk, blk), 1) > lax.broadcasted_iota(jnp.int32, (blk, blk), 0)).astype(BF16)
    pipelined = (_nbytes((blk, blk), BF16) + 2 * _nbytes((tq, hd), BF16) + 2 * _nbytes((seq, hd), BF16))
    slots = 2 * nq * (2 * _nbytes((blk, blk), BF16) + _nbytes((blk, blk), F32))
    resident = nq * _nbytes((hd, blk), F32) + slots + nq * 4 * _nbytes((blk, blk), F32)
    return pl.pallas_call(
        _sb_kernel,
        out_shape=jax.ShapeDtypeStruct((batch, seq, SB_HEADS * hd), BF16),
        grid=(batch, SB_HEADS, seq // tq),
        in_specs=[
            pl.BlockSpec((blk, blk), lambda b, h, i: (0, 0)),
            pl.BlockSpec((1, tq, hd), lambda b, h, i: (b, i, h)),
            pl.BlockSpec((1, seq, hd), lambda b, h, i: (b, 0, SB_HEADS + h)),
            pl.BlockSpec((1, n_blk, hd, blk), lambda b, h, i: (b, 0, h, 0)),
        ],
        out_specs=pl.BlockSpec((1, tq, hd), lambda b, h, i: (b, i, h)),
        scratch_shapes=[
            pltpu.VMEM((nq, hd, blk), F32),
            pltpu.VMEM((nq, 1, blk), F32),
            pltpu.VMEM((2, nq, blk, blk), BF16),
            pltpu.VMEM((2, nq, blk, blk), F32),
            pltpu.VMEM((2, nq, blk, blk), BF16),
        ],
        compiler_params=pltpu.CompilerParams(
            dimension_semantics=("arbitrary", "arbitrary", "arbitrary"),
            vmem_limit_bytes=_vmem_limit(pipelined, resident)),
        name="stick_breaking",
    )(sfx, qk, qk, vt)


def _out_proj_kernel(oa_ref, ob_ref, wa_ref, wb_ref, x_ref, o_ref):
    o_ref[...] = x_ref[...] + _dot(oa_ref[...], wa_ref[0]) + _dot(ob_ref[...], wb_ref[0])


def _out_proj(o_a, o_b, w_out, layer, x2):
    n_tok, d = x2.shape
    ka, kb = o_a.shape[1], o_b.shape[1]
    tm = OUT_PROJ_ROWS
    assert n_tok % tm == 0 and ka == kb and w_out.shape[1] == ka + kb
    pipelined = (_nbytes((tm, ka), BF16) + _nbytes((tm, kb), BF16) + _nbytes((ka, d), BF16)
                 + _nbytes((kb, d), BF16) + 2 * _nbytes((tm, d), F32))
    resident = 2 * _nbytes((tm, d), F32)
    return pl.pallas_call(
        _out_proj_kernel,
        out_shape=jax.ShapeDtypeStruct((n_tok, d), F32),
        grid=(n_tok // tm,),
        in_specs=[
            pl.BlockSpec((tm, ka), lambda i: (i, 0)),
            pl.BlockSpec((tm, kb), lambda i: (i, 0)),
            pl.BlockSpec((1, ka, d), lambda i: (layer, 0, 0)),
            pl.BlockSpec((1, kb, d), lambda i: (layer, 1, 0)),
            pl.BlockSpec((tm, d), lambda i: (i, 0)),
        ],
        out_specs=pl.BlockSpec((tm, d), lambda i: (i, 0)),
        compiler_params=pltpu.CompilerParams(
            dimension_semantics=("arbitrary",),
            vmem_limit_bytes=_vmem_limit(pipelined, resident)),
        name="out_proj",
    )(o_a, o_b, w_out, w_out, x2)


def _ffn_kernel(x_ref, g_ref, wg_ref, wu_ref, wd_ref, fg_ref, o_ref, h_ref, *, final_norm):
    j = pl.program_id(1)

    @pl.when(j == 0)
    def _():
        x = x_ref[...]
        h_ref[...] = _rms_norm(x, g_ref[...]).astype(BF16)
        o_ref[...] = x

    h = h_ref[...]
    act = _silu(_dot(h, wg_ref[0]), scale=_dot(h, wu_ref[0]))
    o_ref[...] += _dot(act.astype(BF16), wd_ref[0])

    if final_norm:
        @pl.when(j == pl.num_programs(1) - 1)
        def _():
            o_ref[...] = _rms_norm(o_ref[...], fg_ref[...])


def _ffn(x2, gain, w_gate, w_up, w_down, layer, final_gain, final_norm, name):
    n_tok, d = x2.shape
    hidden = w_gate.shape[2]
    tm, th = FFN_ROWS, FFN_HIDDEN_TILE
    assert n_tok % tm == 0 and hidden % th == 0
    pipelined = (2 * _nbytes((tm, d), F32) + 2 * _nbytes((d, th), BF16) + _nbytes((th, d), BF16)
                 + 2 * _nbytes((1, d), F32))
    resident = _nbytes((tm, d), BF16) + 4 * _nbytes((tm, th), F32) + _nbytes((tm, d), F32)
    return pl.pallas_call(
        functools.partial(_ffn_kernel, final_norm=final_norm),
        out_shape=jax.ShapeDtypeStruct((n_tok, d), F32),
        grid=(n_tok // tm, hidden // th),
        in_specs=[
            pl.BlockSpec((tm, d), lambda i, j: (i, 0)),
            pl.BlockSpec((1, d), lambda i, j: (0, 0)),
            pl.BlockSpec((1, d, th), lambda i, j: (layer, 0, j)),
            pl.BlockSpec((1, d, th), lambda i, j: (layer, 0, j)),
            pl.BlockSpec((1, th, d), lambda i, j: (layer, j, 0)),
            pl.BlockSpec((1, d), lambda i, j: (0, 0)),
        ],
        out_specs=pl.BlockSpec((tm, d), lambda i, j: (i, 0)),
        scratch_shapes=[pltpu.VMEM((tm, d), BF16)],
        compiler_params=pltpu.CompilerParams(
            dimension_semantics=("arbitrary", "arbitrary"),
            vmem_limit_bytes=_vmem_limit(pipelined, resident)),
        name=name,
    )(x2, gain, w_gate, w_up, w_down, final_gain)


def _pool_kernel(x_ref, halo_ref, g_ref, pw_ref, sc_ref, o_ref, hs_ref):
    ts = x_ref.shape[1]
    halo = POOL_HALO
    i = pl.program_id(1)
    gain = g_ref[...]
    x = x_ref[0]
    hs_ref[0:halo, :] = jnp.where(i == 0, 0.0, _rms_norm(halo_ref[0], gain))
    hs_ref[halo:halo + ts, :] = _rms_norm(x, gain)

    pos = i * ts + lax.broadcasted_iota(jnp.int32, (ts, 1), 0)
    group = pw_ref.shape[1]
    for gi, win in enumerate(POOL_WINDOWS):
        cols = slice(gi * group, (gi + 1) * group)
        wsum = hs_ref[:, cols]
        for s in range(win.bit_length() - 1):
            wsum = wsum + pltpu.roll(wsum, 1 << s, axis=0)
        h = hs_ref[halo:halo + ts, cols]
        inv_count = 1.0 / jnp.minimum(pos + 1, win).astype(F32)
        pooled = wsum[halo:] * inv_count - h
        mixed = _dot(pooled.astype(BF16), pw_ref[gi])
        o_ref[0, :, cols] = x[:, cols] + mixed * sc_ref[:, cols]


def _pool_mixer(x3, gain, pool_w, scale):
    batch, seq, d = x3.shape
    ts, halo = POOL_ROWS, POOL_HALO
    n_groups, group = pool_w.shape[0], pool_w.shape[1]
    assert seq % ts == 0 and ts % halo == 0 and max(POOL_WINDOWS) - 1 <= halo
    assert all(w & (w - 1) == 0 for w in POOL_WINDOWS)
    assert n_groups == len(POOL_WINDOWS) and n_groups * group == d
    halo_per_tile = ts // halo
    pipelined = (2 * _nbytes((ts, d), F32) + _nbytes((halo, d), F32) + _nbytes(pool_w.shape, BF16)
                 + 2 * _nbytes((1, d), F32))
    resident = _nbytes((ts + halo, d), F32) + 3 * _nbytes((ts, d), F32)
    return pl.pallas_call(
        _pool_kernel,
        out_shape=jax.ShapeDtypeStruct((batch, seq, d), F32),
        grid=(batch, seq // ts),
        in_specs=[
            pl.BlockSpec((1, ts, d), lambda b, i: (b, i, 0)),
            pl.BlockSpec((1, halo, d), lambda b, i: (b, jnp.maximum(i * halo_per_tile - 1, 0), 0)),
            pl.BlockSpec((1, d), lambda b, i: (0, 0)),
            pl.BlockSpec((n_groups, group, group), lambda b, i: (0, 0, 0)),
            pl.BlockSpec((1, d), lambda b, i: (0, 0)),
        ],
        out_specs=pl.BlockSpec((1, ts, d), lambda b, i: (b, i, 0)),
        scratch_shapes=[pltpu.VMEM((ts + halo, d), F32)],
        compiler_params=pltpu.CompilerParams(
            dimension_semantics=("arbitrary", "arbitrary"),
            vmem_limit_bytes=_vmem_limit(pipelined, resident)),
        name="pool_mixer",
    )(x3, x3, gain, pool_w, scale)


def kernel(x, mix_norm, ffn_norm, final_norm, ab_w_in, lb_logits, hg_out_norm, ab_w_out,
           pool_w, pool_scale, ffn_w_gate, ffn_w_up, ffn_w_down):
    batch, seq, d = x.shape
    n_tok = batch * seq
    hg_width = HG_HEADS * HEAD_DIM
    sb_width = SB_HEADS * HEAD_DIM
    row = lambda a: a.reshape(1, -1)

    x2 = x.reshape(n_tok, d)
    ffn_weights = (ffn_w_gate, ffn_w_up, ffn_w_down)
    proj_a, qk, vt = _in_proj(x2, row(mix_norm[0]), ab_w_in.astype(BF16), 0, batch, seq, SB_BLOCK)
    o_a, *ffn_bf16 = _hgrn(proj_a.reshape(batch, seq, 4 * hg_width), lb_logits, row(hg_out_norm[0]),
                           batch, seq, 0, ffn_weights, (0, 1))
    ffn0_weights, ffn1_weights = ffn_bf16[:3], ffn_bf16[3:]
    o_b = _stick_breaking(qk.reshape(batch, seq, 2 * sb_width), vt, batch, seq)
    x2 = _out_proj(o_a.reshape(n_tok, hg_width), o_b.reshape(n_tok, sb_width), ab_w_out.astype(BF16), 0, x2)
    x2 = _ffn(x2, row(ffn_norm[0]), *ffn0_weights, 0, row(final_norm), False, "ffn0")

    x3 = _pool_mixer(x2.reshape(batch, seq, d), row(mix_norm[1]), pool_w[0].astype(BF16), row(pool_scale[0]))
    x2 = _ffn(x3.reshape(n_tok, d), row(ffn_norm[1]), *ffn1_weights, 0, row(final_norm), True, "ffn1_final_norm")
    return x2.reshape(batch, seq, d)
```
